```python
import jax
import jax.numpy as jnp
from jax import lax
import numpy as np

D_MODEL = 1024
BATCH = 2
SEQ = 8192
DEPTH = 2
DEC_BATCH = 128
DEC_SEQ = 8
PAST_LEN = 2048
PAGE_SIZE = 128

N_META = 16
N_MIXERS = 4
GROUP_W = D_MODEL // N_MIXERS
HEAD_DIM = 64
N_HG = GROUP_W // HEAD_DIM
MIX_W = N_MIXERS * GROUP_W
CHUNK = 128
SB_BLOCK = 128
SB_BIAS_INIT = -6.0
ROPE_BASE = 10000.0
W_LORA = 64
A_LORA = 64
G_LORA = 128
RW_W = 3 * GROUP_W + W_LORA + A_LORA + G_LORA
RW_SPLITS = (GROUP_W, 2 * GROUP_W, 3 * GROUP_W, 3 * GROUP_W + W_LORA, 3 * GROUP_W + W_LORA + A_LORA)
OFF_D = 11 * GROUP_W
IN_W = OFF_D + RW_W
N_GROUPS = 4
E_PER_GROUP = 8
N_EXPERTS = N_GROUPS * E_PER_GROUP
TOP_K = 2
D_EXPERT = 256
NORM_EPS = 1e-6
RWKV_LN_EPS = 64e-5

kernel_name = 'hybrid_parallel_heads_decode_step'


def rms_norm(x, g):
    xf = x.astype(jnp.float32)
    y = xf * lax.rsqrt(jnp.mean(xf * xf, axis=-1, keepdims=True) + NORM_EPS)
    return y.astype(x.dtype) * g


def group_norm(x, w, b):
    xf = x.astype(jnp.float32)
    mu = jnp.mean(xf, axis=-1, keepdims=True)
    var = jnp.mean(jnp.square(xf - mu), axis=-1, keepdims=True)
    return ((xf - mu) * lax.rsqrt(var + RWKV_LN_EPS)).astype(x.dtype) * w + b


def rope(x, pos):
    half = HEAD_DIM // 2
    inv = ROPE_BASE ** (-jnp.arange(half, dtype=jnp.float32) / half)
    ang = pos[:, None] * inv[None, :]
    cos = jnp.cos(ang)[None, :, None, :].astype(x.dtype)
    sin = jnp.sin(ang)[None, :, None, :].astype(x.dtype)
    x1, x2 = x[..., :half], x[..., half:]
    return jnp.concatenate([x1 * cos - x2 * sin, x1 * sin + x2 * cos], axis=-1)


def retention_log_gamma():
    return jnp.log1p(-jnp.exp2(-5.0 - jnp.arange(N_HG, dtype=jnp.float32)))


def gla_chunk(s0, q, k, v, logw):
    c = q.shape[1]
    bcum = jnp.cumsum(logw.astype(jnp.float32), axis=1)
    causal = jnp.tril(jnp.ones((c, c), dtype=bool))[None, :, :, None, None]
    decay = jnp.exp(jnp.where(causal, bcum[:, :, None] - bcum[:, None, :], -jnp.inf))
    scores = jnp.einsum('bthk,btshk,bshk->btsh', q, decay, k)
    o = jnp.einsum('btsh,bshv->bthv', scores, v) + jnp.einsum('bthk,bhkv->bthv', q * jnp.exp(bcum), s0)
    b_last = bcum[:, -1]
    s_new = s0 * jnp.exp(b_last)[..., None] + jnp.einsum('bshk,bshv->bhkv', k * jnp.exp(b_last[:, None] - bcum), v)
    return o.astype(v.dtype), s_new.astype(s0.dtype)


def linear_recurrence(q, k, v, logw, s0, with_meta_chunk):
    if not with_meta_chunk:
        return gla_chunk(s0, q, k, v, logw)
    o_meta, s = gla_chunk(s0, q[:, :N_META], k[:, :N_META], v[:, :N_META], logw[:, :N_META])
    b, t, h, _ = q.shape
    nc = (t - N_META) // CHUNK

    def to_chunks(a):
        return a[:, N_META:].reshape(b, nc, CHUNK, h, a.shape[-1]).swapaxes(0, 1)

    def body(state, xs):
        o, state = gla_chunk(state, *xs)
        return state, o

    s, o = lax.scan(body, s, (to_chunks(q), to_chunks(k), to_chunks(v), to_chunks(logw)))
    o = o.swapaxes(0, 1).reshape(b, t - N_META, h, v.shape[-1])
    return jnp.concatenate([o_meta, o], axis=1), s


def sb_attend(q, k, v, qpos, kpos, bias):
    z = jnp.einsum('bqhd,bshd->bhqs', q, k).astype(jnp.float32) * (HEAD_DIM ** -0.5)
    z = z + bias.astype(jnp.float32)[None, :, None, None]
    causal = kpos[None, :] < qpos[:, None]
    log_1mb = jnp.where(causal, jax.nn.log_sigmoid(-z), 0.0)
    after = lax.cumsum(log_1mb, axis=3, reverse=True) - log_1mb
    a = jnp.where(causal, jnp.exp(jax.nn.log_sigmoid(z) + after), 0.0)
    return jnp.einsum('bhqs,bshd->bqhd', a.astype(v.dtype), v)


def sb_prompt(q, k, v, bias):
    b, t, h, d = q.shape
    kpos = jnp.arange(t, dtype=jnp.int32)
    o_meta = sb_attend(q[:, :N_META], k[:, :N_META], v[:, :N_META], kpos[:N_META], kpos[:N_META], bias)
    nb = (t - N_META) // SB_BLOCK
    q_blocks = q[:, N_META:].reshape(b, nb, SB_BLOCK, h, d).swapaxes(0, 1)
    p_blocks = kpos[N_META:].reshape(nb, SB_BLOCK)
    o = lax.map(lambda qp: sb_attend(qp[0], k, v, qp[1], kpos, bias), (q_blocks, p_blocks))
    o = o.swapaxes(0, 1).reshape(b, t - N_META, h, v.shape[-1])
    return jnp.concatenate([o_meta, o], axis=1)


def rwkv_scan(s0, r, decay, k, v, kk, a):
    def step(s, xs):
        r_t, w_t, k_t, v_t, kk_t, a_t = xs
        sa = jnp.einsum('bhvk,bhk->bhv', s, -kk_t)
        s = (s * w_t[:, :, None, :] + sa[..., None] * (kk_t * a_t)[:, :, None, :]
             + v_t[..., None] * k_t[:, :, None, :]).astype(s0.dtype)
        return s, jnp.einsum('bhvk,bhk->bhv', s, r_t)

    xs = tuple(t.swapaxes(0, 1) for t in (r, decay, k, v, kk, a))
    s, o = lax.scan(step, s0, xs)
    return o.swapaxes(0, 1), s


def hier_moe(h, p):
    b, t, d = h.shape
    hf = h.reshape(b * t, d)
    g_logits = (hf @ p['moe_w_group'] + p['moe_b_group']).astype(jnp.float32)
    g_prob = jax.nn.softmax(g_logits, axis=-1)
    g_idx = jnp.argmax(g_logits, axis=-1)
    g_w = jnp.take_along_axis(g_prob, g_idx[:, None], axis=-1)
    e_logits = (jnp.einsum('nd,dge->nge', hf, p['moe_w_expert']) + p['moe_b_expert']).astype(jnp.float32)
    e_logits = jnp.take_along_axis(e_logits, g_idx[:, None, None], axis=1)[:, 0]
    top_v, top_i = lax.top_k(e_logits, TOP_K)
    top_w = jax.nn.softmax(top_v, axis=-1) * g_w
    eid = g_idx[:, None] * E_PER_GROUP + top_i
    gates = jnp.sum(jax.nn.one_hot(eid, N_EXPERTS, dtype=jnp.float32) * top_w[..., None], axis=1).astype(h.dtype)
    y = jnp.zeros_like(hf)
    for e in range(N_EXPERTS):
        hid = jax.nn.silu(hf @ p['moe_w1'][e]) * (hf @ p['moe_w3'][e])
        y = y + gates[:, e:e + 1] * (hid @ p['moe_w2'][e])
    return y.reshape(b, t, d)


def trunk_layer(x, p, lb, ret0, hgrn0, rwkv0, shift0, k_past=None, v_past=None):
    is_prompt = k_past is None
    b, t, _ = x.shape
    past = 0 if is_prompt else k_past.shape[1]
    ipos = past + jnp.arange(t, dtype=jnp.int32)
    fpos = ipos.astype(jnp.float32)

    def heads(a):
        return a.reshape(b, t, N_HG, HEAD_DIM)

    def flat(a):
        return a.reshape(b, t, GROUP_W)

    h = rms_norm(x, p['norm1'])
    proj = h @ p['w_in']
    qa, ka, va, ga, qb, fb, ib, gb, qc, kc, vc = jnp.split(proj[..., :OFF_D], 11, axis=-1)
    rw = proj[..., OFF_D:]

    q_ret = rope(heads(qa), fpos)
    k_ret = rope(heads(ka), fpos) * (HEAD_DIM ** -0.5)
    logw_ret = jnp.broadcast_to(retention_log_gamma()[:, None], (b, t, N_HG, HEAD_DIM))
    o_ret, ret_new = linear_recurrence(q_ret, k_ret, heads(va), logw_ret, ret0, is_prompt)
    o_a = flat(rms_norm(o_ret, p['ret_norm'].reshape(N_HG, HEAD_DIM))) * jax.nn.silu(ga)

    lbh = lb.reshape(N_HG, HEAD_DIM)
    fz = heads(fb).astype(jnp.float32)
    logf = jnp.logaddexp(jnp.log(lbh), jnp.log1p(-lbh) + jax.nn.log_sigmoid(fz))
    k_hg = ((1.0 - lbh) * jax.nn.sigmoid(-fz)).astype(x.dtype)
    o_hg, hgrn_new = linear_recurrence(heads(qb), k_hg, heads(ib), logf, hgrn0, is_prompt)
    o_b = flat(rms_norm(o_hg, p['hgrn_norm'].reshape(N_HG, HEAD_DIM))) * jax.nn.silu(gb)

    q_sb = rms_norm(heads(qc), p['sb_q_norm'])
    k_sb = rms_norm(heads(kc), p['sb_k_norm'])
    v_sb = heads(vc)
    if is_prompt:
        o_sb = sb_prompt(q_sb, k_sb, v_sb, p['sb_bias'])
    else:
        o_sb = sb_attend(q_sb, jnp.concatenate([k_past, k_sb], axis=1), jnp.concatenate([v_past, v_sb], axis=1),
                         ipos, jnp.arange(past + t, dtype=jnp.int32), p['sb_bias'])
    o_c = flat(o_sb)

    prev = jnp.concatenate([shift0[:, None].astype(rw.dtype), rw[:, :-1]], axis=1)
    xm = rw + (prev - rw) * p['rwkv_mu']
    r, k, v, wl, al, gl = jnp.split(xm, RW_SPLITS, axis=-1)
    w = (p['rwkv_w0'] + jnp.tanh(wl) @ p['rwkv_w2']).astype(jnp.float32)
    decay = jnp.exp(-jnp.exp(-jax.nn.softplus(-w) - 0.5))
    a = jax.nn.sigmoid(p['rwkv_a0'] + al @ p['rwkv_a2'])
    g = jax.nn.sigmoid(gl) @ p['rwkv_g2']
    kk = heads(k * p['rwkv_kk'])
    kk = kk * lax.rsqrt(jnp.maximum(jnp.sum(jnp.square(kk.astype(jnp.float32)), axis=-1, keepdims=True), 1e-12)).astype(kk.dtype)
    k_rw = heads(k * (1.0 + (a - 1.0) * p['rwkv_ka']))
    r_rw, v_rw, a_rw = heads(r), heads(v), heads(a)
    o_rw, rwkv_new = rwkv_scan(rwkv0, r_rw, heads(decay), k_rw, v_rw, kk, a_rw)
    bonus = jnp.sum(r_rw * k_rw * p['rwkv_rk'], axis=-1, keepdims=True) * v_rw
    o_d = flat(group_norm(o_rw, p['rwkv_ln_w'].reshape(N_HG, HEAD_DIM), p['rwkv_ln_b'].reshape(N_HG, HEAD_DIM)) + bonus) * g
    shift_new = rw[:, -1]

    x = x + jnp.concatenate([o_a, o_b, o_c, o_d], axis=-1) @ p['w_out']
    x = x + hier_moe(rms_norm(x, p['norm2']), p)
    return x, (k_sb, v_sb, ret_new, hgrn_new, rwkv_new, shift_new)


def setup_inputs(seed: int = 0) -> dict:
    key = jax.random.key(seed)
    ks = iter(jax.random.split(key, 48))

    def nrm(shape, scale=1.0):
        return scale * jax.random.normal(next(ks), shape, jnp.float32)

    def uni(shape, lo, hi):
        return jax.random.uniform(next(ks), shape, jnp.float32, lo, hi)

    def gain(shape):
        return 1.0 + nrm(shape, 0.02)

    n_pages = PAST_LEN // PAGE_SIZE
    n_pool = (5 * DEC_BATCH * n_pages) // 4
    st = (DEPTH, DEC_BATCH, N_HG, HEAD_DIM, HEAD_DIM)
    inputs = {
        'x_prompt': nrm((BATCH, SEQ, D_MODEL)),
        'x_sample': nrm((DEC_BATCH, DEC_SEQ, D_MODEL)),
        'cache_sb_k': nrm((DEPTH, n_pool, PAGE_SIZE, N_HG, HEAD_DIM)),
        'cache_sb_v': nrm((DEPTH, n_pool, PAGE_SIZE, N_HG, HEAD_DIM)),
        'state_ret': nrm(st, 0.5),
        'state_hgrn': nrm(st, 0.5),
        'state_rwkv': nrm(st, 0.5),
        'state_rwkv_shift': nrm((DEPTH, DEC_BATCH, RW_W)),
        'page_table': jax.random.permutation(next(ks), n_pool)[:DEC_BATCH * n_pages].reshape(DEC_BATCH, n_pages).astype(jnp.int32),
        'meta_tokens': nrm((N_META, D_MODEL)),
        'norm1': gain((DEPTH, D_MODEL)),
        'norm2': gain((DEPTH, D_MODEL)),
        'w_in': nrm((DEPTH, D_MODEL, IN_W), D_MODEL ** -0.5),
        'w_out': nrm((DEPTH, MIX_W, D_MODEL), MIX_W ** -0.5),
        'ret_norm': gain((DEPTH, GROUP_W)),
        'hgrn_lb_logits': nrm((DEPTH, GROUP_W), 0.5),
        'hgrn_norm': gain((DEPTH, GROUP_W)),
        'sb_q_norm': gain((DEPTH, HEAD_DIM)),
        'sb_k_norm': gain((DEPTH, HEAD_DIM)),
        'sb_bias': SB_BIAS_INIT + nrm((DEPTH, N_HG), 0.3),
        'rwkv_mu': uni((DEPTH, RW_W), 0.0, 1.0),
        'rwkv_w0': uni((DEPTH, GROUP_W), -6.0, 1.0),
        'rwkv_w2': nrm((DEPTH, W_LORA, GROUP_W), W_LORA ** -0.5),
        'rwkv_a0': nrm((DEPTH, GROUP_W), 0.1),
        'rwkv_a2': nrm((DEPTH, A_LORA, GROUP_W), A_LORA ** -0.5),
        'rwkv_g2': nrm((DEPTH, G_LORA, GROUP_W), G_LORA ** -0.5),
        'rwkv_kk': 0.85 + nrm((DEPTH, GROUP_W), 0.02),
        'rwkv_ka': gain((DEPTH, GROUP_W)),
        'rwkv_rk': nrm((DEPTH, N_HG, HEAD_DIM), 0.1),
        'rwkv_ln_w': gain((DEPTH, GROUP_W)),
        'rwkv_ln_b': nrm((DEPTH, GROUP_W), 0.01),
        'moe_w_group': nrm((DEPTH, D_MODEL, N_GROUPS), D_MODEL ** -0.5),
        'moe_b_group': nrm((DEPTH, N_GROUPS), 0.01),
        'moe_w_expert': nrm((DEPTH, D_MODEL, N_GROUPS, E_PER_GROUP), D_MODEL ** -0.5),
        'moe_b_expert': nrm((DEPTH, N_GROUPS, E_PER_GROUP), 0.01),
        'moe_w1': nrm((DEPTH, N_EXPERTS, D_MODEL, D_EXPERT), D_MODEL ** -0.5),
        'moe_w3': nrm((DEPTH, N_EXPERTS, D_MODEL, D_EXPERT), D_MODEL ** -0.5),
        'moe_w2': nrm((DEPTH, N_EXPERTS, D_EXPERT, D_MODEL), D_EXPERT ** -0.5),
    }
    return inputs


def reference(x_prompt, x_sample, cache_sb_k, cache_sb_v, state_ret, state_hgrn, state_rwkv, state_rwkv_shift,
              page_table, meta_tokens, norm1, norm2, w_in, w_out, ret_norm, hgrn_lb_logits, hgrn_norm,
              sb_q_norm, sb_k_norm, sb_bias, rwkv_mu, rwkv_w0, rwkv_w2, rwkv_a0, rwkv_a2, rwkv_g2, rwkv_kk, rwkv_ka,
              rwkv_rk, rwkv_ln_w, rwkv_ln_b, moe_w_group, moe_b_group, moe_w_expert, moe_b_expert,
              moe_w1, moe_w3, moe_w2):
    bp = x_prompt.shape[0]
    db = x_sample.shape[0]
    past_len = page_table.shape[1] * cache_sb_k.shape[2]
    xp = jnp.concatenate([jnp.broadcast_to(meta_tokens[None], (bp, N_META, D_MODEL)).astype(x_prompt.dtype), x_prompt], axis=1)
    xs = x_sample
    lb_cum = jnp.cumsum(jax.nn.softmax(hgrn_lb_logits.astype(jnp.float32), axis=0), axis=0)
    lb_all = lb_cum - lb_cum[0:1]
    zero_state = jnp.zeros((bp, N_HG, HEAD_DIM, HEAD_DIM), xp.dtype)
    zero_shift = jnp.zeros((bp, RW_W), xp.dtype)
    prompt_rows = []
    sample_rows = []
    for l in range(DEPTH):
        p = {'norm1': norm1[l], 'norm2': norm2[l], 'w_in': w_in[l], 'w_out': w_out[l],
             'ret_norm': ret_norm[l], 'hgrn_norm': hgrn_norm[l], 'sb_q_norm': sb_q_norm[l], 'sb_k_norm': sb_k_norm[l],
             'sb_bias': sb_bias[l],
             'rwkv_mu': rwkv_mu[l], 'rwkv_w0': rwkv_w0[l], 'rwkv_w2': rwkv_w2[l], 'rwkv_a0': rwkv_a0[l],
             'rwkv_a2': rwkv_a2[l], 'rwkv_g2': rwkv_g2[l], 'rwkv_kk': rwkv_kk[l], 'rwkv_ka': rwkv_ka[l],
             'rwkv_rk': rwkv_rk[l], 'rwkv_ln_w': rwkv_ln_w[l], 'rwkv_ln_b': rwkv_ln_b[l],
             'moe_w_group': moe_w_group[l], 'moe_b_group': moe_b_group[l], 'moe_w_expert': moe_w_expert[l],
             'moe_b_expert': moe_b_expert[l], 'moe_w1': moe_w1[l], 'moe_w3': moe_w3[l], 'moe_w2': moe_w2[l]}
        xp, rows_p = trunk_layer(xp, p, lb_all[l], zero_state, zero_state, zero_state, zero_shift)
        prompt_rows.append(rows_p)
        k_past = cache_sb_k[l][page_table].reshape(db, past_len, N_HG, HEAD_DIM)
        v_past = cache_sb_v[l][page_table].reshape(db, past_len, N_HG, HEAD_DIM)
        xs, rows_s = trunk_layer(xs, p, lb_all[l], state_ret[l], state_hgrn[l], state_rwkv[l], state_rwkv_shift[l],
                                 k_past, v_past)
        sample_rows.append(rows_s)
    y_prompt = xp[:, N_META:]
    y_sample = xs
    k_p, v_p, ret_p, hgrn_p, rwkv_p, shift_p = [jnp.stack(r) for r in zip(*prompt_rows)]
    k_s, v_s, ret_s, hgrn_s, rwkv_s, shift_s = [jnp.stack(r) for r in zip(*sample_rows)]
    return (y_prompt, y_sample, k_p, v_p, ret_p, hgrn_p, rwkv_p, shift_p, k_s, v_s, ret_s, hgrn_s, rwkv_s, shift_s)
```

```python
import functools
import math

import numpy as np
import jax
import jax.numpy as jnp
from jax import lax
from jax.experimental import pallas as pl
from jax.experimental.pallas import tpu as pltpu

F32 = jnp.float32
BF16 = jnp.bfloat16

HEAD_DIM = 64
N_HG = 4
GROUP_W = N_HG * HEAD_DIM
N_META = 16
LANE = 128
ROW_BLK = 128
PAD_FRONT = ROW_BLK - N_META
LIN_COLS = 11 * GROUP_W
NORM_EPS = 1e-6
RWKV_LN_EPS = 64e-5
ROPE_BASE = 10000.0
E_PER_GROUP = 8
VMEM_LIMIT = 56 * 1024 * 1024


def _iota(shape, dim):
    return lax.broadcasted_iota(jnp.int32, shape, dim)


def _mask01(cond):
    return jnp.where(cond, 1.0, 0.0).astype(BF16)


def _dot(a, b):
    return jnp.dot(a.astype(BF16), b.astype(BF16), preferred_element_type=F32)


def _dot_nt(a, b):
    return lax.dot_general(a.astype(BF16), b.astype(BF16), (((1,), (1,)), ((), ())),
                           preferred_element_type=F32)


def _dot_tn(a, b):
    return lax.dot_general(a.astype(BF16), b.astype(BF16), (((0,), (0,)), ((), ())),
                           preferred_element_type=F32)


def _split(x, n):
    parts = []
    r = x
    for _ in range(n):
        h = r.astype(BF16)
        parts.append(h)
        r = r - h.astype(F32)
    return parts


def _dotx(x, m, n=2):
    out = None
    for p in _split(x, n):
        t = jnp.dot(p, m, preferred_element_type=F32)
        out = t if out is None else out + t
    return out


def _xdot(m, x, n=3):
    out = None
    for p in _split(x, n):
        t = jnp.dot(m, p, preferred_element_type=F32)
        out = t if out is None else out + t
    return out


def _head_ones(w):
    return _mask01((_iota((w, w), 0) >> 6) == (_iota((w, w), 1) >> 6))


def _sigmoid(x):
    return 1.0 / (1.0 + jnp.exp(-x))


def _softplus(x):
    return jnp.maximum(x, 0.0) + jnp.log1p(jnp.exp(-jnp.abs(x)))


def _head_rms(x, gain, ones):
    ms = _dotx(x * x, ones, 2) * (1.0 / HEAD_DIM)
    return x * lax.rsqrt(ms + NORM_EPS) * gain


def _row_tile(n):
    for t in (512, 256, 128):
        if n % t == 0:
            return t
    raise ValueError(f"row count {n} is not a multiple of {ROW_BLK}")


def _cparams(sem):
    return pltpu.CompilerParams(dimension_semantics=sem, vmem_limit_bytes=VMEM_LIMIT)


def _in_proj_kernel(x_ref, g_ref, wa_ref, wb_ref, oa_ref, ob_ref):
    x = x_ref[...]
    ms = jnp.mean(x * x, axis=-1, keepdims=True)
    h = (x * lax.rsqrt(ms + NORM_EPS) * g_ref[...]).astype(BF16)
    oa_ref[...] = jnp.dot(h, wa_ref[...], preferred_element_type=F32)
    ob_ref[...] = jnp.dot(h, wb_ref[...], preferred_element_type=F32)


def _in_proj(x, g, wa, wb):
    n, d = x.shape
    tm = 256 if n % 256 == 0 else ROW_BLK
    return pl.pallas_call(
        _in_proj_kernel,
        grid=(n // tm,),
        in_specs=[pl.BlockSpec((tm, d), lambda i: (i, 0)),
                  pl.BlockSpec((1, d), lambda i: (0, 0)),
                  pl.BlockSpec(wa.shape, lambda i: (0, 0)),
                  pl.BlockSpec(wb.shape, lambda i: (0, 0))],
        out_specs=[pl.BlockSpec((tm, wa.shape[1]), lambda i: (i, 0)),
                   pl.BlockSpec((tm, wb.shape[1]), lambda i: (i, 0))],
        out_shape=[jax.ShapeDtypeStruct((n, wa.shape[1]), F32),
                   jax.ShapeDtypeStruct((n, wb.shape[1]), F32)],
        compiler_params=_cparams(("parallel",)),
        name="in_proj",
    )(x, g, wa, wb)


def _gla_block(q, k, v, lw, states, sub, chain):
    rows, w = q.shape
    nsub = rows // sub
    shift = int(math.log2(sub))
    r_i = _iota((rows, rows), 0)
    c_i = _iota((rows, rows), 1)
    same = (r_i >> shift) == (c_i >> shift)
    tri = _mask01(same & (c_i <= r_i))
    blk = _mask01(same)
    lparts = _split(lw, 3)
    bc = None
    bl = None
    for p in lparts:
        t1 = jnp.dot(tri, p, preferred_element_type=F32)
        t2 = jnp.dot(blk, p, preferred_element_type=F32)
        bc = t1 if bc is None else bc + t1
        bl = t2 if bl is None else bl + t2
    qt = q * jnp.exp(bc)
    kh = k * jnp.exp(bl - bc)
    ones = _head_ones(w)

    local = _iota((rows, w), 0) & (sub - 1)
    o_intra = jnp.zeros((rows, w), F32)
    for dist in range(sub):
        back = (lambda x: x) if dist == 0 else (lambda x: pltpu.roll(x, dist, 0))
        e = jnp.exp(jnp.where(local >= dist, bc - back(bc), -1e30))
        p = q * back(k) * e
        o_intra = o_intra + _dot(p, ones) * back(v)

    bdmask = (_iota((w, w), 0) >> 6) == (_iota((w, w), 1) >> 6)
    ones_sw = jnp.ones((sub, w), BF16)
    outs = []
    new_states = []
    s = states if chain else None
    for i in range(nsub):
        sl = slice(i * sub, (i + 1) * sub)
        if not chain:
            s = states[i]
        outs.append(_dot(qt[sl], s))
        dsum = None
        for p in lparts:
            t = lax.dot_general(p[sl], ones_sw, (((0,), (0,)), ((), ())), preferred_element_type=F32)
            dsum = t if dsum is None else dsum + t
        kv = _dot_tn(kh[sl], v[sl])
        s = s * jnp.exp(dsum) + jnp.where(bdmask, kv, 0.0)
        if not chain:
            new_states.append(s)
    o = o_intra + jnp.concatenate(outs, axis=0)
    return o, (s if chain else new_states)


def _lin_mix_math(p, cos, sin, lgam, retw, lb_logits, hgw, qnw, knw, layer, st_ret, st_hg, sub, chain):
    w = GROUP_W
    qa, ka, va, ga, qb, fb, ib, gb, qc, kc, vc = [p[:, i * w:(i + 1) * w] for i in range(11)]
    rows = p.shape[0]
    ones = _head_ones(w)
    lane = _iota((rows, LANE), 1)
    first_half = (lane & (HEAD_DIM - 1)) < (HEAD_DIM // 2)

    def rope(x):
        halves = []
        for hp in range(w // LANE):
            xh = x[:, hp * LANE:(hp + 1) * LANE]
            rot = jnp.where(first_half, -pltpu.roll(xh, LANE - HEAD_DIM // 2, 1), pltpu.roll(xh, HEAD_DIM // 2, 1))
            halves.append(xh * cos + rot * sin)
        return jnp.concatenate(halves, axis=1)

    q_ret = rope(qa)
    k_ret = rope(ka) * (HEAD_DIM ** -0.5)
    o_ret, st_ret = _gla_block(q_ret, k_ret, va, lgam, st_ret, sub, chain)
    o_a = _head_rms(o_ret, retw, ones) * (ga * _sigmoid(ga))

    lg = [lb_logits[d:d + 1, :] for d in range(lb_logits.shape[0])]
    mx = functools.reduce(jnp.maximum, lg)
    ex = [jnp.exp(row - mx) for row in lg]
    lb = sum(ex[1:layer + 1], jnp.zeros_like(mx)) / sum(ex[1:], ex[0])
    log_sig = jnp.minimum(fb, 0.0) - jnp.log1p(jnp.exp(-jnp.abs(fb)))
    t_a = jnp.broadcast_to(jnp.log(lb), fb.shape)
    t_b = jnp.log1p(-lb) + log_sig
    logf = jnp.maximum(t_a, t_b) + jnp.log1p(jnp.exp(-jnp.abs(t_a - t_b)))
    k_hg = (1.0 - lb) * _sigmoid(-fb)
    o_hg, st_hg = _gla_block(qb, k_hg, ib, logf, st_hg, sub, chain)
    o_b = _head_rms(o_hg, hgw, ones) * (gb * _sigmoid(gb))

    qn = _head_rms(qc, qnw, ones)
    kn = _head_rms(kc, knw, ones)
    return o_a, o_b, qn, kn, vc, st_ret, st_hg


def _lin_mix_prompt_kernel(p_ref, cos_ref, sin_ref, lgam_ref, retw_ref, lbl_ref, hgw_ref, qnw_ref, knw_ref,
                           oab_ref, qs_ref, knf_ref, knb_ref, vb_ref, sret_ref, shg_ref,
                           st_ret, st_hg, *, layer, sub):
    i = pl.program_id(1)

    @pl.when(i == 0)
    def _():
        st_ret[...] = jnp.zeros_like(st_ret)
        st_hg[...] = jnp.zeros_like(st_hg)

    o_a, o_b, qn, kn, vc, s1, s2 = _lin_mix_math(
        p_ref[...], cos_ref[...], sin_ref[...], lgam_ref[...], retw_ref[...], lbl_ref, hgw_ref[...],
        qnw_ref[...], knw_ref[...], layer, st_ret[...], st_hg[...], sub, True)
    st_ret[...] = s1
    st_hg[...] = s2
    oab_ref[...] = jnp.concatenate([o_a, o_b], axis=1)
    qs_ref[...] = (qn * (HEAD_DIM ** -0.5)).astype(BF16)
    knf_ref[...] = kn
    knb_ref[...] = kn.astype(BF16)
    vb_ref[...] = vc.astype(BF16)

    @pl.when(i == pl.num_programs(1) - 1)
    def _():
        sret_ref[...] = s1
        shg_ref[...] = s2


def _lin_mix_prompt(proj, cos, sin, lgam, retw, lbl, hgw, qnw, knw, layer, nb, nblk):
    n = proj.shape[0]
    w = GROUP_W
    row = lambda b, i: (b * nblk + i, 0)
    const = lambda b, i: (0, 0)
    outs = pl.pallas_call(
        functools.partial(_lin_mix_prompt_kernel, layer=layer, sub=16),
        grid=(nb, nblk),
        in_specs=[pl.BlockSpec((ROW_BLK, LIN_COLS), row),
                  pl.BlockSpec((ROW_BLK, LANE), lambda b, i: (i, 0)),
                  pl.BlockSpec((ROW_BLK, LANE), lambda b, i: (i, 0)),
                  pl.BlockSpec((ROW_BLK, w), const), pl.BlockSpec((1, w), const),
                  pl.BlockSpec(lbl.shape, const), pl.BlockSpec((1, w), const),
                  pl.BlockSpec((1, w), const), pl.BlockSpec((1, w), const)],
        out_specs=[pl.BlockSpec((ROW_BLK, 2 * w), row),
                   pl.BlockSpec((ROW_BLK, w), row), pl.BlockSpec((ROW_BLK, w), row),
                   pl.BlockSpec((ROW_BLK, w), row), pl.BlockSpec((ROW_BLK, w), row),
                   pl.BlockSpec((None, w, w), lambda b, i: (b, 0, 0)),
                   pl.BlockSpec((None, w, w), lambda b, i: (b, 0, 0))],
        out_shape=[jax.ShapeDtypeStruct((n, 2 * w), F32),
                   jax.ShapeDtypeStruct((n, w), BF16), jax.ShapeDtypeStruct((n, w), F32),
                   jax.ShapeDtypeStruct((n, w), BF16), jax.ShapeDtypeStruct((n, w), BF16),
                   jax.ShapeDtypeStruct((nb, w, w), F32), jax.ShapeDtypeStruct((nb, w, w), F32)],
        scratch_shapes=[pltpu.VMEM((w, w), F32), pltpu.VMEM((w, w), F32)],
        compiler_params=_cparams(("parallel", "arbitrary")),
        name="lin_mix_prompt",
    )(proj, cos, sin, lgam, retw, lbl, hgw, qnw, knw)
    return outs


def _lin_mix_sample_kernel(p_ref, cos_ref, sin_ref, lgam_ref, retw_ref, lbl_ref, hgw_ref, qnw_ref, knw_ref,
                           sret_in, shg_in,
                           oab_ref, qs_ref, knf_ref, vf_ref, sret_out, shg_out, *, layer, sub):
    w = GROUP_W
    nseq = ROW_BLK // sub
    bdmask = (_iota((w, w), 0) >> 6) == (_iota((w, w), 1) >> 6)
    rep = _mask01(_iota((HEAD_DIM, w), 0) == (_iota((HEAD_DIM, w), 1) & (HEAD_DIM - 1)))
    rep_t = _mask01((_iota((w, HEAD_DIM), 0) & (HEAD_DIM - 1)) == _iota((w, HEAD_DIM), 1))

    def expand(ref):
        return [jnp.where(bdmask, _dotx(ref[j], rep, 3), 0.0) for j in range(nseq)]

    def extract(ref, states):
        for j in range(nseq):
            ref[j] = _dotx(states[j], rep_t, 3)

    o_a, o_b, qn, kn, vc, s1, s2 = _lin_mix_math(
        p_ref[...], cos_ref[...], sin_ref[...], lgam_ref[...], retw_ref[...], lbl_ref, hgw_ref[...],
        qnw_ref[...], knw_ref[...], layer, expand(sret_in), expand(shg_in), sub, False)
    extract(sret_out, s1)
    extract(shg_out, s2)
    oab_ref[...] = jnp.concatenate([o_a, o_b], axis=1)
    qs_ref[...] = qn * (HEAD_DIM ** -0.5)
    knf_ref[...] = kn
    vf_ref[...] = vc


def _lin_mix_sample(proj, cos, sin, lgam, retw, lbl, hgw, qnw, knw, s_ret, s_hg, layer, dec_seq):
    n = proj.shape[0]
    w = GROUP_W
    nseq = ROW_BLK // dec_seq
    row = lambda i: (i, 0)
    const = lambda i: (0, 0)
    st_spec = pl.BlockSpec((nseq, w, HEAD_DIM), lambda i: (i, 0, 0))
    return pl.pallas_call(
        functools.partial(_lin_mix_sample_kernel, layer=layer, sub=dec_seq),
        grid=(n // ROW_BLK,),
        in_specs=[pl.BlockSpec((ROW_BLK, LIN_COLS), row),
                  pl.BlockSpec((ROW_BLK, LANE), const), pl.BlockSpec((ROW_BLK, LANE), const),
                  pl.BlockSpec((ROW_BLK, w), const), pl.BlockSpec((1, w), const),
                  pl.BlockSpec(lbl.shape, const), pl.BlockSpec((1, w), const),
                  pl.BlockSpec((1, w), const), pl.BlockSpec((1, w), const),
                  st_spec, st_spec],
        out_specs=[pl.BlockSpec((ROW_BLK, 2 * w), row),
                   pl.BlockSpec((ROW_BLK, w), row), pl.BlockSpec((ROW_BLK, w), row),
                   pl.BlockSpec((ROW_BLK, w), row), st_spec, st_spec],
        out_shape=[jax.ShapeDtypeStruct((n, 2 * w), F32),
                   jax.ShapeDtypeStruct((n, w), F32), jax.ShapeDtypeStruct((n, w), F32),
                   jax.ShapeDtypeStruct((n, w), F32),
                   jax.ShapeDtypeStruct(s_ret.shape, F32), jax.ShapeDtypeStruct(s_hg.shape, F32)],
        compiler_params=_cparams(("parallel",)),
        name="lin_mix_sample",
    )(proj, cos, sin, lgam, retw, lbl, hgw, qnw, knw, s_ret, s_hg)


def _sb_consts(rows, tmask):
    j_i = _iota((LANE, 2 * LANE), 0)
    s_i = _iota((LANE, 2 * LANE), 1)
    ucat = _mask01((s_i >= LANE) | (j_i >= s_i))
    causal = _iota((rows, LANE), 1) < (_iota((rows, LANE), 0) & tmask)
    return ucat, causal


def _sb_block(z, carry, ucat, causal):
    sp = _softplus(z)
    l = -sp
    if causal is not None:
        l = jnp.where(causal, l, 0.0)
    t = _dotx(l, ucat, 2)
    after = carry + (t[:, :LANE] - l)
    a = jnp.exp((z - sp) + after)
    if causal is not None:
        a = jnp.where(causal, a, 0.0)
    return a, carry + t[:, LANE:]


def _sb_prompt_kernel(bias_ref, q_ref, k_ref, v_ref, o_ref):
    i = pl.program_id(1)
    q = q_ref[...]
    left = _iota((ROW_BLK, LANE), 1) < HEAD_DIM
    zero = jnp.zeros((ROW_BLK, LANE), BF16)
    qstack = []
    for p in range(2):
        qp = q[:, p * LANE:(p + 1) * LANE]
        qstack.append(jnp.concatenate([jnp.where(left, qp, zero), jnp.where(left, zero, qp)], axis=0))
    bias_col = jnp.concatenate([jnp.full((ROW_BLK, 1), bias_ref[h], F32) for h in range(N_HG)], axis=0)
    rows = N_HG * ROW_BLK
    ucat, causal = _sb_consts(rows, ROW_BLK - 1)

    def step(kb, carry, acc0, acc1, mask):
        start = pl.multiple_of(kb * ROW_BLK, ROW_BLK)
        kblk = k_ref[pl.ds(start, ROW_BLK), :]
        vblk = v_ref[pl.ds(start, ROW_BLK), :]
        z = jnp.concatenate([_dot_nt(qstack[p], kblk[:, p * LANE:(p + 1) * LANE]) for p in range(2)], axis=0) + bias_col
        a, carry = _sb_block(z, carry, ucat, mask)
        ab = a.astype(BF16)
        accs = []
        for p, acc in enumerate((acc0, acc1)):
            r = jnp.dot(ab[p * 2 * ROW_BLK:(p + 1) * 2 * ROW_BLK], vblk[:, p * LANE:(p + 1) * LANE],
                        preferred_element_type=F32)
            accs.append(acc + jnp.where(left, r[:ROW_BLK], r[ROW_BLK:]))
        return carry, accs[0], accs[1]

    zeros = jnp.zeros((ROW_BLK, LANE), F32)
    state = step(i, jnp.zeros((rows, LANE), F32), zeros, zeros, causal)
    state = lax.fori_loop(0, i, lambda j, c: step(i - 1 - j, c[0], c[1], c[2], None), state)
    o_ref[...] = jnp.concatenate([state[1], state[2]], axis=1)


def _sb_prompt(bias, qs, kb, vb, nb, nblk):
    n, w = qs.shape
    tp = nblk * ROW_BLK
    return pl.pallas_call(
        _sb_prompt_kernel,
        grid=(nb, nblk),
        in_specs=[pl.BlockSpec(memory_space=pltpu.SMEM),
                  pl.BlockSpec((ROW_BLK, w), lambda b, i: (b * nblk + i, 0)),
                  pl.BlockSpec((tp, w), lambda b, i: (b, 0)),
                  pl.BlockSpec((tp, w), lambda b, i: (b, 0))],
        out_specs=pl.BlockSpec((ROW_BLK, w), lambda b, i: (b * nblk + i, 0)),
        out_shape=jax.ShapeDtypeStruct((n, w), F32),
        compiler_params=_cparams(("parallel", "arbitrary")),
        name="sb_prompt",
    )(bias, qs, kb, vb)


def _sb_sample_kernel(pt_ref, bias_ref, q_ref, kn_ref, vn_ref, *rest, n_pages, dec_seq):
    del pt_ref
    k_pages = rest[:n_pages]
    v_pages = rest[n_pages:2 * n_pages]
    o_ref = rest[2 * n_pages]
    w = GROUP_W
    rows = N_HG * dec_seq
    q = q_ref[...]
    head_of_lane = _iota((dec_seq, w), 1) >> 6
    qbd = jnp.concatenate([jnp.where(head_of_lane == h, q, 0.0) for h in range(N_HG)], axis=0).astype(BF16)
    bias_col = jnp.concatenate([jnp.full((dec_seq, 1), bias_ref[h], F32) for h in range(N_HG)], axis=0)
    ucat, causal = _sb_consts(rows, dec_seq - 1)
    pad = jnp.zeros((LANE - dec_seq, w), F32)
    k_new = jnp.concatenate([kn_ref[...], pad], axis=0)
    v_new = jnp.concatenate([vn_ref[...], pad], axis=0)
    a, carry = _sb_block(_dot_nt(qbd, k_new) + bias_col, jnp.zeros((rows, LANE), F32), ucat, causal)
    acc = _dot(a, v_new)
    for p in reversed(range(n_pages)):
        a, carry = _sb_block(_dot_nt(qbd, k_pages[p][...]) + bias_col, carry, ucat, None)
        acc = acc + _dot(a, v_pages[p][...])
    out = jnp.zeros((dec_seq, w), F32)
    for h in range(N_HG):
        out = out + jnp.where(head_of_lane == h, acc[h * dec_seq:(h + 1) * dec_seq], 0.0)
    o_ref[...] = out


def _sb_sample(page_table, bias, qs, kn, vn, cache_k, cache_v, layer, dec_seq):
    n, w = qs.shape
    db, n_pages = page_table.shape
    page = cache_k.shape[2]
    assert page == LANE and dec_seq % 8 == 0
    row = lambda b, pt: (b, 0)
    page_specs = [pl.BlockSpec((None, None, page, w), lambda b, pt, p=p: (layer, pt[b, p], 0, 0))
                  for p in range(n_pages)]
    grid_spec = pltpu.PrefetchScalarGridSpec(
        num_scalar_prefetch=1,
        grid=(db,),
        in_specs=[pl.BlockSpec(memory_space=pltpu.SMEM),
                  pl.BlockSpec((dec_seq, w), row), pl.BlockSpec((dec_seq, w), row), pl.BlockSpec((dec_seq, w), row)]
                 + page_specs + page_specs,
        out_specs=pl.BlockSpec((dec_seq, w), row),
    )
    return pl.pallas_call(
        functools.partial(_sb_sample_kernel, n_pages=n_pages, dec_seq=dec_seq),
        grid_spec=grid_spec,
        out_shape=jax.ShapeDtypeStruct((n, w), F32),
        compiler_params=_cparams(("arbitrary",)),
        name="sb_sample",
    )(page_table, bias, qs, kn, vn, *([cache_k] * n_pages), *([cache_v] * n_pages))


def _rwkv_prologue(rw, prev, mu, w0, w2p, a0, a2p, g2, kkp, ka, rk, ones):
    w = GROUP_W
    xm = rw + (prev - rw) * mu
    r = xm[:, 0:w]
    k = xm[:, w:2 * w]
    v = xm[:, 2 * w:3 * w]
    wa = xm[:, 3 * w:3 * w + LANE]
    gl = xm[:, 3 * w + LANE:]
    wd = w0 + _dot(jnp.tanh(wa), w2p)
    decay = jnp.exp(-jnp.exp(-_softplus(-wd) - 0.5))
    a = _sigmoid(a0 + _dot(wa, a2p))
    g = _dot(_sigmoid(gl), g2)
    kk = k * kkp
    kk = kk * lax.rsqrt(jnp.maximum(_dotx(kk * kk, ones, 2), 1e-12))
    k_rw = k * (1.0 + (a - 1.0) * ka)
    bonus = _dotx(r * k_rw * rk, ones, 2) * v
    return r, decay, k_rw, v, kk, kk * a, g, bonus


def _rwkv_scan(tiles, tseq, s0, op_refs, vt_ref, acc_ref):
    g = len(tiles)
    hd = HEAD_DIM
    kk_ref, w_ref, bk_ref, kr_ref, rr_ref = op_refs
    ones_h = _head_ones(LANE)
    ones_f = jnp.ones((LANE, LANE), BF16)
    j2 = _mask01((_iota((LANE, 2 * LANE), 0) >> 6) == (_iota((LANE, 2 * LANE), 1) >> 7))
    lane_t = _iota((hd, LANE), 1)
    left = lane_t < hd
    x0 = [vt_ref[rb, p, 0:hd, :] for (rb, _, p) in tiles]
    x1 = [vt_ref[rb, p, hd:2 * hd, :] for (rb, _, p) in tiles]
    tile_rows = lambda a, i: a[i * hd:(i + 1) * hd]
    step_rows = 8

    def group(t8, s):
        base = t8 * step_rows
        blks = []
        for ref in op_refs:
            per_tile = []
            for (rb, j, p) in tiles:
                start = rb * ROW_BLK + j * tseq + base
                if not isinstance(start, int):
                    start = pl.multiple_of(start, step_rows)
                per_tile.append(ref[pl.ds(start, step_rows), pl.ds(p * LANE, LANE)])
            blks.append(per_tile)
        for u in range(step_rows):
            row = lambda q, i: blks[q][i][u:u + 1, :]
            s_t = [tile_rows(s, i) for i in range(g)]
            sa = _dotx(jnp.concatenate([s_t[i] * row(0, i) for i in range(g)], axis=0), ones_h, 2)
            msk = [lane_t == (j * tseq + base + u) for (_, j, _) in tiles]
            vsel = jnp.concatenate([jnp.where(msk[i], x0[i], 0.0) for i in range(g)]
                                   + [jnp.where(msk[i], x1[i], 0.0) for i in range(g)], axis=0)
            vc = _dotx(vsel, ones_f, 2)
            new = []
            for i in range(g):
                vcol = jnp.where(left, tile_rows(vc, i), tile_rows(vc, g + i))
                new.append(s_t[i] * row(1, i) - tile_rows(sa, i) * row(2, i) + vcol * row(3, i))
            ro = _dot(jnp.concatenate([new[i] * row(4, i) for i in range(g)], axis=0), j2)
            for i, (rb, _, p) in enumerate(tiles):
                r_i = tile_rows(ro, i)
                acc_ref[rb, p, 0:hd, :] = jnp.where(msk[i], r_i[:, :LANE], acc_ref[rb, p, 0:hd, :])
                acc_ref[rb, p, hd:2 * hd, :] = jnp.where(msk[i], r_i[:, LANE:], acc_ref[rb, p, hd:2 * hd, :])
            s = jnp.concatenate(new, axis=0)
        return s

    assert tseq % step_rows == 0
    if tseq == step_rows:
        return group(0, s0)
    return lax.fori_loop(0, tseq // step_rows, group, s0)


def _rwkv_kernel(*refs, nrb, tseq, prompt):
    if prompt:
        (rw_ref, mu_ref, w0_ref, w2p_ref, a0_ref, a2p_ref, g2_ref, kkp_ref, ka_ref, rk_ref, lnw_ref, lnb_ref,
         o_ref, s_out_ref,
         kk_s, w_s, bk_s, kr_s, rr_s, g_s, bonus_s, vt_s, acc_s, carry_s, state_s) = refs
        ovr_ref = s_in_ref = None
    else:
        (rw_ref, ovr_ref, s_in_ref, mu_ref, w0_ref, w2p_ref, a0_ref, a2p_ref, g2_ref, kkp_ref, ka_ref, rk_ref,
         lnw_ref, lnb_ref, o_ref, s_out_ref,
         kk_s, w_s, bk_s, kr_s, rr_s, g_s, bonus_s, vt_s, acc_s) = refs
        carry_s = state_s = None
    i = pl.program_id(0)
    w = GROUP_W
    hd = HEAD_DIM
    ones = _head_ones(w)
    nseq = ROW_BLK // tseq

    if prompt:
        @pl.when(i == 0)
        def _():
            carry_s[...] = jnp.zeros_like(carry_s)
            state_s[...] = jnp.zeros_like(state_s)

    row_i = _iota((ROW_BLK, 1), 0)
    for rb in range(nrb):
        rw = rw_ref[rb]
        rolled = pltpu.roll(rw, 1, 0)
        if prompt:
            prev = jnp.where(row_i == 0, carry_s[rb, 0:1, :], rolled)
            carry_s[rb, 0:1, :] = rw[ROW_BLK - 1:ROW_BLK, :]
        else:
            prev = jnp.where((row_i & (tseq - 1)) == 0, ovr_ref[rb], rolled)
        r, decay, k_rw, v, kk, bk, g, bonus = _rwkv_prologue(
            rw, prev, mu_ref[...], w0_ref[...], w2p_ref[...], a0_ref[...], a2p_ref[...], g2_ref[...],
            kkp_ref[...], ka_ref[...], rk_ref[...], ones)
        sl = pl.ds(rb * ROW_BLK, ROW_BLK)
        kk_s[sl, :] = kk
        w_s[sl, :] = decay
        bk_s[sl, :] = bk
        kr_s[sl, :] = k_rw
        rr_s[sl, :] = r
        g_s[sl, :] = g
        bonus_s[sl, :] = bonus
        for p in range(2):
            vt_s[rb, p] = v[:, p * LANE:(p + 1) * LANE].T
    acc_s[...] = jnp.zeros_like(acc_s)

    ops = (kk_s, w_s, bk_s, kr_s, rr_s)
    if prompt:
        tiles = [(rb, 0, p) for rb in range(nrb) for p in range(2)]
        s = _rwkv_scan(tiles, tseq, state_s[...], ops, vt_s, acc_s)
        state_s[...] = s

        @pl.when(i == pl.num_programs(0) - 1)
        def _():
            s_out_ref[...] = s
    else:
        per = 2
        for grp in range(nseq // per):
            tiles = [(0, grp * per + jj, p) for jj in range(per) for p in range(2)]
            rows = pl.ds(grp * per * 2 * hd, per * 2 * hd)
            s_out_ref[rows, :] = _rwkv_scan(tiles, tseq, s_in_ref[rows, :], ops, vt_s, acc_s)

    for rb in range(nrb):
        o = jnp.concatenate([acc_s[rb, p].T for p in range(2)], axis=1)
        mu_h = _dotx(o, ones, 2) * (1.0 / hd)
        dlt = o - mu_h
        var = _dotx(dlt * dlt, ones, 2) * (1.0 / hd)
        sl = pl.ds(rb * ROW_BLK, ROW_BLK)
        y = dlt * lax.rsqrt(var + RWKV_LN_EPS) * lnw_ref[...] + lnb_ref[...]
        o_ref[rb] = (y + bonus_s[sl, :]) * g_s[sl, :]


def _rwkv_scratch(nrb):
    w = GROUP_W
    return ([pltpu.VMEM((nrb * ROW_BLK, w), F32) for _ in range(7)]
            + [pltpu.VMEM((nrb, 2, ROW_BLK, LANE), F32), pltpu.VMEM((nrb, 2, ROW_BLK, LANE), F32)])


def _rwkv_prompt(rw, prm, nb, nblk):
    w = GROUP_W
    cw = rw.shape[-1]
    const = lambda i: (0, 0)
    pspecs = [pl.BlockSpec(a.shape, const) for a in prm]
    return pl.pallas_call(
        functools.partial(_rwkv_kernel, nrb=nb, tseq=ROW_BLK, prompt=True),
        grid=(nblk,),
        in_specs=[pl.BlockSpec((nb, ROW_BLK, cw), lambda i: (0, i, 0))] + pspecs,
        out_specs=[pl.BlockSpec((nb, ROW_BLK, w), lambda i: (0, i, 0)),
                   pl.BlockSpec((nb * 2 * HEAD_DIM, LANE), const)],
        out_shape=[jax.ShapeDtypeStruct((nb, nblk * ROW_BLK, w), F32),
                   jax.ShapeDtypeStruct((nb * 2 * HEAD_DIM, LANE), F32)],
        scratch_shapes=_rwkv_scratch(nb) + [pltpu.VMEM((nb, 8, cw), F32), pltpu.VMEM((nb * 2 * HEAD_DIM, LANE), F32)],
        compiler_params=_cparams(("arbitrary",)),
        name="rwkv_prompt",
    )(rw, *prm)


def _rwkv_sample(rw, ovr, s_in, prm, tseq):
    w = GROUP_W
    nblocks, _, cw = rw.shape
    nseq = ROW_BLK // tseq
    const = lambda i: (0, 0)
    pspecs = [pl.BlockSpec(a.shape, const) for a in prm]
    blk3 = lambda width: pl.BlockSpec((1, ROW_BLK, width), lambda i: (i, 0, 0))
    st_spec = pl.BlockSpec((nseq * 2 * HEAD_DIM, LANE), lambda i: (i, 0))
    return pl.pallas_call(
        functools.partial(_rwkv_kernel, nrb=1, tseq=tseq, prompt=False),
        grid=(nblocks,),
        in_specs=[blk3(cw), blk3(cw), st_spec] + pspecs,
        out_specs=[blk3(w), st_spec],
        out_shape=[jax.ShapeDtypeStruct((nblocks, ROW_BLK, w), F32), jax.ShapeDtypeStruct(s_in.shape, F32)],
        scratch_shapes=_rwkv_scratch(1),
        compiler_params=_cparams(("parallel",)),
        name="rwkv_sample",
    )(rw, ovr, s_in, *prm)


ROUTE_W = LANE


def _out_router_kernel(x_ref, oab_ref, oc_ref, od_ref, wo_ref, g2_ref, wr_hi_ref, wr_lo_ref, br_ref,
                       x1_ref, h_ref, gate_ref, *, n_experts, n_groups, null_rows, tm):
    w = GROUP_W
    x1 = (x_ref[...]
          + _dot(oab_ref[...], wo_ref[0:2 * w, :])
          + _dot(oc_ref[...], wo_ref[2 * w:3 * w, :])
          + _dot(od_ref[...], wo_ref[3 * w:4 * w, :]))
    if null_rows:
        rowg = pl.program_id(0) * tm + _iota((tm, 1), 0)
        null = rowg < 0
        for start in null_rows:
            null = null | ((rowg >= start) & (rowg < start + PAD_FRONT))
        x1 = jnp.where(null, 0.0, x1)
    x1_ref[...] = x1
    ms = jnp.mean(x1 * x1, axis=-1, keepdims=True)
    h = x1 * lax.rsqrt(ms + NORM_EPS) * g2_ref[...]
    h_hi = h.astype(BF16)
    h_lo = (h - h_hi.astype(F32)).astype(BF16)
    h_ref[...] = h_hi
    lg = (jnp.dot(h_hi, wr_hi_ref[...], preferred_element_type=F32)
          + jnp.dot(h_lo, wr_hi_ref[...], preferred_element_type=F32)
          + jnp.dot(h_hi, wr_lo_ref[...], preferred_element_type=F32)) + br_ref[...]
    lane = _iota(lg.shape, 1)
    big = jnp.int32(1 << 20)
    neg = jnp.float32(-jnp.inf)
    is_g = (lane >= n_experts) & (lane < n_experts + n_groups)
    gl = jnp.where(is_g, lg, neg)
    gmax = jnp.max(gl, axis=-1, keepdims=True)
    gidx = jnp.min(jnp.where(gl == gmax, lane, big), axis=-1, keepdims=True) - n_experts
    g_w = 1.0 / jnp.sum(jnp.where(is_g, jnp.exp(lg - gmax), 0.0), axis=-1, keepdims=True)
    per = n_experts // n_groups
    in_group = (lane >= gidx * per) & (lane < gidx * per + per)
    el = jnp.where(in_group, lg, neg)
    v1 = jnp.max(el, axis=-1, keepdims=True)
    i1 = jnp.min(jnp.where(el == v1, lane, big), axis=-1, keepdims=True)
    el2 = jnp.where(lane == i1, neg, el)
    v2 = jnp.max(el2, axis=-1, keepdims=True)
    i2 = jnp.min(jnp.where(el2 == v2, lane, big), axis=-1, keepdims=True)
    e21 = jnp.exp(v2 - v1)
    p1 = 1.0 / (1.0 + e21)
    p2 = e21 / (1.0 + e21)
    gate_ref[...] = jnp.where(lane == i1, p1 * g_w, 0.0) + jnp.where(lane == i2, p2 * g_w, 0.0)


def _out_router(x, oab, oc, od, wo, g2, wr_hi, wr_lo, br, n_experts, n_groups, null_rows):
    n, d = x.shape
    w = GROUP_W
    tm = 256 if n % 256 == 0 else ROW_BLK
    row = lambda i: (i, 0)
    const = lambda i: (0, 0)
    return pl.pallas_call(
        functools.partial(_out_router_kernel, n_experts=n_experts, n_groups=n_groups, null_rows=null_rows, tm=tm),
        grid=(n // tm,),
        in_specs=[pl.BlockSpec((tm, d), row), pl.BlockSpec((tm, 2 * w), row), pl.BlockSpec((tm, w), row),
                  pl.BlockSpec((tm, w), row), pl.BlockSpec(wo.shape, const), pl.BlockSpec((1, d), const),
                  pl.BlockSpec(wr_hi.shape, const), pl.BlockSpec(wr_lo.shape, const), pl.BlockSpec((1, ROUTE_W), const)],
        out_specs=[pl.BlockSpec((tm, d), row), pl.BlockSpec((tm, d), row), pl.BlockSpec((tm, ROUTE_W), row)],
        out_shape=[jax.ShapeDtypeStruct((n, d), F32), jax.ShapeDtypeStruct((n, d), BF16),
                   jax.ShapeDtypeStruct((n, ROUTE_W), F32)],
        compiler_params=_cparams(("parallel",)),
        name="out_router",
    )(x, oab, oc, od, wo, g2, wr_hi, wr_lo, br)


def _moe_kernel(x1_ref, h_ref, gate_ref, w1_ref, w3_ref, w2_ref, y_ref):
    e = pl.program_id(1)

    @pl.when(e == 0)
    def _():
        y_ref[...] = x1_ref[...]

    h = h_ref[...]
    gates = gate_ref[...]
    ge = jnp.sum(jnp.where(_iota(gates.shape, 1) == e, gates, 0.0), axis=-1, keepdims=True)
    up = jnp.dot(h, w1_ref[...].astype(BF16), preferred_element_type=F32)
    lin = jnp.dot(h, w3_ref[...].astype(BF16), preferred_element_type=F32)
    hid = (up * _sigmoid(up)) * lin * ge
    y_ref[...] += jnp.dot(hid.astype(BF16), w2_ref[...].astype(BF16), preferred_element_type=F32)


def _moe_tile(n):
    for t in (1280, 1024, 640, 512, 256, 128):
        if n % t == 0:
            return t
    raise ValueError(f"row count {n} is not a multiple of {ROW_BLK}")


def _moe(x1, h, gates, w1, w3, w2, layer):
    n, d = x1.shape
    n_experts, _, f = w1.shape[1:]
    tm = _moe_tile(n)
    row = lambda i, e: (i, 0)
    return pl.pallas_call(
        _moe_kernel,
        grid=(n // tm, n_experts),
        in_specs=[pl.BlockSpec((tm, d), row), pl.BlockSpec((tm, d), row), pl.BlockSpec((tm, ROUTE_W), row),
                  pl.BlockSpec((None, None, d, f), lambda i, e: (layer, e, 0, 0)),
                  pl.BlockSpec((None, None, d, f), lambda i, e: (layer, e, 0, 0)),
                  pl.BlockSpec((None, None, f, d), lambda i, e: (layer, e, 0, 0))],
        out_specs=pl.BlockSpec((tm, d), row),
        out_shape=jax.ShapeDtypeStruct((n, d), F32),
        compiler_params=_cparams(("parallel", "arbitrary")),
        name="moe",
    )(x1, h, gates, w1, w3, w2)


def _rope_tables(pos):
    half = HEAD_DIM // 2
    inv = ROPE_BASE ** (-jnp.arange(half, dtype=F32) / half)
    ang = pos.astype(F32)[:, None] * jnp.tile(inv, LANE // half)[None, :]
    return jnp.cos(ang), jnp.sin(ang)


def _retention_log_gamma():
    lg = jnp.log1p(-jnp.exp2(-5.0 - jnp.arange(N_HG, dtype=F32)))
    return jnp.broadcast_to(jnp.repeat(lg, HEAD_DIM)[None, :], (ROW_BLK, GROUP_W))


def _row(v):
    return v.reshape(1, -1).astype(F32)


def _rwkv_params(p):
    w2 = p['rwkv_w2']
    a2 = p['rwkv_a2']
    w2p = jnp.concatenate([w2, jnp.zeros_like(a2)], axis=0)
    a2p = jnp.concatenate([jnp.zeros_like(w2), a2], axis=0)
    return (_row(p['rwkv_mu']), _row(p['rwkv_w0']), w2p, _row(p['rwkv_a0']), a2p, p['rwkv_g2'],
            _row(p['rwkv_kk']), _row(p['rwkv_ka']), _row(p['rwkv_rk']), _row(p['rwkv_ln_w']), _row(p['rwkv_ln_b']))


def _rwkv_state_to_tiles(s):
    n = s.shape[0]
    return s.reshape(n, 2, 2, HEAD_DIM, HEAD_DIM).transpose(0, 1, 3, 2, 4).reshape(n * 2 * HEAD_DIM, LANE)


def _rwkv_tiles_to_state(t, n):
    return t.reshape(n, 2, HEAD_DIM, 2, HEAD_DIM).transpose(0, 1, 3, 2, 4).reshape(n, N_HG, HEAD_DIM, HEAD_DIM)


def _diag_heads(s):
    return jnp.stack([s[:, h * HEAD_DIM:(h + 1) * HEAD_DIM, h * HEAD_DIM:(h + 1) * HEAD_DIM] for h in range(N_HG)], axis=1)


def kernel(x_prompt, x_sample, cache_sb_k, cache_sb_v, state_ret, state_hgrn, state_rwkv, state_rwkv_shift,
           page_table, meta_tokens, norm1, norm2, w_in, w_out, ret_norm, hgrn_lb_logits, hgrn_norm,
           sb_q_norm, sb_k_norm, sb_bias, rwkv_mu, rwkv_w0, rwkv_w2, rwkv_a0, rwkv_a2, rwkv_g2, rwkv_kk, rwkv_ka,
           rwkv_rk, rwkv_ln_w, rwkv_ln_b, moe_w_group, moe_b_group, moe_w_expert, moe_b_expert,
           moe_w1, moe_w3, moe_w2):
    bp, seq, dm = x_prompt.shape
    db, ds, _ = x_sample.shape
    depth = w_in.shape[0]
    w = GROUP_W
    assert seq % ROW_BLK == 0 and (db * ds) % ROW_BLK == 0 and ROW_BLK % ds == 0 and ds & (ds - 1) == 0
    assert w_in.shape[2] == LIN_COLS + 4 * w and cache_sb_k.shape[3] * cache_sb_k.shape[4] == w
    tp = seq + ROW_BLK
    nblk = tp // ROW_BLK
    tlen = seq + N_META
    past = page_table.shape[1] * cache_sb_k.shape[2]
    n_groups, e_per = moe_w_expert.shape[2:]
    n_experts = n_groups * e_per
    assert n_experts + n_groups <= ROUTE_W

    xp = jnp.concatenate([jnp.zeros((bp, PAD_FRONT, dm), F32),
                          jnp.broadcast_to(meta_tokens[None], (bp, N_META, dm)).astype(F32), x_prompt], axis=1)
    xp = xp.reshape(bp * tp, dm)
    xs = x_sample.reshape(db * ds, dm)
    null_rows = tuple(b * tp for b in range(bp))

    cos_p, sin_p = _rope_tables(jnp.maximum(jnp.arange(tp) - PAD_FRONT, 0))
    cos_s, sin_s = _rope_tables(jnp.tile(past + jnp.arange(ds), ROW_BLK // ds))
    lgam = _retention_log_gamma()
    cache_k = cache_sb_k.reshape(cache_sb_k.shape[:3] + (w,))
    cache_v = cache_sb_v.reshape(cache_sb_v.shape[:3] + (w,))
    tile_heads = lambda v: jnp.tile(v, N_HG)[None, :].astype(F32)

    prompt_rows, sample_rows = [], []
    for l in range(depth):
        wa = w_in[l][:, :LIN_COLS].astype(BF16)
        wb = w_in[l][:, LIN_COLS:].astype(BF16)
        g1 = _row(norm1[l])
        lin_args = (lgam, _row(ret_norm[l]), hgrn_lb_logits.astype(F32), _row(hgrn_norm[l]),
                    tile_heads(sb_q_norm[l]), tile_heads(sb_k_norm[l]))
        rwkv_prm = _rwkv_params(dict(rwkv_mu=rwkv_mu[l], rwkv_w0=rwkv_w0[l], rwkv_w2=rwkv_w2[l], rwkv_a0=rwkv_a0[l],
                                     rwkv_a2=rwkv_a2[l], rwkv_g2=rwkv_g2[l], rwkv_kk=rwkv_kk[l], rwkv_ka=rwkv_ka[l],
                                     rwkv_rk=rwkv_rk[l], rwkv_ln_w=rwkv_ln_w[l], rwkv_ln_b=rwkv_ln_b[l]))
        wo = w_out[l].astype(BF16)
        wr = jnp.zeros((dm, ROUTE_W), F32)
        wr = wr.at[:, :n_experts].set(moe_w_expert[l].reshape(dm, n_experts))
        wr = wr.at[:, n_experts:n_experts + n_groups].set(moe_w_group[l])
        wr_hi = wr.astype(BF16)
        wr_lo = (wr - wr_hi.astype(F32)).astype(BF16)
        br = jnp.zeros((1, ROUTE_W), F32)
        br = br.at[0, :n_experts].set(moe_b_expert[l].reshape(n_experts))
        br = br.at[0, n_experts:n_experts + n_groups].set(moe_b_group[l])
        router = (wo, _row(norm2[l]), wr_hi, wr_lo, br)

        pa, pb = _in_proj(xp, g1, wa, wb)
        oab, qs, knf, knb, vb, s_ret, s_hg = _lin_mix_prompt(pa, cos_p, sin_p, *lin_args, l, bp, nblk)
        oc = _sb_prompt(sb_bias[l].astype(F32), qs, knb, vb, bp, nblk)
        od, s_rw = _rwkv_prompt(pb.reshape(bp, tp, -1), rwkv_prm, bp, nblk)
        x1, h2, gates = _out_router(xp, oab, oc, od.reshape(bp * tp, w), *router, n_experts, n_groups, null_rows)
        xp = _moe(x1, h2, gates, moe_w1, moe_w3, moe_w2, l)
        real = lambda a: a.reshape(bp, tp, -1)[:, PAD_FRONT:]
        prompt_rows.append((real(knf).reshape(bp, tlen, N_HG, HEAD_DIM),
                            real(pa[:, LIN_COLS - w:]).reshape(bp, tlen, N_HG, HEAD_DIM),
                            _diag_heads(s_ret), _diag_heads(s_hg), _rwkv_tiles_to_state(s_rw, bp),
                            pb.reshape(bp, tp, -1)[:, -1]))

        sa, sb = _in_proj(xs, g1, wa, wb)
        stack = lambda s: s.reshape(db, w, HEAD_DIM)
        oab, qs, knf, vf, s_ret, s_hg = _lin_mix_sample(sa, cos_s, sin_s, *lin_args, stack(state_ret[l]),
                                                        stack(state_hgrn[l]), l, ds)
        oc = _sb_sample(page_table, sb_bias[l].astype(F32), qs, knf, vf, cache_k, cache_v, l, ds)
        cw = sb.shape[-1]
        ovr = jnp.concatenate([state_rwkv_shift[l][:, None, :], jnp.zeros((db, ds - 1, cw), F32)], axis=1)
        od, s_rw = _rwkv_sample(sb.reshape(-1, ROW_BLK, cw), ovr.reshape(-1, ROW_BLK, cw),
                                _rwkv_state_to_tiles(state_rwkv[l]), rwkv_prm, ds)
        x1, h2, gates = _out_router(xs, oab, oc, od.reshape(db * ds, w), *router, n_experts, n_groups, ())
        xs = _moe(x1, h2, gates, moe_w1, moe_w3, moe_w2, l)
        sample_rows.append((knf.reshape(db, ds, N_HG, HEAD_DIM), vf.reshape(db, ds, N_HG, HEAD_DIM),
                            s_ret.reshape(db, N_HG, HEAD_DIM, HEAD_DIM), s_hg.reshape(db, N_HG, HEAD_DIM, HEAD_DIM),
                            _rwkv_tiles_to_state(s_rw, db), sb.reshape(db, ds, cw)[:, -1]))

    y_prompt = xp.reshape(bp, tp, dm)[:, ROW_BLK:]
    y_sample = xs.reshape(db, ds, dm)
    stacked_p = [jnp.stack(r) for r in zip(*prompt_rows)]
    stacked_s = [jnp.stack(r) for r in zip(*sample_rows)]
    return (y_prompt, y_sample, *stacked_p, *stacked_s)
```

```python
import functools
import math

import numpy as np
import jax
import jax.numpy as jnp
from jax import lax
from jax.experimental import pallas as pl
from jax.experimental.pallas import tpu as pltpu

F32 = jnp.float32
BF16 = jnp.bfloat16

HEAD_DIM = 64
N_HG = 4
GROUP_W = N_HG * HEAD_DIM
N_META = 16
LANE = 128
ROW_BLK = 128
SB_QBLK = 256
PAD_FRONT = SB_QBLK - N_META
LIN_COLS = 11 * GROUP_W
NORM_EPS = 1e-6
RWKV_LN_EPS = 64e-5
ROPE_BASE = 10000.0
LOG2E = math.log2(math.e)
SB_SCALE = HEAD_DIM ** -0.5 * LOG2E
E_PER_GROUP = 8
VMEM_LIMIT = 56 * 1024 * 1024


def _iota(shape, dim):
    return lax.broadcasted_iota(jnp.int32, shape, dim)


def _mask01(cond):
    return jnp.where(cond, 1.0, 0.0).astype(BF16)


def _dot(a, b):
    return jnp.dot(a.astype(BF16), b.astype(BF16), preferred_element_type=F32)


def _dot_nt(a, b):
    return lax.dot_general(a.astype(BF16), b.astype(BF16), (((1,), (1,)), ((), ())),
                           preferred_element_type=F32)


def _dot_tn(a, b):
    return lax.dot_general(a.astype(BF16), b.astype(BF16), (((0,), (0,)), ((), ())),
                           preferred_element_type=F32)


def _split(x, n):
    parts = []
    r = x
    for _ in range(n):
        h = r.astype(BF16)
        parts.append(h)
        r = r - h.astype(F32)
    return parts


def _dotx(x, m, n=2):
    out = None
    for p in _split(x, n):
        t = jnp.dot(p, m, preferred_element_type=F32)
        out = t if out is None else out + t
    return out


def _xdot(m, x, n=3):
    out = None
    for p in _split(x, n):
        t = jnp.dot(m, p, preferred_element_type=F32)
        out = t if out is None else out + t
    return out


def _head_ones(w):
    return _mask01((_iota((w, w), 0) >> 6) == (_iota((w, w), 1) >> 6))


def _sigmoid(x):
    return 1.0 / (1.0 + jnp.exp(-x))


def _softplus(x):
    return jnp.maximum(x, 0.0) + jnp.log(1.0 + jnp.exp(-jnp.abs(x)))


def _softplus2(x):
    return jnp.maximum(x, 0.0) + jnp.log2(1.0 + jnp.exp2(-jnp.abs(x)))


def _head_rms(x, gain, ones):
    ms = _dotx(x * x, ones, 2) * (1.0 / HEAD_DIM)
    return x * lax.rsqrt(ms + NORM_EPS) * gain


def _row_tile(n):
    for t in (512, 256, 128):
        if n % t == 0:
            return t
    raise ValueError(f"row count {n} is not a multiple of {ROW_BLK}")


def _cparams(sem, flags=None):
    return pltpu.CompilerParams(dimension_semantics=sem, vmem_limit_bytes=VMEM_LIMIT, flags=flags)


def _in_proj_kernel(x_ref, g_ref, wa_ref, wb_ref, oa_ref, ob_ref):
    x = x_ref[...]
    ms = jnp.mean(x * x, axis=-1, keepdims=True)
    h = (x * lax.rsqrt(ms + NORM_EPS) * g_ref[...]).astype(BF16)
    oa_ref[...] = jnp.dot(h, wa_ref[...], preferred_element_type=F32)
    ob_ref[...] = jnp.dot(h, wb_ref[...], preferred_element_type=F32)


def _in_proj(x, g, wa, wb):
    n, d = x.shape
    tm = 256 if n % 256 == 0 else ROW_BLK
    return pl.pallas_call(
        _in_proj_kernel,
        grid=(n // tm,),
        in_specs=[pl.BlockSpec((tm, d), lambda i: (i, 0)),
                  pl.BlockSpec((1, d), lambda i: (0, 0)),
                  pl.BlockSpec(wa.shape, lambda i: (0, 0)),
                  pl.BlockSpec(wb.shape, lambda i: (0, 0))],
        out_specs=[pl.BlockSpec((tm, wa.shape[1]), lambda i: (i, 0)),
                   pl.BlockSpec((tm, wb.shape[1]), lambda i: (i, 0))],
        out_shape=[jax.ShapeDtypeStruct((n, wa.shape[1]), F32),
                   jax.ShapeDtypeStruct((n, wb.shape[1]), F32)],
        compiler_params=_cparams(("parallel",)),
        name="in_proj",
    )(x, g, wa, wb)


def _gla_block(q, k, v, lw, states, sub, chain):
    rows, w = q.shape
    nsub = rows // sub
    shift = int(math.log2(sub))
    r_i = _iota((rows, rows), 0)
    c_i = _iota((rows, rows), 1)
    same = (r_i >> shift) == (c_i >> shift)
    tri = _mask01(same & (c_i <= r_i))
    blk = _mask01(same)
    sub_of_row = _mask01((_iota((rows, LANE), 0) >> shift) == _iota((rows, LANE), 1))
    bc = bl = dsum = None
    for p in _split(lw, 3):
        t1 = jnp.dot(tri, p, preferred_element_type=F32)
        t2 = jnp.dot(blk, p, preferred_element_type=F32)
        t3 = lax.dot_general(p, sub_of_row, (((0,), (0,)), ((), ())), preferred_element_type=F32)
        bc = t1 if bc is None else bc + t1
        bl = t2 if bl is None else bl + t2
        dsum = t3 if dsum is None else dsum + t3
    dcol = jnp.exp(dsum)
    qt = q * jnp.exp(bc)
    kh = k * jnp.exp(bl - bc)
    ones = _head_ones(w)

    local = _iota((rows, w), 0) & (sub - 1)
    o_intra = jnp.zeros((rows, w), F32)
    for dist in range(sub):
        back = (lambda x: x) if dist == 0 else (lambda x: pltpu.roll(x, dist, 0))
        e = jnp.exp(jnp.where(local >= dist, bc - back(bc), -1e30))
        p = q * back(k) * e
        o_intra = o_intra + _dot(p, ones) * back(v)

    bdmask = (_iota((w, w), 0) >> 6) == (_iota((w, w), 1) >> 6)
    outs = []
    new_states = []
    s = states if chain else None
    for i in range(nsub):
        sl = slice(i * sub, (i + 1) * sub)
        if not chain:
            s = states[i]
        outs.append(_dot(qt[sl], s))
        kv = _dot_tn(kh[sl], v[sl])
        s = s * dcol[:, i:i + 1] + jnp.where(bdmask, kv, 0.0)
        if not chain:
            new_states.append(s)
    o = o_intra + jnp.concatenate(outs, axis=0)
    return o, (s if chain else new_states)


def _lin_mix_math(p, cos, sin, lgam, retw, lb_logits, hgw, qnw, knw, layer, st_ret, st_hg, sub, chain):
    w = GROUP_W
    qa, ka, va, ga, qb, fb, ib, gb, qc, kc, vc = [p[:, i * w:(i + 1) * w] for i in range(11)]
    rows = p.shape[0]
    ones = _head_ones(w)
    lane = _iota((rows, LANE), 1)
    first_half = (lane & (HEAD_DIM - 1)) < (HEAD_DIM // 2)

    def rope(x):
        halves = []
        for hp in range(w // LANE):
            xh = x[:, hp * LANE:(hp + 1) * LANE]
            rot = jnp.where(first_half, -pltpu.roll(xh, LANE - HEAD_DIM // 2, 1), pltpu.roll(xh, HEAD_DIM // 2, 1))
            halves.append(xh * cos + rot * sin)
        return jnp.concatenate(halves, axis=1)

    q_ret = rope(qa)
    k_ret = rope(ka) * (HEAD_DIM ** -0.5)
    o_ret, st_ret = _gla_block(q_ret, k_ret, va, lgam, st_ret, sub, chain)
    o_a = _head_rms(o_ret, retw, ones) * (ga * _sigmoid(ga))

    lg = [lb_logits[d:d + 1, :] for d in range(lb_logits.shape[0])]
    mx = functools.reduce(jnp.maximum, lg)
    ex = [jnp.exp(row - mx) for row in lg]
    lb = sum(ex[1:layer + 1], jnp.zeros_like(mx)) / sum(ex[1:], ex[0])
    log_sig = jnp.minimum(fb, 0.0) - jnp.log1p(jnp.exp(-jnp.abs(fb)))
    t_a = jnp.broadcast_to(jnp.log(lb), fb.shape)
    t_b = jnp.log1p(-lb) + log_sig
    logf = jnp.maximum(t_a, t_b) + jnp.log1p(jnp.exp(-jnp.abs(t_a - t_b)))
    k_hg = (1.0 - lb) * _sigmoid(-fb)
    o_hg, st_hg = _gla_block(qb, k_hg, ib, logf, st_hg, sub, chain)
    o_b = _head_rms(o_hg, hgw, ones) * (gb * _sigmoid(gb))

    qn = _head_rms(qc, qnw, ones)
    kn = _head_rms(kc, knw, ones)
    return o_a, o_b, qn, kn, vc, st_ret, st_hg


def _lin_mix_prompt_kernel(p_ref, cos_ref, sin_ref, lgam_ref, retw_ref, lbl_ref, hgw_ref, qnw_ref, knw_ref,
                           oab_ref, qs_ref, knf_ref, knb_ref, vb_ref, sret_ref, shg_ref,
                           st_ret, st_hg, *, layer, sub):
    i = pl.program_id(1)

    @pl.when(i == 0)
    def _():
        st_ret[...] = jnp.zeros_like(st_ret)
        st_hg[...] = jnp.zeros_like(st_hg)

    o_a, o_b, qn, kn, vc, s1, s2 = _lin_mix_math(
        p_ref[...], cos_ref[...], sin_ref[...], lgam_ref[...], retw_ref[...], lbl_ref, hgw_ref[...],
        qnw_ref[...], knw_ref[...], layer, st_ret[...], st_hg[...], sub, True)
    st_ret[...] = s1
    st_hg[...] = s2
    oab_ref[...] = jnp.concatenate([o_a, o_b], axis=1)
    qs_ref[...] = (qn * SB_SCALE).astype(BF16)
    knf_ref[...] = kn
    knb_ref[...] = kn.astype(BF16)
    vb_ref[...] = vc.astype(BF16)

    @pl.when(i == pl.num_programs(1) - 1)
    def _():
        sret_ref[...] = s1
        shg_ref[...] = s2


def _lin_mix_prompt(proj, cos, sin, lgam, retw, lbl, hgw, qnw, knw, layer, nb, nblk):
    n = proj.shape[0]
    w = GROUP_W
    row = lambda b, i: (b * nblk + i, 0)
    const = lambda b, i: (0, 0)
    outs = pl.pallas_call(
        functools.partial(_lin_mix_prompt_kernel, layer=layer, sub=16),
        grid=(nb, nblk),
        in_specs=[pl.BlockSpec((ROW_BLK, LIN_COLS), row),
                  pl.BlockSpec((ROW_BLK, LANE), lambda b, i: (i, 0)),
                  pl.BlockSpec((ROW_BLK, LANE), lambda b, i: (i, 0)),
                  pl.BlockSpec((ROW_BLK, w), const), pl.BlockSpec((1, w), const),
                  pl.BlockSpec(lbl.shape, const), pl.BlockSpec((1, w), const),
                  pl.BlockSpec((1, w), const), pl.BlockSpec((1, w), const)],
        out_specs=[pl.BlockSpec((ROW_BLK, 2 * w), row),
                   pl.BlockSpec((ROW_BLK, w), row), pl.BlockSpec((ROW_BLK, w), row),
                   pl.BlockSpec((ROW_BLK, w), row), pl.BlockSpec((ROW_BLK, w), row),
                   pl.BlockSpec((None, w, w), lambda b, i: (b, 0, 0)),
                   pl.BlockSpec((None, w, w), lambda b, i: (b, 0, 0))],
        out_shape=[jax.ShapeDtypeStruct((n, 2 * w), F32),
                   jax.ShapeDtypeStruct((n, w), BF16), jax.ShapeDtypeStruct((n, w), F32),
                   jax.ShapeDtypeStruct((n, w), BF16), jax.ShapeDtypeStruct((n, w), BF16),
                   jax.ShapeDtypeStruct((nb, w, w), F32), jax.ShapeDtypeStruct((nb, w, w), F32)],
        scratch_shapes=[pltpu.VMEM((w, w), F32), pltpu.VMEM((w, w), F32)],
        compiler_params=_cparams(("parallel", "arbitrary")),
        name="lin_mix_prompt",
    )(proj, cos, sin, lgam, retw, lbl, hgw, qnw, knw)
    return outs


def _lin_mix_sample_kernel(p_ref, cos_ref, sin_ref, lgam_ref, retw_ref, lbl_ref, hgw_ref, qnw_ref, knw_ref,
                           sret_in, shg_in,
                           oab_ref, qs_ref, knf_ref, vf_ref, sret_out, shg_out, *, layer, sub):
    w = GROUP_W
    nseq = ROW_BLK // sub
    bdmask = (_iota((w, w), 0) >> 6) == (_iota((w, w), 1) >> 6)
    rep = _mask01(_iota((HEAD_DIM, w), 0) == (_iota((HEAD_DIM, w), 1) & (HEAD_DIM - 1)))
    rep_t = _mask01((_iota((w, HEAD_DIM), 0) & (HEAD_DIM - 1)) == _iota((w, HEAD_DIM), 1))

    def expand(ref):
        return [jnp.where(bdmask, _dotx(ref[j], rep, 3), 0.0) for j in range(nseq)]

    def extract(ref, states):
        for j in range(nseq):
            ref[j] = _dotx(states[j], rep_t, 3)

    o_a, o_b, qn, kn, vc, s1, s2 = _lin_mix_math(
        p_ref[...], cos_ref[...], sin_ref[...], lgam_ref[...], retw_ref[...], lbl_ref, hgw_ref[...],
        qnw_ref[...], knw_ref[...], layer, expand(sret_in), expand(shg_in), sub, False)
    extract(sret_out, s1)
    extract(shg_out, s2)
    oab_ref[...] = jnp.concatenate([o_a, o_b], axis=1)
    qs_ref[...] = qn * SB_SCALE
    knf_ref[...] = kn
    vf_ref[...] = vc


def _lin_mix_sample(proj, cos, sin, lgam, retw, lbl, hgw, qnw, knw, s_ret, s_hg, layer, dec_seq):
    n = proj.shape[0]
    w = GROUP_W
    nseq = ROW_BLK // dec_seq
    row = lambda i: (i, 0)
    const = lambda i: (0, 0)
    st_spec = pl.BlockSpec((nseq, w, HEAD_DIM), lambda i: (i, 0, 0))
    return pl.pallas_call(
        functools.partial(_lin_mix_sample_kernel, layer=layer, sub=dec_seq),
        grid=(n // ROW_BLK,),
        in_specs=[pl.BlockSpec((ROW_BLK, LIN_COLS), row),
                  pl.BlockSpec((ROW_BLK, LANE), const), pl.BlockSpec((ROW_BLK, LANE), const),
                  pl.BlockSpec((ROW_BLK, w), const), pl.BlockSpec((1, w), const),
                  pl.BlockSpec(lbl.shape, const), pl.BlockSpec((1, w), const),
                  pl.BlockSpec((1, w), const), pl.BlockSpec((1, w), const),
                  st_spec, st_spec],
        out_specs=[pl.BlockSpec((ROW_BLK, 2 * w), row),
                   pl.BlockSpec((ROW_BLK, w), row), pl.BlockSpec((ROW_BLK, w), row),
                   pl.BlockSpec((ROW_BLK, w), row), st_spec, st_spec],
        out_shape=[jax.ShapeDtypeStruct((n, 2 * w), F32),
                   jax.ShapeDtypeStruct((n, w), F32), jax.ShapeDtypeStruct((n, w), F32),
                   jax.ShapeDtypeStruct((n, w), F32),
                   jax.ShapeDtypeStruct(s_ret.shape, F32), jax.ShapeDtypeStruct(s_hg.shape, F32)],
        compiler_params=_cparams(("parallel",)),
        name="lin_mix_sample",
    )(proj, cos, sin, lgam, retw, lbl, hgw, qnw, knw, s_ret, s_hg)


def _sb_consts(rows, tmask):
    j_i = _iota((LANE, 2 * LANE), 0)
    s_i = _iota((LANE, 2 * LANE), 1)
    ucat = _mask01((s_i >= LANE) | (j_i >= s_i))
    causal = _iota((rows, LANE), 1) < (_iota((rows, LANE), 0) & tmask)
    return ucat, causal


def _sb_block(z, carry, ucat, causal):
    sp = _softplus2(z)
    if causal is not None:
        sp = jnp.where(causal, sp, 0.0)
    t = _dotx(sp, ucat, 2)
    a = jnp.exp2(z - (carry + t[:, :LANE]))
    if causal is not None:
        a = jnp.where(causal, a, 0.0)
    return a, carry + t[:, LANE:]


def _sb_prompt_kernel(bias_ref, q_ref, k_ref, v_ref, o_ref):
    i = pl.program_id(1)
    qb = SB_QBLK
    kpq = qb // ROW_BLK
    q = q_ref[...]
    left = _iota((ROW_BLK, LANE), 1) < HEAD_DIM
    zero = jnp.zeros((ROW_BLK, LANE), BF16)
    width = N_HG * ROW_BLK
    lane = _iota((qb, width), 1)
    bias_row = jnp.zeros((qb, width), F32)
    for h in range(N_HG):
        bias_row = jnp.where((lane >> 7) == h, bias_ref[h] * LOG2E, bias_row)
    sk = lane & (ROW_BLK - 1)
    tq = _iota((qb, width), 0)
    r2 = _iota((2 * LANE, 2 * LANE), 0)
    c2 = _iota((2 * LANE, 2 * LANE), 1)
    ubd = _mask01(((r2 >> 7) == (c2 >> 7)) & ((r2 & (LANE - 1)) >= (c2 & (LANE - 1))))

    def split_heads(x):
        return jnp.concatenate([jnp.where(left, x, zero), jnp.where(left, zero, x)], axis=0)

    def step(kb, carry, acc0, acc1, mask):
        start = pl.multiple_of(kb * ROW_BLK, ROW_BLK)
        kblk = k_ref[pl.ds(start, ROW_BLK), :]
        vblk = v_ref[pl.ds(start, ROW_BLK), :]
        z = jnp.concatenate([_dot_nt(q[:, p * LANE:(p + 1) * LANE], split_heads(kblk[:, p * LANE:(p + 1) * LANE]))
                             for p in range(2)], axis=1) + bias_row
        sp = _softplus2(z)
        if mask is not None:
            sp = jnp.where(mask, sp, 0.0)
        t = jnp.concatenate([_dotx(sp[:, p * 2 * LANE:(p + 1) * 2 * LANE], ubd, 2) for p in range(2)], axis=1)
        a = jnp.exp2(z - (carry + t))
        if mask is not None:
            a = jnp.where(mask, a, 0.0)
        total = jnp.concatenate([jnp.broadcast_to(t[:, h * ROW_BLK:h * ROW_BLK + 1], (qb, ROW_BLK))
                                 for h in range(N_HG)], axis=1)
        ab = a.astype(BF16)
        accs = []
        for p, acc in enumerate((acc0, acc1)):
            accs.append(acc + jnp.dot(ab[:, p * 2 * LANE:(p + 1) * 2 * LANE],
                                      split_heads(vblk[:, p * LANE:(p + 1) * LANE]), preferred_element_type=F32))
        return carry + total, accs[0], accs[1]

    zeros = jnp.zeros((qb, LANE), F32)
    state = (jnp.zeros((qb, width), F32), zeros, zeros)
    for d in reversed(range(kpq)):
        state = step(i * kpq + d, *state, sk + d * ROW_BLK < tq)

    def earlier_blocks(j, c):
        for d in range(kpq):
            c = step((i - j) * kpq - 1 - d, *c, None)
        return c

    state = lax.fori_loop(0, i, earlier_blocks, state)
    o_ref[...] = jnp.concatenate([state[1], state[2]], axis=1)


def _sb_prompt(bias, qs, kb, vb, nb, tp):
    n, w = qs.shape
    nq = tp // SB_QBLK
    return pl.pallas_call(
        _sb_prompt_kernel,
        grid=(nb, nq),
        in_specs=[pl.BlockSpec(memory_space=pltpu.SMEM),
                  pl.BlockSpec((SB_QBLK, w), lambda b, i: (b * nq + i, 0)),
                  pl.BlockSpec((tp, w), lambda b, i: (b, 0)),
                  pl.BlockSpec((tp, w), lambda b, i: (b, 0))],
        out_specs=pl.BlockSpec((SB_QBLK, w), lambda b, i: (b * nq + i, 0)),
        out_shape=jax.ShapeDtypeStruct((n, w), F32),
        compiler_params=_cparams(("parallel", "arbitrary")),
        name="sb_prompt",
    )(bias, qs, kb, vb)


def _sb_sample_kernel(pt_ref, bias_ref, q_ref, kn_ref, vn_ref, *rest, n_pages, dec_seq):
    del pt_ref
    k_pages = rest[:n_pages]
    v_pages = rest[n_pages:2 * n_pages]
    o_ref = rest[2 * n_pages]
    w = GROUP_W
    rows = N_HG * dec_seq
    q = q_ref[...]
    head_of_lane = _iota((dec_seq, w), 1) >> 6
    qbd = jnp.concatenate([jnp.where(head_of_lane == h, q, 0.0) for h in range(N_HG)], axis=0).astype(BF16)
    bias_col = jnp.concatenate([jnp.full((dec_seq, 1), bias_ref[h] * LOG2E, F32) for h in range(N_HG)], axis=0)
    ucat, causal = _sb_consts(rows, dec_seq - 1)
    pad = jnp.zeros((LANE - dec_seq, w), F32)
    k_new = jnp.concatenate([kn_ref[...], pad], axis=0)
    v_new = jnp.concatenate([vn_ref[...], pad], axis=0)
    a, carry = _sb_block(_dot_nt(qbd, k_new) + bias_col, jnp.zeros((rows, LANE), F32), ucat, causal)
    acc = _dot(a, v_new)
    for p in reversed(range(n_pages)):
        a, carry = _sb_block(_dot_nt(qbd, k_pages[p][...]) + bias_col, carry, ucat, None)
        acc = acc + _dot(a, v_pages[p][...])
    out = jnp.zeros((dec_seq, w), F32)
    for h in range(N_HG):
        out = out + jnp.where(head_of_lane == h, acc[h * dec_seq:(h + 1) * dec_seq], 0.0)
    o_ref[...] = out


def _sb_sample(page_table, bias, qs, kn, vn, cache_k, cache_v, layer, dec_seq):
    n, w = qs.shape
    db, n_pages = page_table.shape
    page = cache_k.shape[2]
    assert page == LANE and dec_seq % 8 == 0
    row = lambda b, pt: (b, 0)
    page_specs = [pl.BlockSpec((None, None, page, w), lambda b, pt, p=p: (layer, pt[b, p], 0, 0))
                  for p in range(n_pages)]
    grid_spec = pltpu.PrefetchScalarGridSpec(
        num_scalar_prefetch=1,
        grid=(db,),
        in_specs=[pl.BlockSpec(memory_space=pltpu.SMEM),
                  pl.BlockSpec((dec_seq, w), row), pl.BlockSpec((dec_seq, w), row), pl.BlockSpec((dec_seq, w), row)]
                 + page_specs + page_specs,
        out_specs=pl.BlockSpec((dec_seq, w), row),
    )
    return pl.pallas_call(
        functools.partial(_sb_sample_kernel, n_pages=n_pages, dec_seq=dec_seq),
        grid_spec=grid_spec,
        out_shape=jax.ShapeDtypeStruct((n, w), F32),
        compiler_params=_cparams(("arbitrary",)),
        name="sb_sample",
    )(page_table, bias, qs, kn, vn, *([cache_k] * n_pages), *([cache_v] * n_pages))


def _rwkv_prologue(rw, prev, mu, w0, w2p, a0, a2p, g2, kkp, ka, rk, ones):
    w = GROUP_W
    xm = rw + (prev - rw) * mu
    r = xm[:, 0:w]
    k = xm[:, w:2 * w]
    v = xm[:, 2 * w:3 * w]
    wa = xm[:, 3 * w:3 * w + LANE]
    gl = xm[:, 3 * w + LANE:]
    wd = w0 + _dot(jnp.tanh(wa), w2p)
    log_decay = -jnp.exp(-_softplus(-wd) - 0.5)
    a = _sigmoid(a0 + _dot(wa, a2p))
    g = _dot(_sigmoid(gl), g2)
    kk = k * kkp
    kk = kk * lax.rsqrt(jnp.maximum(_dotx(kk * kk, ones, 2), 1e-12))
    k_rw = k * (1.0 + (a - 1.0) * ka)
    bonus = _dotx(r * k_rw * rk, ones, 2) * v
    return r, log_decay, k_rw, v, kk, kk * a, g, bonus


def _rwkv_group_norm(o, lnw, lnb, bonus, g, ones):
    mu_h = _dotx(o, ones, 2) * (1.0 / HEAD_DIM)
    dlt = o - mu_h
    var = _dotx(dlt * dlt, ones, 2) * (1.0 / HEAD_DIM)
    return (dlt * lax.rsqrt(var + RWKV_LN_EPS) * lnw + lnb + bonus) * g


def _rwkv_chunk_kernel(rw_ref, mu_ref, w0_ref, w2p_ref, a0_ref, a2p_ref, g2_ref, kkp_ref, ka_ref, rk_ref, lnw_ref, lnb_ref,
                       o_ref, s_out_ref, carry_s, state_s, *, sub):
    i = pl.program_id(1)
    w = GROUP_W
    rows = ROW_BLK
    nsub = rows // sub
    shift = int(math.log2(sub))
    ones = _head_ones(w)

    @pl.when(i == 0)
    def _():
        carry_s[...] = jnp.zeros_like(carry_s)
        state_s[...] = jnp.zeros_like(state_s)

    rw = rw_ref[...]
    prev = jnp.where(_iota((rows, 1), 0) == 0, carry_s[0:1, :], pltpu.roll(rw, 1, 0))
    carry_s[0:1, :] = rw[rows - 1:rows, :]
    r, lw, k, v, kk, b, g, bonus = _rwkv_prologue(
        rw, prev, mu_ref[...], w0_ref[...], w2p_ref[...], a0_ref[...], a2p_ref[...], g2_ref[...],
        kkp_ref[...], ka_ref[...], rk_ref[...], ones)

    r_i = _iota((rows, rows), 0)
    c_i = _iota((rows, rows), 1)
    same = (r_i >> shift) == (c_i >> shift)
    tri = _mask01(same & (c_i <= r_i))
    blk = _mask01(same)
    sub_of_row = _mask01((_iota((rows, LANE), 0) >> shift) == _iota((rows, LANE), 1))
    c = cl = dsum = None
    for part in _split(lw, 3):
        t1 = jnp.dot(tri, part, preferred_element_type=F32)
        t2 = jnp.dot(blk, part, preferred_element_type=F32)
        t3 = lax.dot_general(part, sub_of_row, (((0,), (0,)), ((), ())), preferred_element_type=F32)
        c = t1 if c is None else c + t1
        cl = t2 if cl is None else cl + t2
        dsum = t3 if dsum is None else dsum + t3
    dcol = jnp.exp(dsum)
    kkd = kk * jnp.exp(c - lw)
    rd = r * jnp.exp(c)
    tail = jnp.exp(cl - c)
    khat = k * tail
    bhat = b * tail
    kkw = kk * jnp.exp(-lw)

    local = _iota((rows, w), 0) & (sub - 1)
    y_intra = jnp.zeros((rows, w), F32)
    o_acc = jnp.zeros((rows, w), F32)
    abk = [None]
    abr = []
    for dist in range(sub):
        back = (lambda x: x) if dist == 0 else (lambda x: pltpu.roll(x, dist, 0))
        e = jnp.exp(jnp.where(local >= dist, c - back(c), -1e30))
        kb = back(k) * e
        bb = back(b) * e
        vb = back(v)
        o_acc = o_acc + _dotx(r * kb, ones, 2) * vb
        abr.append(_dotx(r * bb, ones, 2))
        if dist > 0:
            y_intra = y_intra + _dotx(kkw * kb, ones, 2) * vb
            abk.append(_dotx(kkw * bb, ones, 2))

    bdmask = (_iota((w, w), 0) >> 6) == (_iota((w, w), 1) >> 6)
    m = state_s[...]
    us = []
    o_state = []
    for j in range(nsub):
        sl = slice(j * sub, (j + 1) * sub)
        y = y_intra[sl] + _dot(kkd[sl], m)
        o_state.append(_dot(rd[sl], m))
        u_rows = []
        for t in range(sub):
            acc = y[t:t + 1, :]
            for s in range(t):
                acc = acc - abk[t - s][j * sub + t:j * sub + t + 1, :] * u_rows[s]
            u_rows.append(acc)
        u = jnp.concatenate(u_rows, axis=0)
        us.append(u)
        kv = _dot_tn(khat[sl], v[sl]) - _dot_tn(bhat[sl], u)
        m = m * dcol[:, j:j + 1] + jnp.where(bdmask, kv, 0.0)
    state_s[...] = m
    u_all = jnp.concatenate(us, axis=0)
    o = o_acc + jnp.concatenate(o_state, axis=0)
    for dist in range(sub):
        o = o - abr[dist] * (u_all if dist == 0 else pltpu.roll(u_all, dist, 0))
    o_ref[...] = _rwkv_group_norm(o, lnw_ref[...], lnb_ref[...], bonus, g, ones)

    @pl.when(i == pl.num_programs(1) - 1)
    def _():
        s_out_ref[...] = m


def _rwkv_prompt(rw, prm, nb, nblk):
    w = GROUP_W
    n, cw = rw.shape
    const = lambda b, i: (0, 0)
    row = lambda b, i: (b * nblk + i, 0)
    return pl.pallas_call(
        functools.partial(_rwkv_chunk_kernel, sub=16),
        grid=(nb, nblk),
        in_specs=[pl.BlockSpec((ROW_BLK, cw), row)] + [pl.BlockSpec(a.shape, const) for a in prm],
        out_specs=[pl.BlockSpec((ROW_BLK, w), row), pl.BlockSpec((None, w, w), lambda b, i: (b, 0, 0))],
        out_shape=[jax.ShapeDtypeStruct((n, w), F32), jax.ShapeDtypeStruct((nb, w, w), F32)],
        scratch_shapes=[pltpu.VMEM((8, cw), F32), pltpu.VMEM((w, w), F32)],
        compiler_params=_cparams(("parallel", "arbitrary")),
        name="rwkv_prompt",
    )(rw, *prm)


def _rwkv_scan(tiles, tseq, s0, op_refs, vt_ref, acc_ref):
    g = len(tiles)
    hd = HEAD_DIM
    kk_ref, w_ref, bk_ref, kr_ref, rr_ref = op_refs
    ones_h = _head_ones(LANE)
    ones_f = jnp.ones((LANE, LANE), BF16)
    j2 = _mask01((_iota((LANE, 2 * LANE), 0) >> 6) == (_iota((LANE, 2 * LANE), 1) >> 7))
    lane_t = _iota((hd, LANE), 1)
    left = lane_t < hd
    x0 = [vt_ref[rb, p, 0:hd, :] for (rb, _, p) in tiles]
    x1 = [vt_ref[rb, p, hd:2 * hd, :] for (rb, _, p) in tiles]
    tile_rows = lambda a, i: a[i * hd:(i + 1) * hd]
    step_rows = 8

    def group(t8, s):
        base = t8 * step_rows
        blks = []
        for ref in op_refs:
            per_tile = []
            for (rb, j, p) in tiles:
                start = rb * ROW_BLK + j * tseq + base
                if not isinstance(start, int):
                    start = pl.multiple_of(start, step_rows)
                per_tile.append(ref[pl.ds(start, step_rows), pl.ds(p * LANE, LANE)])
            blks.append(per_tile)
        for u in range(step_rows):
            row = lambda q, i: blks[q][i][u:u + 1, :]
            s_t = [tile_rows(s, i) for i in range(g)]
            sa = _dotx(jnp.concatenate([s_t[i] * row(0, i) for i in range(g)], axis=0), ones_h, 2)
            msk = [lane_t == (j * tseq + base + u) for (_, j, _) in tiles]
            vsel = jnp.concatenate([jnp.where(msk[i], x0[i], 0.0) for i in range(g)]
                                   + [jnp.where(msk[i], x1[i], 0.0) for i in range(g)], axis=0)
            vc = _dotx(vsel, ones_f, 2)
            new = []
            for i in range(g):
                vcol = jnp.where(left, tile_rows(vc, i), tile_rows(vc, g + i))
                new.append(s_t[i] * row(1, i) - tile_rows(sa, i) * row(2, i) + vcol * row(3, i))
            ro = _dot(jnp.concatenate([new[i] * row(4, i) for i in range(g)], axis=0), j2)
            for i, (rb, _, p) in enumerate(tiles):
                r_i = tile_rows(ro, i)
                acc_ref[rb, p, 0:hd, :] = jnp.where(msk[i], r_i[:, :LANE], acc_ref[rb, p, 0:hd, :])
                acc_ref[rb, p, hd:2 * hd, :] = jnp.where(msk[i], r_i[:, LANE:], acc_ref[rb, p, hd:2 * hd, :])
            s = jnp.concatenate(new, axis=0)
        return s

    assert tseq % step_rows == 0
    if tseq == step_rows:
        return group(0, s0)
    return lax.fori_loop(0, tseq // step_rows, group, s0)


def _rwkv_sample_kernel(rw_ref, ovr_ref, s_in_ref, mu_ref, w0_ref, w2p_ref, a0_ref, a2p_ref, g2_ref, kkp_ref, ka_ref,
                        rk_ref, lnw_ref, lnb_ref, o_ref, s_out_ref,
                        kk_s, w_s, bk_s, kr_s, rr_s, vt_s, acc_s, *, tseq):
    w = GROUP_W
    hd = HEAD_DIM
    ones = _head_ones(w)
    nseq = ROW_BLK // tseq
    rw = rw_ref[0]
    first = (_iota((ROW_BLK, 1), 0) & (tseq - 1)) == 0
    prev = jnp.where(first, ovr_ref[0], pltpu.roll(rw, 1, 0))
    r, log_decay, k_rw, v, kk, bk, g, bonus = _rwkv_prologue(
        rw, prev, mu_ref[...], w0_ref[...], w2p_ref[...], a0_ref[...], a2p_ref[...], g2_ref[...],
        kkp_ref[...], ka_ref[...], rk_ref[...], ones)
    kk_s[...] = kk
    w_s[...] = jnp.exp(log_decay)
    bk_s[...] = bk
    kr_s[...] = k_rw
    rr_s[...] = r
    for p in range(2):
        vt_s[0, p] = v[:, p * LANE:(p + 1) * LANE].T
    acc_s[...] = jnp.zeros_like(acc_s)

    ops = (kk_s, w_s, bk_s, kr_s, rr_s)
    per = 2
    for grp in range(nseq // per):
        tiles = [(0, grp * per + jj, p) for jj in range(per) for p in range(2)]
        rows = pl.ds(grp * per * 2 * hd, per * 2 * hd)
        s_out_ref[rows, :] = _rwkv_scan(tiles, tseq, s_in_ref[rows, :], ops, vt_s, acc_s)

    o = jnp.concatenate([acc_s[0, p].T for p in range(2)], axis=1)
    o_ref[0] = _rwkv_group_norm(o, lnw_ref[...], lnb_ref[...], bonus, g, ones)


def _rwkv_sample(rw, ovr, s_in, prm, tseq):
    w = GROUP_W
    nblocks, _, cw = rw.shape
    nseq = ROW_BLK // tseq
    const = lambda i: (0, 0)
    pspecs = [pl.BlockSpec(a.shape, const) for a in prm]
    blk3 = lambda width: pl.BlockSpec((1, ROW_BLK, width), lambda i: (i, 0, 0))
    st_spec = pl.BlockSpec((nseq * 2 * HEAD_DIM, LANE), lambda i: (i, 0))
    scratch = ([pltpu.VMEM((ROW_BLK, w), F32) for _ in range(5)]
               + [pltpu.VMEM((1, 2, ROW_BLK, LANE), F32), pltpu.VMEM((1, 2, ROW_BLK, LANE), F32)])
    return pl.pallas_call(
        functools.partial(_rwkv_sample_kernel, tseq=tseq),
        grid=(nblocks,),
        in_specs=[blk3(cw), blk3(cw), st_spec] + pspecs,
        out_specs=[blk3(w), st_spec],
        out_shape=[jax.ShapeDtypeStruct((nblocks, ROW_BLK, w), F32), jax.ShapeDtypeStruct(s_in.shape, F32)],
        scratch_shapes=scratch,
        compiler_params=_cparams(("parallel",)),
        name="rwkv_sample",
    )(rw, ovr, s_in, *prm)


ROUTE_W = LANE


def _out_router_kernel(x_ref, oab_ref, oc_ref, od_ref, wo_ref, g2_ref, wr_hi_ref, wr_lo_ref, br_ref,
                       x1_ref, h_ref, gate_ref, *, n_experts, n_groups, null_rows, tm):
    w = GROUP_W
    x1 = (x_ref[...]
          + _dot(oab_ref[...], wo_ref[0:2 * w, :])
          + _dot(oc_ref[...], wo_ref[2 * w:3 * w, :])
          + _dot(od_ref[...], wo_ref[3 * w:4 * w, :]))
    if null_rows:
        rowg = pl.program_id(0) * tm + _iota((tm, 1), 0)
        null = rowg < 0
        for start in null_rows:
            null = null | ((rowg >= start) & (rowg < start + PAD_FRONT))
        x1 = jnp.where(null, 0.0, x1)
    x1_ref[...] = x1
    ms = jnp.mean(x1 * x1, axis=-1, keepdims=True)
    h = x1 * lax.rsqrt(ms + NORM_EPS) * g2_ref[...]
    h_hi = h.astype(BF16)
    h_lo = (h - h_hi.astype(F32)).astype(BF16)
    h_ref[...] = h_hi
    lg = (jnp.dot(h_hi, wr_hi_ref[...], preferred_element_type=F32)
          + jnp.dot(h_lo, wr_hi_ref[...], preferred_element_type=F32)
          + jnp.dot(h_hi, wr_lo_ref[...], preferred_element_type=F32)) + br_ref[...]
    lane = _iota(lg.shape, 1)
    big = jnp.int32(1 << 20)
    neg = jnp.float32(-jnp.inf)
    is_g = (lane >= n_experts) & (lane < n_experts + n_groups)
    gl = jnp.where(is_g, lg, neg)
    gmax = jnp.max(gl, axis=-1, keepdims=True)
    gidx = jnp.min(jnp.where(gl == gmax, lane, big), axis=-1, keepdims=True) - n_experts
    g_w = 1.0 / jnp.sum(jnp.where(is_g, jnp.exp(lg - gmax), 0.0), axis=-1, keepdims=True)
    per = n_experts // n_groups
    in_group = (lane >= gidx * per) & (lane < gidx * per + per)
    el = jnp.where(in_group, lg, neg)
    v1 = jnp.max(el, axis=-1, keepdims=True)
    i1 = jnp.min(jnp.where(el == v1, lane, big), axis=-1, keepdims=True)
    el2 = jnp.where(lane == i1, neg, el)
    v2 = jnp.max(el2, axis=-1, keepdims=True)
    i2 = jnp.min(jnp.where(el2 == v2, lane, big), axis=-1, keepdims=True)
    e21 = jnp.exp(v2 - v1)
    p1 = 1.0 / (1.0 + e21)
    p2 = e21 / (1.0 + e21)
    gate_ref[...] = jnp.where(lane == i1, p1 * g_w, 0.0) + jnp.where(lane == i2, p2 * g_w, 0.0)


def _out_router(x, oab, oc, od, wo, g2, wr_hi, wr_lo, br, n_experts, n_groups, null_rows):
    n, d = x.shape
    w = GROUP_W
    tm = 256 if n % 256 == 0 else ROW_BLK
    row = lambda i: (i, 0)
    const = lambda i: (0, 0)
    return pl.pallas_call(
        functools.partial(_out_router_kernel, n_experts=n_experts, n_groups=n_groups, null_rows=null_rows, tm=tm),
        grid=(n // tm,),
        in_specs=[pl.BlockSpec((tm, d), row), pl.BlockSpec((tm, 2 * w), row), pl.BlockSpec((tm, w), row),
                  pl.BlockSpec((tm, w), row), pl.BlockSpec(wo.shape, const), pl.BlockSpec((1, d), const),
                  pl.BlockSpec(wr_hi.shape, const), pl.BlockSpec(wr_lo.shape, const), pl.BlockSpec((1, ROUTE_W), const)],
        out_specs=[pl.BlockSpec((tm, d), row), pl.BlockSpec((tm, d), row), pl.BlockSpec((tm, ROUTE_W), row)],
        out_shape=[jax.ShapeDtypeStruct((n, d), F32), jax.ShapeDtypeStruct((n, d), BF16),
                   jax.ShapeDtypeStruct((n, ROUTE_W), F32)],
        compiler_params=_cparams(("parallel",)),
        name="out_router",
    )(x, oab, oc, od, wo, g2, wr_hi, wr_lo, br)


def _moe_kernel(x1_ref, h_ref, gate_ref, w1_ref, w3_ref, w2_ref, y_ref):
    e = pl.program_id(1)

    @pl.when(e == 0)
    def _():
        y_ref[...] = x1_ref[...]

    h = h_ref[...]
    gates = gate_ref[...]
    ge = jnp.sum(jnp.where(_iota(gates.shape, 1) == e, gates, 0.0), axis=-1, keepdims=True)
    up = jnp.dot(h, w1_ref[...].astype(BF16), preferred_element_type=F32)
    lin = jnp.dot(h, w3_ref[...].astype(BF16), preferred_element_type=F32)
    hid = (up * _sigmoid(up)) * lin * ge
    y_ref[...] += jnp.dot(hid.astype(BF16), w2_ref[...].astype(BF16), preferred_element_type=F32)


def _moe_tile(n):
    for t in (1280, 1024, 640, 512, 256, 128):
        if n % t == 0:
            return t
    raise ValueError(f"row count {n} is not a multiple of {ROW_BLK}")


def _moe(x1, h, gates, w1, w3, w2, layer):
    n, d = x1.shape
    n_experts, _, f = w1.shape[1:]
    tm = _moe_tile(n)
    row = lambda i, e: (i, 0)
    return pl.pallas_call(
        _moe_kernel,
        grid=(n // tm, n_experts),
        in_specs=[pl.BlockSpec((tm, d), row), pl.BlockSpec((tm, d), row), pl.BlockSpec((tm, ROUTE_W), row),
                  pl.BlockSpec((None, None, d, f), lambda i, e: (layer, e, 0, 0)),
                  pl.BlockSpec((None, None, d, f), lambda i, e: (layer, e, 0, 0)),
                  pl.BlockSpec((None, None, f, d), lambda i, e: (layer, e, 0, 0))],
        out_specs=pl.BlockSpec((tm, d), row),
        out_shape=jax.ShapeDtypeStruct((n, d), F32),
        compiler_params=_cparams(("parallel", "arbitrary")),
        name="moe",
    )(x1, h, gates, w1, w3, w2)


def _rope_tables(pos):
    half = HEAD_DIM // 2
    inv = ROPE_BASE ** (-jnp.arange(half, dtype=F32) / half)
    ang = pos.astype(F32)[:, None] * jnp.tile(inv, LANE // half)[None, :]
    return jnp.cos(ang), jnp.sin(ang)


def _retention_log_gamma():
    lg = jnp.log1p(-jnp.exp2(-5.0 - jnp.arange(N_HG, dtype=F32)))
    return jnp.broadcast_to(jnp.repeat(lg, HEAD_DIM)[None, :], (ROW_BLK, GROUP_W))


def _row(v):
    return v.reshape(1, -1).astype(F32)


def _rwkv_params(p):
    w2 = p['rwkv_w2']
    a2 = p['rwkv_a2']
    w2p = jnp.concatenate([w2, jnp.zeros_like(a2)], axis=0)
    a2p = jnp.concatenate([jnp.zeros_like(w2), a2], axis=0)
    return (_row(p['rwkv_mu']), _row(p['rwkv_w0']), w2p, _row(p['rwkv_a0']), a2p, p['rwkv_g2'],
            _row(p['rwkv_kk']), _row(p['rwkv_ka']), _row(p['rwkv_rk']), _row(p['rwkv_ln_w']), _row(p['rwkv_ln_b']))


def _rwkv_state_to_tiles(s):
    n = s.shape[0]
    return s.reshape(n, 2, 2, HEAD_DIM, HEAD_DIM).transpose(0, 1, 3, 2, 4).reshape(n * 2 * HEAD_DIM, LANE)


def _rwkv_tiles_to_state(t, n):
    return t.reshape(n, 2, HEAD_DIM, 2, HEAD_DIM).transpose(0, 1, 3, 2, 4).reshape(n, N_HG, HEAD_DIM, HEAD_DIM)


def _diag_heads(s):
    return jnp.stack([s[:, h * HEAD_DIM:(h + 1) * HEAD_DIM, h * HEAD_DIM:(h + 1) * HEAD_DIM] for h in range(N_HG)], axis=1)


def kernel(x_prompt, x_sample, cache_sb_k, cache_sb_v, state_ret, state_hgrn, state_rwkv, state_rwkv_shift,
           page_table, meta_tokens, norm1, norm2, w_in, w_out, ret_norm, hgrn_lb_logits, hgrn_norm,
           sb_q_norm, sb_k_norm, sb_bias, rwkv_mu, rwkv_w0, rwkv_w2, rwkv_a0, rwkv_a2, rwkv_g2, rwkv_kk, rwkv_ka,
           rwkv_rk, rwkv_ln_w, rwkv_ln_b, moe_w_group, moe_b_group, moe_w_expert, moe_b_expert,
           moe_w1, moe_w3, moe_w2):
    bp, seq, dm = x_prompt.shape
    db, ds, _ = x_sample.shape
    depth = w_in.shape[0]
    w = GROUP_W
    assert seq % ROW_BLK == 0 and (db * ds) % ROW_BLK == 0 and ROW_BLK % ds == 0 and ds & (ds - 1) == 0
    assert w_in.shape[2] == LIN_COLS + 4 * w and cache_sb_k.shape[3] * cache_sb_k.shape[4] == w
    tp = seq + SB_QBLK
    nblk = tp // ROW_BLK
    tlen = seq + N_META
    past = page_table.shape[1] * cache_sb_k.shape[2]
    n_groups, e_per = moe_w_expert.shape[2:]
    n_experts = n_groups * e_per
    assert n_experts + n_groups <= ROUTE_W

    xp = jnp.concatenate([jnp.zeros((bp, PAD_FRONT, dm), F32),
                          jnp.broadcast_to(meta_tokens[None], (bp, N_META, dm)).astype(F32), x_prompt], axis=1)
    xp = xp.reshape(bp * tp, dm)
    xs = x_sample.reshape(db * ds, dm)
    null_rows = tuple(b * tp for b in range(bp))

    cos_p, sin_p = _rope_tables(jnp.maximum(jnp.arange(tp) - PAD_FRONT, 0))
    cos_s, sin_s = _rope_tables(jnp.tile(past + jnp.arange(ds), ROW_BLK // ds))
    lgam = _retention_log_gamma()
    cache_k = cache_sb_k.reshape(cache_sb_k.shape[:3] + (w,))
    cache_v = cache_sb_v.reshape(cache_sb_v.shape[:3] + (w,))
    tile_heads = lambda v: jnp.tile(v, N_HG)[None, :].astype(F32)

    prompt_rows, sample_rows = [], []
    for l in range(depth):
        wa = w_in[l][:, :LIN_COLS].astype(BF16)
        wb = w_in[l][:, LIN_COLS:].astype(BF16)
        g1 = _row(norm1[l])
        lin_args = (lgam, _row(ret_norm[l]), hgrn_lb_logits.astype(F32), _row(hgrn_norm[l]),
                    tile_heads(sb_q_norm[l]), tile_heads(sb_k_norm[l]))
        rwkv_prm = _rwkv_params(dict(rwkv_mu=rwkv_mu[l], rwkv_w0=rwkv_w0[l], rwkv_w2=rwkv_w2[l], rwkv_a0=rwkv_a0[l],
                                     rwkv_a2=rwkv_a2[l], rwkv_g2=rwkv_g2[l], rwkv_kk=rwkv_kk[l], rwkv_ka=rwkv_ka[l],
                                     rwkv_rk=rwkv_rk[l], rwkv_ln_w=rwkv_ln_w[l], rwkv_ln_b=rwkv_ln_b[l]))
        wo = w_out[l].astype(BF16)
        wr = jnp.zeros((dm, ROUTE_W), F32)
        wr = wr.at[:, :n_experts].set(moe_w_expert[l].reshape(dm, n_experts))
        wr = wr.at[:, n_experts:n_experts + n_groups].set(moe_w_group[l])
        wr_hi = wr.astype(BF16)
        wr_lo = (wr - wr_hi.astype(F32)).astype(BF16)
        br = jnp.zeros((1, ROUTE_W), F32)
        br = br.at[0, :n_experts].set(moe_b_expert[l].reshape(n_experts))
        br = br.at[0, n_experts:n_experts + n_groups].set(moe_b_group[l])
        router = (wo, _row(norm2[l]), wr_hi, wr_lo, br)

        pa, pb = _in_proj(xp, g1, wa, wb)
        oab, qs, knf, knb, vb, s_ret, s_hg = _lin_mix_prompt(pa, cos_p, sin_p, *lin_args, l, bp, nblk)
        oc = _sb_prompt(sb_bias[l].astype(F32), qs, knb, vb, bp, tp)
        od, s_rw = _rwkv_prompt(pb, rwkv_prm, bp, nblk)
        x1, h2, gates = _out_router(xp, oab, oc, od, *router, n_experts, n_groups, null_rows)
        xp = _moe(x1, h2, gates, moe_w1, moe_w3, moe_w2, l)
        real = lambda a: a.reshape(bp, tp, -1)[:, PAD_FRONT:]
        prompt_rows.append((real(knf).reshape(bp, tlen, N_HG, HEAD_DIM),
                            real(pa[:, LIN_COLS - w:]).reshape(bp, tlen, N_HG, HEAD_DIM),
                            _diag_heads(s_ret), _diag_heads(s_hg), _diag_heads(s_rw).swapaxes(-1, -2),
                            pb.reshape(bp, tp, -1)[:, -1]))

        sa, sb = _in_proj(xs, g1, wa, wb)
        stack = lambda s: s.reshape(db, w, HEAD_DIM)
        oab, qs, knf, vf, s_ret, s_hg = _lin_mix_sample(sa, cos_s, sin_s, *lin_args, stack(state_ret[l]),
                                                        stack(state_hgrn[l]), l, ds)
        oc = _sb_sample(page_table, sb_bias[l].astype(F32), qs, knf, vf, cache_k, cache_v, l, ds)
        cw = sb.shape[-1]
        ovr = jnp.concatenate([state_rwkv_shift[l][:, None, :], jnp.zeros((db, ds - 1, cw), F32)], axis=1)
        od, s_rw = _rwkv_sample(sb.reshape(-1, ROW_BLK, cw), ovr.reshape(-1, ROW_BLK, cw),
                                _rwkv_state_to_tiles(state_rwkv[l]), rwkv_prm, ds)
        x1, h2, gates = _out_router(xs, oab, oc, od.reshape(db * ds, w), *router, n_experts, n_groups, ())
        xs = _moe(x1, h2, gates, moe_w1, moe_w3, moe_w2, l)
        sample_rows.append((knf.reshape(db, ds, N_HG, HEAD_DIM), vf.reshape(db, ds, N_HG, HEAD_DIM),
                            s_ret.reshape(db, N_HG, HEAD_DIM, HEAD_DIM), s_hg.reshape(db, N_HG, HEAD_DIM, HEAD_DIM),
                            _rwkv_tiles_to_state(s_rw, db), sb.reshape(db, ds, cw)[:, -1]))

    y_prompt = xp.reshape(bp, tp, dm)[:, SB_QBLK:]
    y_sample = xs.reshape(db, ds, dm)
    stacked_p = [jnp.stack(r) for r in zip(*prompt_rows)]
    stacked_s = [jnp.stack(r) for r in zip(*sample_rows)]
    return (y_prompt, y_sample, *stacked_p, *stacked_s)
```

```python
import functools
import math

import numpy as np
import jax
import jax.numpy as jnp
from jax import lax
from jax.experimental import pallas as pl
from jax.experimental.pallas import tpu as pltpu

F32 = jnp.float32
BF16 = jnp.bfloat16

HEAD_DIM = 64
N_HG = 4
GROUP_W = N_HG * HEAD_DIM
N_META = 16
LANE = 128
ROW_BLK = 128
SB_QBLK = 256
PAD_FRONT = SB_QBLK - N_META
LIN_COLS = 11 * GROUP_W
NORM_EPS = 1e-6
RWKV_LN_EPS = 64e-5
ROPE_BASE = 10000.0
LOG2E = math.log2(math.e)
SB_SCALE = HEAD_DIM ** -0.5 * LOG2E
E_PER_GROUP = 8
VMEM_LIMIT = 56 * 1024 * 1024


def _iota(shape, dim):
    return lax.broadcasted_iota(jnp.int32, shape, dim)


def _mask01(cond):
    return jnp.where(cond, 1.0, 0.0).astype(BF16)


def _dot(a, b):
    return jnp.dot(a.astype(BF16), b.astype(BF16), preferred_element_type=F32)


def _dot_nt(a, b):
    return lax.dot_general(a.astype(BF16), b.astype(BF16), (((1,), (1,)), ((), ())),
                           preferred_element_type=F32)


def _dot_tn(a, b):
    return lax.dot_general(a.astype(BF16), b.astype(BF16), (((0,), (0,)), ((), ())),
                           preferred_element_type=F32)


def _split(x, n):
    parts = []
    r = x
    for _ in range(n):
        h = r.astype(BF16)
        parts.append(h)
        r = r - h.astype(F32)
    return parts


def _dotx(x, m, n=2):
    out = None
    for p in _split(x, n):
        t = jnp.dot(p, m, preferred_element_type=F32)
        out = t if out is None else out + t
    return out


def _xdot(m, x, n=3):
    out = None
    for p in _split(x, n):
        t = jnp.dot(m, p, preferred_element_type=F32)
        out = t if out is None else out + t
    return out


def _head_ones(w):
    return _mask01((_iota((w, w), 0) >> 6) == (_iota((w, w), 1) >> 6))


def _sigmoid(x):
    return 1.0 / (1.0 + jnp.exp(-x))


def _softplus(x):
    return jnp.maximum(x, 0.0) + jnp.log(1.0 + jnp.exp(-jnp.abs(x)))


def _softplus2(x):
    return jnp.maximum(x, 0.0) + jnp.log2(1.0 + jnp.exp2(-jnp.abs(x)))


def _head_rms(x, gain, ones):
    ms = _dotx(x * x, ones, 2) * (1.0 / HEAD_DIM)
    return x * lax.rsqrt(ms + NORM_EPS) * gain


def _row_tile(n):
    for t in (512, 256, 128):
        if n % t == 0:
            return t
    raise ValueError(f"row count {n} is not a multiple of {ROW_BLK}")


def _cparams(sem, flags=None):
    return pltpu.CompilerParams(dimension_semantics=sem, vmem_limit_bytes=VMEM_LIMIT, flags=flags)


def _in_proj_kernel(x_ref, g_ref, wa_ref, wb_ref, oa_ref, ob_ref):
    x = x_ref[...]
    ms = jnp.mean(x * x, axis=-1, keepdims=True)
    h = (x * lax.rsqrt(ms + NORM_EPS) * g_ref[...]).astype(BF16)
    oa_ref[...] = jnp.dot(h, wa_ref[...], preferred_element_type=F32)
    ob_ref[...] = jnp.dot(h, wb_ref[...], preferred_element_type=F32)


def _in_proj(x, g, wa, wb):
    n, d = x.shape
    tm = 256 if n % 256 == 0 else ROW_BLK
    return pl.pallas_call(
        _in_proj_kernel,
        grid=(n // tm,),
        in_specs=[pl.BlockSpec((tm, d), lambda i: (i, 0)),
                  pl.BlockSpec((1, d), lambda i: (0, 0)),
                  pl.BlockSpec(wa.shape, lambda i: (0, 0)),
                  pl.BlockSpec(wb.shape, lambda i: (0, 0))],
        out_specs=[pl.BlockSpec((tm, wa.shape[1]), lambda i: (i, 0)),
                   pl.BlockSpec((tm, wb.shape[1]), lambda i: (i, 0))],
        out_shape=[jax.ShapeDtypeStruct((n, wa.shape[1]), F32),
                   jax.ShapeDtypeStruct((n, wb.shape[1]), F32)],
        compiler_params=_cparams(("parallel",)),
        name="in_proj",
    )(x, g, wa, wb)


def _gla_block(q, k, v, lw, states, sub, chain):
    rows, w = q.shape
    nsub = rows // sub
    shift = int(math.log2(sub))
    r_i = _iota((rows, rows), 0)
    c_i = _iota((rows, rows), 1)
    same = (r_i >> shift) == (c_i >> shift)
    tri = _mask01(same & (c_i <= r_i))
    blk = _mask01(same)
    sub_of_row = _mask01((_iota((rows, LANE), 0) >> shift) == _iota((rows, LANE), 1))
    bc = bl = dsum = None
    for p in _split(lw, 3):
        t1 = jnp.dot(tri, p, preferred_element_type=F32)
        t2 = jnp.dot(blk, p, preferred_element_type=F32)
        t3 = lax.dot_general(p, sub_of_row, (((0,), (0,)), ((), ())), preferred_element_type=F32)
        bc = t1 if bc is None else bc + t1
        bl = t2 if bl is None else bl + t2
        dsum = t3 if dsum is None else dsum + t3
    dcol = jnp.exp(dsum)
    qt = q * jnp.exp(bc)
    kh = k * jnp.exp(bl - bc)
    ones = _head_ones(w)

    local = _iota((rows, w), 0) & (sub - 1)
    o_intra = jnp.zeros((rows, w), F32)
    for dist in range(sub):
        back = (lambda x: x) if dist == 0 else (lambda x: pltpu.roll(x, dist, 0))
        e = jnp.exp(jnp.where(local >= dist, bc - back(bc), -1e30))
        p = q * back(k) * e
        o_intra = o_intra + _dot(p, ones) * back(v)

    bdmask = (_iota((w, w), 0) >> 6) == (_iota((w, w), 1) >> 6)
    outs = []
    new_states = []
    s = states if chain else None
    for i in range(nsub):
        sl = slice(i * sub, (i + 1) * sub)
        if not chain:
            s = states[i]
        outs.append(_dot(qt[sl], s))
        kv = _dot_tn(kh[sl], v[sl])
        s = s * dcol[:, i:i + 1] + jnp.where(bdmask, kv, 0.0)
        if not chain:
            new_states.append(s)
    o = o_intra + jnp.concatenate(outs, axis=0)
    return o, (s if chain else new_states)


def _retention_block(q, k, v, lgam, s):
    rows, w = q.shape
    width = N_HG * rows
    t_row = _iota((rows, w), 0).astype(F32)
    qd = q * jnp.exp((t_row + 1.0) * lgam)
    kd = k * jnp.exp((rows - 1.0 - t_row) * lgam)
    left = _iota((rows, LANE), 1) < HEAD_DIM

    def split_heads(x):
        x = x.astype(BF16)
        zero = jnp.zeros_like(x)
        return jnp.concatenate([jnp.where(left, x, zero), jnp.where(left, zero, x)], axis=0)

    dist = (_iota((rows, width), 0) - (_iota((rows, width), 1) & (rows - 1))).astype(F32)
    lg_heads = jnp.concatenate([jnp.broadcast_to(lgam[:, h * HEAD_DIM:h * HEAD_DIM + 1], (rows, rows))
                                for h in range(N_HG)], axis=1)
    dmat = jnp.where(dist >= 0.0, jnp.exp(jnp.maximum(dist, 0.0) * lg_heads), 0.0)
    scores = jnp.concatenate([_dot_nt(q[:, p * LANE:(p + 1) * LANE], split_heads(k[:, p * LANE:(p + 1) * LANE]))
                              for p in range(2)], axis=1) * dmat
    o_intra = jnp.concatenate([_dot(scores[:, p * 2 * LANE:(p + 1) * 2 * LANE], split_heads(v[:, p * LANE:(p + 1) * LANE]))
                               for p in range(2)], axis=1)
    o = o_intra + _dot(qd, s)
    ones_rl = jnp.ones((rows, LANE), BF16)
    dsum = None
    for part in _split(lgam, 3):
        t3 = lax.dot_general(part, ones_rl, (((0,), (0,)), ((), ())), preferred_element_type=F32)
        dsum = t3 if dsum is None else dsum + t3
    bdmask = (_iota((w, w), 0) >> 6) == (_iota((w, w), 1) >> 6)
    s = s * jnp.exp(dsum)[:, 0:1] + jnp.where(bdmask, _dot_tn(kd, v), 0.0)
    return o, s


def _lin_mix_math(p, cos, sin, lgam, retw, lb_logits, hgw, qnw, knw, layer, st_ret, st_hg, sub, chain):
    w = GROUP_W
    qa, ka, va, ga, qb, fb, ib, gb, qc, kc, vc = [p[:, i * w:(i + 1) * w] for i in range(11)]
    rows = p.shape[0]
    ones = _head_ones(w)
    lane = _iota((rows, LANE), 1)
    first_half = (lane & (HEAD_DIM - 1)) < (HEAD_DIM // 2)

    def rope(x):
        halves = []
        for hp in range(w // LANE):
            xh = x[:, hp * LANE:(hp + 1) * LANE]
            rot = jnp.where(first_half, -pltpu.roll(xh, LANE - HEAD_DIM // 2, 1), pltpu.roll(xh, HEAD_DIM // 2, 1))
            halves.append(xh * cos + rot * sin)
        return jnp.concatenate(halves, axis=1)

    q_ret = rope(qa)
    k_ret = rope(ka) * (HEAD_DIM ** -0.5)
    if chain:
        o_ret, st_ret = _retention_block(q_ret, k_ret, va, lgam, st_ret)
    else:
        o_ret, st_ret = _gla_block(q_ret, k_ret, va, lgam, st_ret, sub, chain)
    o_a = _head_rms(o_ret, retw, ones) * (ga * _sigmoid(ga))

    lg = [lb_logits[d:d + 1, :] for d in range(lb_logits.shape[0])]
    mx = functools.reduce(jnp.maximum, lg)
    ex = [jnp.exp(row - mx) for row in lg]
    lb = sum(ex[1:layer + 1], jnp.zeros_like(mx)) / sum(ex[1:], ex[0])
    log_sig = jnp.minimum(fb, 0.0) - jnp.log1p(jnp.exp(-jnp.abs(fb)))
    t_a = jnp.broadcast_to(jnp.log(lb), fb.shape)
    t_b = jnp.log1p(-lb) + log_sig
    logf = jnp.maximum(t_a, t_b) + jnp.log1p(jnp.exp(-jnp.abs(t_a - t_b)))
    k_hg = (1.0 - lb) * _sigmoid(-fb)
    o_hg, st_hg = _gla_block(qb, k_hg, ib, logf, st_hg, sub, chain)
    o_b = _head_rms(o_hg, hgw, ones) * (gb * _sigmoid(gb))

    qn = _head_rms(qc, qnw, ones)
    kn = _head_rms(kc, knw, ones)
    return o_a, o_b, qn, kn, vc, st_ret, st_hg


def _lin_mix_prompt_kernel(p_ref, cos_ref, sin_ref, lgam_ref, retw_ref, lbl_ref, hgw_ref, qnw_ref, knw_ref,
                           oab_ref, qs_ref, knf_ref, knb_ref, vb_ref, sret_ref, shg_ref,
                           st_ret, st_hg, *, layer, sub):
    i = pl.program_id(1)

    @pl.when(i == 0)
    def _():
        st_ret[...] = jnp.zeros_like(st_ret)
        st_hg[...] = jnp.zeros_like(st_hg)

    o_a, o_b, qn, kn, vc, s1, s2 = _lin_mix_math(
        p_ref[...], cos_ref[...], sin_ref[...], lgam_ref[...], retw_ref[...], lbl_ref, hgw_ref[...],
        qnw_ref[...], knw_ref[...], layer, st_ret[...], st_hg[...], sub, True)
    st_ret[...] = s1
    st_hg[...] = s2
    oab_ref[...] = jnp.concatenate([o_a, o_b], axis=1)
    qs_ref[...] = (qn * SB_SCALE).astype(BF16)
    knf_ref[...] = kn
    knb_ref[...] = kn.astype(BF16)
    vb_ref[...] = vc.astype(BF16)

    @pl.when(i == pl.num_programs(1) - 1)
    def _():
        sret_ref[...] = s1
        shg_ref[...] = s2


def _lin_mix_prompt(proj, cos, sin, lgam, retw, lbl, hgw, qnw, knw, layer, nb, nblk):
    n = proj.shape[0]
    w = GROUP_W
    row = lambda b, i: (b * nblk + i, 0)
    const = lambda b, i: (0, 0)
    outs = pl.pallas_call(
        functools.partial(_lin_mix_prompt_kernel, layer=layer, sub=16),
        grid=(nb, nblk),
        in_specs=[pl.BlockSpec((ROW_BLK, LIN_COLS), row),
                  pl.BlockSpec((ROW_BLK, LANE), lambda b, i: (i, 0)),
                  pl.BlockSpec((ROW_BLK, LANE), lambda b, i: (i, 0)),
                  pl.BlockSpec((ROW_BLK, w), const), pl.BlockSpec((1, w), const),
                  pl.BlockSpec(lbl.shape, const), pl.BlockSpec((1, w), const),
                  pl.BlockSpec((1, w), const), pl.BlockSpec((1, w), const)],
        out_specs=[pl.BlockSpec((ROW_BLK, 2 * w), row),
                   pl.BlockSpec((ROW_BLK, w), row), pl.BlockSpec((ROW_BLK, w), row),
                   pl.BlockSpec((ROW_BLK, w), row), pl.BlockSpec((ROW_BLK, w), row),
                   pl.BlockSpec((None, w, w), lambda b, i: (b, 0, 0)),
                   pl.BlockSpec((None, w, w), lambda b, i: (b, 0, 0))],
        out_shape=[jax.ShapeDtypeStruct((n, 2 * w), F32),
                   jax.ShapeDtypeStruct((n, w), BF16), jax.ShapeDtypeStruct((n, w), F32),
                   jax.ShapeDtypeStruct((n, w), BF16), jax.ShapeDtypeStruct((n, w), BF16),
                   jax.ShapeDtypeStruct((nb, w, w), F32), jax.ShapeDtypeStruct((nb, w, w), F32)],
        scratch_shapes=[pltpu.VMEM((w, w), F32), pltpu.VMEM((w, w), F32)],
        compiler_params=_cparams(("parallel", "arbitrary")),
        name="lin_mix_prompt",
    )(proj, cos, sin, lgam, retw, lbl, hgw, qnw, knw)
    return outs


def _lin_mix_sample_kernel(p_ref, cos_ref, sin_ref, lgam_ref, retw_ref, lbl_ref, hgw_ref, qnw_ref, knw_ref,
                           sret_in, shg_in,
                           oab_ref, qs_ref, knf_ref, vf_ref, sret_out, shg_out, *, layer, sub):
    w = GROUP_W
    nseq = ROW_BLK // sub
    bdmask = (_iota((w, w), 0) >> 6) == (_iota((w, w), 1) >> 6)
    rep = _mask01(_iota((HEAD_DIM, w), 0) == (_iota((HEAD_DIM, w), 1) & (HEAD_DIM - 1)))
    rep_t = _mask01((_iota((w, HEAD_DIM), 0) & (HEAD_DIM - 1)) == _iota((w, HEAD_DIM), 1))

    def expand(ref):
        return [jnp.where(bdmask, _dotx(ref[j], rep, 3), 0.0) for j in range(nseq)]

    def extract(ref, states):
        for j in range(nseq):
            ref[j] = _dotx(states[j], rep_t, 3)

    o_a, o_b, qn, kn, vc, s1, s2 = _lin_mix_math(
        p_ref[...], cos_ref[...], sin_ref[...], lgam_ref[...], retw_ref[...], lbl_ref, hgw_ref[...],
        qnw_ref[...], knw_ref[...], layer, expand(sret_in), expand(shg_in), sub, False)
    extract(sret_out, s1)
    extract(shg_out, s2)
    oab_ref[...] = jnp.concatenate([o_a, o_b], axis=1)
    qs_ref[...] = qn * SB_SCALE
    knf_ref[...] = kn
    vf_ref[...] = vc


def _lin_mix_sample(proj, cos, sin, lgam, retw, lbl, hgw, qnw, knw, s_ret, s_hg, layer, dec_seq):
    n = proj.shape[0]
    w = GROUP_W
    nseq = ROW_BLK // dec_seq
    row = lambda i: (i, 0)
    const = lambda i: (0, 0)
    st_spec = pl.BlockSpec((nseq, w, HEAD_DIM), lambda i: (i, 0, 0))
    return pl.pallas_call(
        functools.partial(_lin_mix_sample_kernel, layer=layer, sub=dec_seq),
        grid=(n // ROW_BLK,),
        in_specs=[pl.BlockSpec((ROW_BLK, LIN_COLS), row),
                  pl.BlockSpec((ROW_BLK, LANE), const), pl.BlockSpec((ROW_BLK, LANE), const),
                  pl.BlockSpec((ROW_BLK, w), const), pl.BlockSpec((1, w), const),
                  pl.BlockSpec(lbl.shape, const), pl.BlockSpec((1, w), const),
                  pl.BlockSpec((1, w), const), pl.BlockSpec((1, w), const),
                  st_spec, st_spec],
        out_specs=[pl.BlockSpec((ROW_BLK, 2 * w), row),
                   pl.BlockSpec((ROW_BLK, w), row), pl.BlockSpec((ROW_BLK, w), row),
                   pl.BlockSpec((ROW_BLK, w), row), st_spec, st_spec],
        out_shape=[jax.ShapeDtypeStruct((n, 2 * w), F32),
                   jax.ShapeDtypeStruct((n, w), F32), jax.ShapeDtypeStruct((n, w), F32),
                   jax.ShapeDtypeStruct((n, w), F32),
                   jax.ShapeDtypeStruct(s_ret.shape, F32), jax.ShapeDtypeStruct(s_hg.shape, F32)],
        compiler_params=_cparams(("parallel",)),
        name="lin_mix_sample",
    )(proj, cos, sin, lgam, retw, lbl, hgw, qnw, knw, s_ret, s_hg)


def _sb_consts(rows, tmask):
    j_i = _iota((LANE, 2 * LANE), 0)
    s_i = _iota((LANE, 2 * LANE), 1)
    ucat = _mask01((s_i >= LANE) | (j_i >= s_i))
    causal = _iota((rows, LANE), 1) < (_iota((rows, LANE), 0) & tmask)
    return ucat, causal


def _sb_block(z, carry, ucat, causal):
    sp = _softplus2(z)
    if causal is not None:
        sp = jnp.where(causal, sp, 0.0)
    t = _dotx(sp, ucat, 2)
    a = jnp.exp2(z - (carry + t[:, :LANE]))
    if causal is not None:
        a = jnp.where(causal, a, 0.0)
    return a, carry + t[:, LANE:]


def _sb_prompt_kernel(bias_ref, q_ref, k_ref, v_ref, o_ref):
    i = pl.program_id(1)
    qb = SB_QBLK
    kpq = qb // ROW_BLK
    q = q_ref[...]
    left = _iota((ROW_BLK, LANE), 1) < HEAD_DIM
    zero = jnp.zeros((ROW_BLK, LANE), BF16)
    width = N_HG * ROW_BLK
    lane = _iota((qb, width), 1)
    bias_row = jnp.zeros((qb, width), F32)
    for h in range(N_HG):
        bias_row = jnp.where((lane >> 7) == h, bias_ref[h] * LOG2E, bias_row)
    sk = lane & (ROW_BLK - 1)
    tq = _iota((qb, width), 0)
    r2 = _iota((2 * LANE, 2 * LANE), 0)
    c2 = _iota((2 * LANE, 2 * LANE), 1)
    ubd = _mask01(((r2 >> 7) == (c2 >> 7)) & ((r2 & (LANE - 1)) >= (c2 & (LANE - 1))))

    def split_heads(x):
        return jnp.concatenate([jnp.where(left, x, zero), jnp.where(left, zero, x)], axis=0)

    def step(kb, carry, acc0, acc1, mask):
        start = pl.multiple_of(kb * ROW_BLK, ROW_BLK)
        kblk = k_ref[pl.ds(start, ROW_BLK), :]
        vblk = v_ref[pl.ds(start, ROW_BLK), :]
        z = jnp.concatenate([_dot_nt(q[:, p * LANE:(p + 1) * LANE], split_heads(kblk[:, p * LANE:(p + 1) * LANE]))
                             for p in range(2)], axis=1) + bias_row
        sp = _softplus2(z)
        if mask is not None:
            sp = jnp.where(mask, sp, 0.0)
        t = jnp.concatenate([_dotx(sp[:, p * 2 * LANE:(p + 1) * 2 * LANE], ubd, 2) for p in range(2)], axis=1)
        a = jnp.exp2(z - (carry + t))
        if mask is not None:
            a = jnp.where(mask, a, 0.0)
        total = jnp.concatenate([jnp.broadcast_to(t[:, h * ROW_BLK:h * ROW_BLK + 1], (qb, ROW_BLK))
                                 for h in range(N_HG)], axis=1)
        ab = a.astype(BF16)
        accs = []
        for p, acc in enumerate((acc0, acc1)):
            accs.append(acc + jnp.dot(ab[:, p * 2 * LANE:(p + 1) * 2 * LANE],
                                      split_heads(vblk[:, p * LANE:(p + 1) * LANE]), preferred_element_type=F32))
        return carry + total, accs[0], accs[1]

    zeros = jnp.zeros((qb, LANE), F32)
    state = (jnp.zeros((qb, width), F32), zeros, zeros)
    for d in reversed(range(kpq)):
        state = step(i * kpq + d, *state, sk + d * ROW_BLK < tq)

    def earlier_blocks(j, c):
        for d in range(kpq):
            c = step((i - j) * kpq - 1 - d, *c, None)
        return c

    state = lax.fori_loop(0, i, earlier_blocks, state)
    o_ref[...] = jnp.concatenate([state[1], state[2]], axis=1)


def _sb_prompt(bias, qs, kb, vb, nb, tp):
    n, w = qs.shape
    nq = tp // SB_QBLK
    return pl.pallas_call(
        _sb_prompt_kernel,
        grid=(nb, nq),
        in_specs=[pl.BlockSpec(memory_space=pltpu.SMEM),
                  pl.BlockSpec((SB_QBLK, w), lambda b, i: (b * nq + i, 0)),
                  pl.BlockSpec((tp, w), lambda b, i: (b, 0)),
                  pl.BlockSpec((tp, w), lambda b, i: (b, 0))],
        out_specs=pl.BlockSpec((SB_QBLK, w), lambda b, i: (b * nq + i, 0)),
        out_shape=jax.ShapeDtypeStruct((n, w), F32),
        compiler_params=_cparams(("parallel", "arbitrary")),
        name="sb_prompt",
    )(bias, qs, kb, vb)


def _sb_sample_kernel(pt_ref, bias_ref, q_ref, kn_ref, vn_ref, *rest, n_pages, dec_seq):
    del pt_ref
    k_pages = rest[:n_pages]
    v_pages = rest[n_pages:2 * n_pages]
    o_ref = rest[2 * n_pages]
    w = GROUP_W
    rows = N_HG * dec_seq
    q = q_ref[...]
    head_of_lane = _iota((dec_seq, w), 1) >> 6
    qbd = jnp.concatenate([jnp.where(head_of_lane == h, q, 0.0) for h in range(N_HG)], axis=0).astype(BF16)
    bias_col = jnp.concatenate([jnp.full((dec_seq, 1), bias_ref[h] * LOG2E, F32) for h in range(N_HG)], axis=0)
    ucat, causal = _sb_consts(rows, dec_seq - 1)
    pad = jnp.zeros((LANE - dec_seq, w), F32)
    k_new = jnp.concatenate([kn_ref[...], pad], axis=0)
    v_new = jnp.concatenate([vn_ref[...], pad], axis=0)
    a, carry = _sb_block(_dot_nt(qbd, k_new) + bias_col, jnp.zeros((rows, LANE), F32), ucat, causal)
    acc = _dot(a, v_new)
    for p in reversed(range(n_pages)):
        a, carry = _sb_block(_dot_nt(qbd, k_pages[p][...]) + bias_col, carry, ucat, None)
        acc = acc + _dot(a, v_pages[p][...])
    out = jnp.zeros((dec_seq, w), F32)
    for h in range(N_HG):
        out = out + jnp.where(head_of_lane == h, acc[h * dec_seq:(h + 1) * dec_seq], 0.0)
    o_ref[...] = out


def _sb_sample(page_table, bias, qs, kn, vn, cache_k, cache_v, layer, dec_seq):
    n, w = qs.shape
    db, n_pages = page_table.shape
    page = cache_k.shape[2]
    assert page == LANE and dec_seq % 8 == 0
    row = lambda b, pt: (b, 0)
    page_specs = [pl.BlockSpec((None, None, page, w), lambda b, pt, p=p: (layer, pt[b, p], 0, 0))
                  for p in range(n_pages)]
    grid_spec = pltpu.PrefetchScalarGridSpec(
        num_scalar_prefetch=1,
        grid=(db,),
        in_specs=[pl.BlockSpec(memory_space=pltpu.SMEM),
                  pl.BlockSpec((dec_seq, w), row), pl.BlockSpec((dec_seq, w), row), pl.BlockSpec((dec_seq, w), row)]
                 + page_specs + page_specs,
        out_specs=pl.BlockSpec((dec_seq, w), row),
    )
    return pl.pallas_call(
        functools.partial(_sb_sample_kernel, n_pages=n_pages, dec_seq=dec_seq),
        grid_spec=grid_spec,
        out_shape=jax.ShapeDtypeStruct((n, w), F32),
        compiler_params=_cparams(("arbitrary",)),
        name="sb_sample",
    )(page_table, bias, qs, kn, vn, *([cache_k] * n_pages), *([cache_v] * n_pages))


def _rwkv_prologue(rw, prev, mu, w0, w2p, a0, a2p, g2, kkp, ka, rk, ones):
    w = GROUP_W
    xm = rw + (prev - rw) * mu
    r = xm[:, 0:w]
    k = xm[:, w:2 * w]
    v = xm[:, 2 * w:3 * w]
    wa = xm[:, 3 * w:3 * w + LANE]
    gl = xm[:, 3 * w + LANE:]
    wd = w0 + _dot(jnp.tanh(wa), w2p)
    log_decay = -jnp.exp(-_softplus(-wd) - 0.5)
    a = _sigmoid(a0 + _dot(wa, a2p))
    g = _dot(_sigmoid(gl), g2)
    kk = k * kkp
    kk = kk * lax.rsqrt(jnp.maximum(_dotx(kk * kk, ones, 2), 1e-12))
    k_rw = k * (1.0 + (a - 1.0) * ka)
    bonus = _dotx(r * k_rw * rk, ones, 2) * v
    return r, log_decay, k_rw, v, kk, kk * a, g, bonus


def _rwkv_group_norm(o, lnw, lnb, bonus, g, ones):
    mu_h = _dotx(o, ones, 2) * (1.0 / HEAD_DIM)
    dlt = o - mu_h
    var = _dotx(dlt * dlt, ones, 2) * (1.0 / HEAD_DIM)
    return (dlt * lax.rsqrt(var + RWKV_LN_EPS) * lnw + lnb + bonus) * g


def _rwkv_chunk_kernel(rw_ref, mu_ref, w0_ref, w2p_ref, a0_ref, a2p_ref, g2_ref, kkp_ref, ka_ref, rk_ref, lnw_ref, lnb_ref,
                       o_ref, s_out_ref, carry_s, state_s, *, sub):
    i = pl.program_id(1)
    w = GROUP_W
    rows = ROW_BLK
    nsub = rows // sub
    shift = int(math.log2(sub))
    ones = _head_ones(w)

    @pl.when(i == 0)
    def _():
        carry_s[...] = jnp.zeros_like(carry_s)
        state_s[...] = jnp.zeros_like(state_s)

    rw = rw_ref[...]
    prev = jnp.where(_iota((rows, 1), 0) == 0, carry_s[0:1, :], pltpu.roll(rw, 1, 0))
    carry_s[0:1, :] = rw[rows - 1:rows, :]
    r, lw, k, v, kk, b, g, bonus = _rwkv_prologue(
        rw, prev, mu_ref[...], w0_ref[...], w2p_ref[...], a0_ref[...], a2p_ref[...], g2_ref[...],
        kkp_ref[...], ka_ref[...], rk_ref[...], ones)

    r_i = _iota((rows, rows), 0)
    c_i = _iota((rows, rows), 1)
    same = (r_i >> shift) == (c_i >> shift)
    tri = _mask01(same & (c_i <= r_i))
    blk = _mask01(same)
    sub_of_row = _mask01((_iota((rows, LANE), 0) >> shift) == _iota((rows, LANE), 1))
    c = cl = dsum = None
    for part in _split(lw, 3):
        t1 = jnp.dot(tri, part, preferred_element_type=F32)
        t2 = jnp.dot(blk, part, preferred_element_type=F32)
        t3 = lax.dot_general(part, sub_of_row, (((0,), (0,)), ((), ())), preferred_element_type=F32)
        c = t1 if c is None else c + t1
        cl = t2 if cl is None else cl + t2
        dsum = t3 if dsum is None else dsum + t3
    dcol = jnp.exp(dsum)
    kkd = kk * jnp.exp(c - lw)
    rd = r * jnp.exp(c)
    tail = jnp.exp(cl - c)
    khat = k * tail
    bhat = b * tail
    kkw = kk * jnp.exp(-lw)

    local = _iota((rows, w), 0) & (sub - 1)
    y_intra = jnp.zeros((rows, w), F32)
    o_acc = jnp.zeros((rows, w), F32)
    abk = [None]
    abr = []
    for dist in range(sub):
        back = (lambda x: x) if dist == 0 else (lambda x: pltpu.roll(x, dist, 0))
        e = jnp.exp(jnp.where(local >= dist, c - back(c), -1e30))
        kb = back(k) * e
        bb = back(b) * e
        vb = back(v)
        o_acc = o_acc + _dotx(r * kb, ones, 2) * vb
        abr.append(_dotx(r * bb, ones, 2))
        if dist > 0:
            y_intra = y_intra + _dotx(kkw * kb, ones, 2) * vb
            abk.append(_dotx(kkw * bb, ones, 2))

    bdmask = (_iota((w, w), 0) >> 6) == (_iota((w, w), 1) >> 6)
    m = state_s[...]
    us = []
    o_state = []
    for j in range(nsub):
        sl = slice(j * sub, (j + 1) * sub)
        from_state = _dot(jnp.concatenate([kkd[sl], rd[sl]], axis=0), m)
        y = y_intra[sl] + from_state[:sub]
        o_state.append(from_state[sub:])
        u_rows = []
        for t in range(sub):
            acc = y[t:t + 1, :]
            for s in range(t):
                acc = acc - abk[t - s][j * sub + t:j * sub + t + 1, :] * u_rows[s]
            u_rows.append(acc)
        u = jnp.concatenate(u_rows, axis=0)
        us.append(u)
        kv = _dot_tn(jnp.concatenate([khat[sl], -bhat[sl]], axis=0), jnp.concatenate([v[sl], u], axis=0))
        m = m * dcol[:, j:j + 1] + jnp.where(bdmask, kv, 0.0)
    state_s[...] = m
    u_all = jnp.concatenate(us, axis=0)
    o = o_acc + jnp.concatenate(o_state, axis=0)
    for dist in range(sub):
        o = o - abr[dist] * (u_all if dist == 0 else pltpu.roll(u_all, dist, 0))
    o_ref[...] = _rwkv_group_norm(o, lnw_ref[...], lnb_ref[...], bonus, g, ones)

    @pl.when(i == pl.num_programs(1) - 1)
    def _():
        s_out_ref[...] = m


def _rwkv_prompt(rw, prm, nb, nblk):
    w = GROUP_W
    n, cw = rw.shape
    const = lambda b, i: (0, 0)
    row = lambda b, i: (b * nblk + i, 0)
    return pl.pallas_call(
        functools.partial(_rwkv_chunk_kernel, sub=16),
        grid=(nb, nblk),
        in_specs=[pl.BlockSpec((ROW_BLK, cw), row)] + [pl.BlockSpec(a.shape, const) for a in prm],
        out_specs=[pl.BlockSpec((ROW_BLK, w), row), pl.BlockSpec((None, w, w), lambda b, i: (b, 0, 0))],
        out_shape=[jax.ShapeDtypeStruct((n, w), F32), jax.ShapeDtypeStruct((nb, w, w), F32)],
        scratch_shapes=[pltpu.VMEM((8, cw), F32), pltpu.VMEM((w, w), F32)],
        compiler_params=_cparams(("parallel", "arbitrary")),
        name="rwkv_prompt",
    )(rw, *prm)


def _rwkv_scan(tiles, tseq, s0, op_refs, vt_ref, acc_ref):
    g = len(tiles)
    hd = HEAD_DIM
    kk_ref, w_ref, bk_ref, kr_ref, rr_ref = op_refs
    ones_h = _head_ones(LANE)
    ones_f = jnp.ones((LANE, LANE), BF16)
    j2 = _mask01((_iota((LANE, 2 * LANE), 0) >> 6) == (_iota((LANE, 2 * LANE), 1) >> 7))
    lane_t = _iota((hd, LANE), 1)
    left = lane_t < hd
    x0 = [vt_ref[rb, p, 0:hd, :] for (rb, _, p) in tiles]
    x1 = [vt_ref[rb, p, hd:2 * hd, :] for (rb, _, p) in tiles]
    tile_rows = lambda a, i: a[i * hd:(i + 1) * hd]
    step_rows = 8

    def group(t8, s):
        base = t8 * step_rows
        blks = []
        for ref in op_refs:
            per_tile = []
            for (rb, j, p) in tiles:
                start = rb * ROW_BLK + j * tseq + base
                if not isinstance(start, int):
                    start = pl.multiple_of(start, step_rows)
                per_tile.append(ref[pl.ds(start, step_rows), pl.ds(p * LANE, LANE)])
            blks.append(per_tile)
        for u in range(step_rows):
            row = lambda q, i: blks[q][i][u:u + 1, :]
            s_t = [tile_rows(s, i) for i in range(g)]
            sa = _dotx(jnp.concatenate([s_t[i] * row(0, i) for i in range(g)], axis=0), ones_h, 2)
            msk = [lane_t == (j * tseq + base + u) for (_, j, _) in tiles]
            vsel = jnp.concatenate([jnp.where(msk[i], x0[i], 0.0) for i in range(g)]
                                   + [jnp.where(msk[i], x1[i], 0.0) for i in range(g)], axis=0)
            vc = _dotx(vsel, ones_f, 2)
            new = []
            for i in range(g):
                vcol = jnp.where(left, tile_rows(vc, i), tile_rows(vc, g + i))
                new.append(s_t[i] * row(1, i) - tile_rows(sa, i) * row(2, i) + vcol * row(3, i))
            ro = _dot(jnp.concatenate([new[i] * row(4, i) for i in range(g)], axis=0), j2)
            for i, (rb, _, p) in enumerate(tiles):
                r_i = tile_rows(ro, i)
                acc_ref[rb, p, 0:hd, :] = jnp.where(msk[i], r_i[:, :LANE], acc_ref[rb, p, 0:hd, :])
                acc_ref[rb, p, hd:2 * hd, :] = jnp.where(msk[i], r_i[:, LANE:], acc_ref[rb, p, hd:2 * hd, :])
            s = jnp.concatenate(new, axis=0)
        return s

    assert tseq % step_rows == 0
    if tseq == step_rows:
        return group(0, s0)
    return lax.fori_loop(0, tseq // step_rows, group, s0)


def _rwkv_sample_kernel(rw_ref, ovr_ref, s_in_ref, mu_ref, w0_ref, w2p_ref, a0_ref, a2p_ref, g2_ref, kkp_ref, ka_ref,
                        rk_ref, lnw_ref, lnb_ref, o_ref, s_out_ref,
                        kk_s, w_s, bk_s, kr_s, rr_s, vt_s, acc_s, *, tseq):
    w = GROUP_W
    hd = HEAD_DIM
    ones = _head_ones(w)
    nseq = ROW_BLK // tseq
    rw = rw_ref[0]
    first = (_iota((ROW_BLK, 1), 0) & (tseq - 1)) == 0
    prev = jnp.where(first, ovr_ref[0], pltpu.roll(rw, 1, 0))
    r, log_decay, k_rw, v, kk, bk, g, bonus = _rwkv_prologue(
        rw, prev, mu_ref[...], w0_ref[...], w2p_ref[...], a0_ref[...], a2p_ref[...], g2_ref[...],
        kkp_ref[...], ka_ref[...], rk_ref[...], ones)
    kk_s[...] = kk
    w_s[...] = jnp.exp(log_decay)
    bk_s[...] = bk
    kr_s[...] = k_rw
    rr_s[...] = r
    for p in range(2):
        vt_s[0, p] = v[:, p * LANE:(p + 1) * LANE].T
    acc_s[...] = jnp.zeros_like(acc_s)

    ops = (kk_s, w_s, bk_s, kr_s, rr_s)
    per = 2
    for grp in range(nseq // per):
        tiles = [(0, grp * per + jj, p) for jj in range(per) for p in range(2)]
        rows = pl.ds(grp * per * 2 * hd, per * 2 * hd)
        s_out_ref[rows, :] = _rwkv_scan(tiles, tseq, s_in_ref[rows, :], ops, vt_s, acc_s)

    o = jnp.concatenate([acc_s[0, p].T for p in range(2)], axis=1)
    o_ref[0] = _rwkv_group_norm(o, lnw_ref[...], lnb_ref[...], bonus, g, ones)


def _rwkv_sample(rw, ovr, s_in, prm, tseq):
    w = GROUP_W
    nblocks, _, cw = rw.shape
    nseq = ROW_BLK // tseq
    const = lambda i: (0, 0)
    pspecs = [pl.BlockSpec(a.shape, const) for a in prm]
    blk3 = lambda width: pl.BlockSpec((1, ROW_BLK, width), lambda i: (i, 0, 0))
    st_spec = pl.BlockSpec((nseq * 2 * HEAD_DIM, LANE), lambda i: (i, 0))
    scratch = ([pltpu.VMEM((ROW_BLK, w), F32) for _ in range(5)]
               + [pltpu.VMEM((1, 2, ROW_BLK, LANE), F32), pltpu.VMEM((1, 2, ROW_BLK, LANE), F32)])
    return pl.pallas_call(
        functools.partial(_rwkv_sample_kernel, tseq=tseq),
        grid=(nblocks,),
        in_specs=[blk3(cw), blk3(cw), st_spec] + pspecs,
        out_specs=[blk3(w), st_spec],
        out_shape=[jax.ShapeDtypeStruct((nblocks, ROW_BLK, w), F32), jax.ShapeDtypeStruct(s_in.shape, F32)],
        scratch_shapes=scratch,
        compiler_params=_cparams(("parallel",)),
        name="rwkv_sample",
    )(rw, ovr, s_in, *prm)


ROUTE_W = LANE


def _out_router_kernel(x_ref, oab_ref, oc_ref, od_ref, wo_ref, g2_ref, wr_hi_ref, wr_lo_ref, br_ref,
                       x1_ref, h_ref, gate_ref, *, n_experts, n_groups, null_rows, tm):
    w = GROUP_W
    x1 = (x_ref[...]
          + _dot(oab_ref[...], wo_ref[0:2 * w, :])
          + _dot(oc_ref[...], wo_ref[2 * w:3 * w, :])
          + _dot(od_ref[...], wo_ref[3 * w:4 * w, :]))
    if null_rows:
        rowg = pl.program_id(0) * tm + _iota((tm, 1), 0)
        null = rowg < 0
        for start in null_rows:
            null = null | ((rowg >= start) & (rowg < start + PAD_FRONT))
        x1 = jnp.where(null, 0.0, x1)
    x1_ref[...] = x1
    ms = jnp.mean(x1 * x1, axis=-1, keepdims=True)
    h = x1 * lax.rsqrt(ms + NORM_EPS) * g2_ref[...]
    h_hi = h.astype(BF16)
    h_lo = (h - h_hi.astype(F32)).astype(BF16)
    h_ref[...] = h_hi
    lg = (jnp.dot(h_hi, wr_hi_ref[...], preferred_element_type=F32)
          + jnp.dot(h_lo, wr_hi_ref[...], preferred_element_type=F32)
          + jnp.dot(h_hi, wr_lo_ref[...], preferred_element_type=F32)) + br_ref[...]
    lane = _iota(lg.shape, 1)
    big = jnp.int32(1 << 20)
    neg = jnp.float32(-jnp.inf)
    is_g = (lane >= n_experts) & (lane < n_experts + n_groups)
    gl = jnp.where(is_g, lg, neg)
    gmax = jnp.max(gl, axis=-1, keepdims=True)
    gidx = jnp.min(jnp.where(gl == gmax, lane, big), axis=-1, keepdims=True) - n_experts
    g_w = 1.0 / jnp.sum(jnp.where(is_g, jnp.exp(lg - gmax), 0.0), axis=-1, keepdims=True)
    per = n_experts // n_groups
    in_group = (lane >= gidx * per) & (lane < gidx * per + per)
    el = jnp.where(in_group, lg, neg)
    v1 = jnp.max(el, axis=-1, keepdims=True)
    i1 = jnp.min(jnp.where(el == v1, lane, big), axis=-1, keepdims=True)
    el2 = jnp.where(lane == i1, neg, el)
    v2 = jnp.max(el2, axis=-1, keepdims=True)
    i2 = jnp.min(jnp.where(el2 == v2, lane, big), axis=-1, keepdims=True)
    e21 = jnp.exp(v2 - v1)
    p1 = 1.0 / (1.0 + e21)
    p2 = e21 / (1.0 + e21)
    gate_ref[...] = jnp.where(lane == i1, p1 * g_w, 0.0) + jnp.where(lane == i2, p2 * g_w, 0.0)


def _out_router(x, oab, oc, od, wo, g2, wr_hi, wr_lo, br, n_experts, n_groups, null_rows):
    n, d = x.shape
    w = GROUP_W
    tm = 256 if n % 256 == 0 else ROW_BLK
    row = lambda i: (i, 0)
    const = lambda i: (0, 0)
    return pl.pallas_call(
        functools.partial(_out_router_kernel, n_experts=n_experts, n_groups=n_groups, null_rows=null_rows, tm=tm),
        grid=(n // tm,),
        in_specs=[pl.BlockSpec((tm, d), row), pl.BlockSpec((tm, 2 * w), row), pl.BlockSpec((tm, w), row),
                  pl.BlockSpec((tm, w), row), pl.BlockSpec(wo.shape, const), pl.BlockSpec((1, d), const),
                  pl.BlockSpec(wr_hi.shape, const), pl.BlockSpec(wr_lo.shape, const), pl.BlockSpec((1, ROUTE_W), const)],
        out_specs=[pl.BlockSpec((tm, d), row), pl.BlockSpec((tm, d), row), pl.BlockSpec((tm, ROUTE_W), row)],
        out_shape=[jax.ShapeDtypeStruct((n, d), F32), jax.ShapeDtypeStruct((n, d), BF16),
                   jax.ShapeDtypeStruct((n, ROUTE_W), F32)],
        compiler_params=_cparams(("parallel",)),
        name="out_router",
    )(x, oab, oc, od, wo, g2, wr_hi, wr_lo, br)


def _moe_kernel(x1_ref, h_ref, gate_ref, w1_ref, w3_ref, w2_ref, y_ref):
    e = pl.program_id(1)

    @pl.when(e == 0)
    def _():
        y_ref[...] = x1_ref[...]

    h = h_ref[...]
    gates = gate_ref[...]
    ge = jnp.sum(jnp.where(_iota(gates.shape, 1) == e, gates, 0.0), axis=-1, keepdims=True)
    up = jnp.dot(h, w1_ref[...].astype(BF16), preferred_element_type=F32)
    lin = jnp.dot(h, w3_ref[...].astype(BF16), preferred_element_type=F32)
    hid = (up * _sigmoid(up)) * lin * ge
    y_ref[...] += jnp.dot(hid.astype(BF16), w2_ref[...].astype(BF16), preferred_element_type=F32)


def _moe_tile(n):
    for t in (1536, 1408, 1280, 1024, 640, 512, 256, 128):
        if n % t == 0:
            return t
    raise ValueError(f"row count {n} is not a multiple of {ROW_BLK}")


def _moe(x1, h, gates, w1, w3, w2, layer):
    n, d = x1.shape
    n_experts, _, f = w1.shape[1:]
    tm = _moe_tile(n)
    row = lambda i, e: (i, 0)
    return pl.pallas_call(
        _moe_kernel,
        grid=(n // tm, n_experts),
        in_specs=[pl.BlockSpec((tm, d), row), pl.BlockSpec((tm, d), row), pl.BlockSpec((tm, ROUTE_W), row),
                  pl.BlockSpec((None, None, d, f), lambda i, e: (layer, e, 0, 0)),
                  pl.BlockSpec((None, None, d, f), lambda i, e: (layer, e, 0, 0)),
                  pl.BlockSpec((None, None, f, d), lambda i, e: (layer, e, 0, 0))],
        out_specs=pl.BlockSpec((tm, d), row),
        out_shape=jax.ShapeDtypeStruct((n, d), F32),
        compiler_params=_cparams(("parallel", "arbitrary")),
        name="moe",
    )(x1, h, gates, w1, w3, w2)


def _rope_tables(pos):
    half = HEAD_DIM // 2
    inv = ROPE_BASE ** (-jnp.arange(half, dtype=F32) / half)
    ang = pos.astype(F32)[:, None] * jnp.tile(inv, LANE // half)[None, :]
    return jnp.cos(ang), jnp.sin(ang)


def _retention_log_gamma():
    lg = jnp.log1p(-jnp.exp2(-5.0 - jnp.arange(N_HG, dtype=F32)))
    return jnp.broadcast_to(jnp.repeat(lg, HEAD_DIM)[None, :], (ROW_BLK, GROUP_W))


def _row(v):
    return v.reshape(1, -1).astype(F32)


def _rwkv_params(p):
    w2 = p['rwkv_w2']
    a2 = p['rwkv_a2']
    w2p = jnp.concatenate([w2, jnp.zeros_like(a2)], axis=0)
    a2p = jnp.concatenate([jnp.zeros_like(w2), a2], axis=0)
    return (_row(p['rwkv_mu']), _row(p['rwkv_w0']), w2p, _row(p['rwkv_a0']), a2p, p['rwkv_g2'],
            _row(p['rwkv_kk']), _row(p['rwkv_ka']), _row(p['rwkv_rk']), _row(p['rwkv_ln_w']), _row(p['rwkv_ln_b']))


def _rwkv_state_to_tiles(s):
    n = s.shape[0]
    return s.reshape(n, 2, 2, HEAD_DIM, HEAD_DIM).transpose(0, 1, 3, 2, 4).reshape(n * 2 * HEAD_DIM, LANE)


def _rwkv_tiles_to_state(t, n):
    return t.reshape(n, 2, HEAD_DIM, 2, HEAD_DIM).transpose(0, 1, 3, 2, 4).reshape(n, N_HG, HEAD_DIM, HEAD_DIM)


def _diag_heads(s):
    return jnp.stack([s[:, h * HEAD_DIM:(h + 1) * HEAD_DIM, h * HEAD_DIM:(h + 1) * HEAD_DIM] for h in range(N_HG)], axis=1)


def kernel(x_prompt, x_sample, cache_sb_k, cache_sb_v, state_ret, state_hgrn, state_rwkv, state_rwkv_shift,
           page_table, meta_tokens, norm1, norm2, w_in, w_out, ret_norm, hgrn_lb_logits, hgrn_norm,
           sb_q_norm, sb_k_norm, sb_bias, rwkv_mu, rwkv_w0, rwkv_w2, rwkv_a0, rwkv_a2, rwkv_g2, rwkv_kk, rwkv_ka,
           rwkv_rk, rwkv_ln_w, rwkv_ln_b, moe_w_group, moe_b_group, moe_w_expert, moe_b_expert,
           moe_w1, moe_w3, moe_w2):
    bp, seq, dm = x_prompt.shape
    db, ds, _ = x_sample.shape
    depth = w_in.shape[0]
    w = GROUP_W
    assert seq % ROW_BLK == 0 and (db * ds) % ROW_BLK == 0 and ROW_BLK % ds == 0 and ds & (ds - 1) == 0
    assert w_in.shape[2] == LIN_COLS + 4 * w and cache_sb_k.shape[3] * cache_sb_k.shape[4] == w
    tp = seq + SB_QBLK
    nblk = tp // ROW_BLK
    tlen = seq + N_META
    past = page_table.shape[1] * cache_sb_k.shape[2]
    n_groups, e_per = moe_w_expert.shape[2:]
    n_experts = n_groups * e_per
    assert n_experts + n_groups <= ROUTE_W

    xp = jnp.concatenate([jnp.zeros((bp, PAD_FRONT, dm), F32),
                          jnp.broadcast_to(meta_tokens[None], (bp, N_META, dm)).astype(F32), x_prompt], axis=1)
    xp = xp.reshape(bp * tp, dm)
    xs = x_sample.reshape(db * ds, dm)
    null_rows = tuple(b * tp for b in range(bp))

    cos_p, sin_p = _rope_tables(jnp.maximum(jnp.arange(tp) - PAD_FRONT, 0))
    cos_s, sin_s = _rope_tables(jnp.tile(past + jnp.arange(ds), ROW_BLK // ds))
    lgam = _retention_log_gamma()
    cache_k = cache_sb_k.reshape(cache_sb_k.shape[:3] + (w,))
    cache_v = cache_sb_v.reshape(cache_sb_v.shape[:3] + (w,))
    tile_heads = lambda v: jnp.tile(v, N_HG)[None, :].astype(F32)

    prompt_rows, sample_rows = [], []
    for l in range(depth):
        wa = w_in[l][:, :LIN_COLS].astype(BF16)
        wb = w_in[l][:, LIN_COLS:].astype(BF16)
        g1 = _row(norm1[l])
        lin_args = (lgam, _row(ret_norm[l]), hgrn_lb_logits.astype(F32), _row(hgrn_norm[l]),
                    tile_heads(sb_q_norm[l]), tile_heads(sb_k_norm[l]))
        rwkv_prm = _rwkv_params(dict(rwkv_mu=rwkv_mu[l], rwkv_w0=rwkv_w0[l], rwkv_w2=rwkv_w2[l], rwkv_a0=rwkv_a0[l],
                                     rwkv_a2=rwkv_a2[l], rwkv_g2=rwkv_g2[l], rwkv_kk=rwkv_kk[l], rwkv_ka=rwkv_ka[l],
                                     rwkv_rk=rwkv_rk[l], rwkv_ln_w=rwkv_ln_w[l], rwkv_ln_b=rwkv_ln_b[l]))
        wo = w_out[l].astype(BF16)
        wr = jnp.zeros((dm, ROUTE_W), F32)
        wr = wr.at[:, :n_experts].set(moe_w_expert[l].reshape(dm, n_experts))
        wr = wr.at[:, n_experts:n_experts + n_groups].set(moe_w_group[l])
        wr_hi = wr.astype(BF16)
        wr_lo = (wr - wr_hi.astype(F32)).astype(BF16)
        br = jnp.zeros((1, ROUTE_W), F32)
        br = br.at[0, :n_experts].set(moe_b_expert[l].reshape(n_experts))
        br = br.at[0, n_experts:n_experts + n_groups].set(moe_b_group[l])
        router = (wo, _row(norm2[l]), wr_hi, wr_lo, br)

        pa, pb = _in_proj(xp, g1, wa, wb)
        oab, qs, knf, knb, vb, s_ret, s_hg = _lin_mix_prompt(pa, cos_p, sin_p, *lin_args, l, bp, nblk)
        oc = _sb_prompt(sb_bias[l].astype(F32), qs, knb, vb, bp, tp)
        od, s_rw = _rwkv_prompt(pb, rwkv_prm, bp, nblk)
        x1, h2, gates = _out_router(xp, oab, oc, od, *router, n_experts, n_groups, null_rows)
        xp = _moe(x1, h2, gates, moe_w1, moe_w3, moe_w2, l)
        real = lambda a: a.reshape(bp, tp, -1)[:, PAD_FRONT:]
        prompt_rows.append((real(knf).reshape(bp, tlen, N_HG, HEAD_DIM),
                            real(pa[:, LIN_COLS - w:]).reshape(bp, tlen, N_HG, HEAD_DIM),
                            _diag_heads(s_ret), _diag_heads(s_hg), _diag_heads(s_rw).swapaxes(-1, -2),
                            pb.reshape(bp, tp, -1)[:, -1]))

        sa, sb = _in_proj(xs, g1, wa, wb)
        stack = lambda s: s.reshape(db, w, HEAD_DIM)
        oab, qs, knf, vf, s_ret, s_hg = _lin_mix_sample(sa, cos_s, sin_s, *lin_args, stack(state_ret[l]),
                                                        stack(state_hgrn[l]), l, ds)
        oc = _sb_sample(page_table, sb_bias[l].astype(F32), qs, knf, vf, cache_k, cache_v, l, ds)
        cw = sb.shape[-1]
        ovr = jnp.concatenate([state_rwkv_shift[l][:, None, :], jnp.zeros((db, ds - 1, cw), F32)], axis=1)
        od, s_rw = _rwkv_sample(sb.reshape(-1, ROW_BLK, cw), ovr.reshape(-1, ROW_BLK, cw),
                                _rwkv_state_to_tiles(state_rwkv[l]), rwkv_prm, ds)
        x1, h2, gates = _out_router(xs, oab, oc, od.reshape(db * ds, w), *router, n_experts, n_groups, ())
        xs = _moe(x1, h2, gates, moe_w1, moe_w3, moe_w2, l)
        sample_rows.append((knf.reshape(db, ds, N_HG, HEAD_DIM), vf.reshape(db, ds, N_HG, HEAD_DIM),
                            s_ret.reshape(db, N_HG, HEAD_DIM, HEAD_DIM), s_hg.reshape(db, N_HG, HEAD_DIM, HEAD_DIM),
                            _rwkv_tiles_to_state(s_rw, db), sb.reshape(db, ds, cw)[:, -1]))

    y_prompt = xp.reshape(bp, tp, dm)[:, SB_QBLK:]
    y_sample = xs.reshape(db, ds, dm)
    stacked_p = [jnp.stack(r) for r in zip(*prompt_rows)]
    stacked_s = [jnp.stack(r) for r in zip(*sample_rows)]
    return (y_prompt, y_sample, *stacked_p, *stacked_s)
```

```python
import functools
import math

import numpy as np
import jax
import jax.numpy as jnp
from jax import lax
from jax.experimental import pallas as pl
from jax.experimental.pallas import tpu as pltpu

F32 = jnp.float32
BF16 = jnp.bfloat16

HEAD_DIM = 64
N_HG = 4
GROUP_W = N_HG * HEAD_DIM
N_META = 16
LANE = 128
ROW_BLK = 128
SB_QBLK = 256
PAD_FRONT = SB_QBLK - N_META
LIN_COLS = 11 * GROUP_W
NORM_EPS = 1e-6
RWKV_LN_EPS = 64e-5
ROPE_BASE = 10000.0
LOG2E = math.log2(math.e)
SB_SCALE = HEAD_DIM ** -0.5 * LOG2E
E_PER_GROUP = 8
VMEM_LIMIT = 56 * 1024 * 1024


def _iota(shape, dim):
    return lax.broadcasted_iota(jnp.int32, shape, dim)


def _mask01(cond):
    return jnp.where(cond, 1.0, 0.0).astype(BF16)


def _dot(a, b):
    return jnp.dot(a.astype(BF16), b.astype(BF16), preferred_element_type=F32)


def _dot_nt(a, b):
    return lax.dot_general(a.astype(BF16), b.astype(BF16), (((1,), (1,)), ((), ())),
                           preferred_element_type=F32)


def _dot_tn(a, b):
    return lax.dot_general(a.astype(BF16), b.astype(BF16), (((0,), (0,)), ((), ())),
                           preferred_element_type=F32)


def _split(x, n):
    parts = []
    r = x
    for _ in range(n):
        h = r.astype(BF16)
        parts.append(h)
        r = r - h.astype(F32)
    return parts


def _dotx(x, m, n=2):
    out = None
    for p in _split(x, n):
        t = jnp.dot(p, m, preferred_element_type=F32)
        out = t if out is None else out + t
    return out


def _xdot(m, x, n=3):
    out = None
    for p in _split(x, n):
        t = jnp.dot(m, p, preferred_element_type=F32)
        out = t if out is None else out + t
    return out


def _head_ones(w):
    return _mask01((_iota((w, w), 0) >> 6) == (_iota((w, w), 1) >> 6))


def _sigmoid(x):
    return 1.0 / (1.0 + jnp.exp(-x))


def _softplus(x):
    return jnp.maximum(x, 0.0) + jnp.log(1.0 + jnp.exp(-jnp.abs(x)))


def _softplus2(x):
    return jnp.maximum(x, 0.0) + jnp.log2(1.0 + jnp.exp2(-jnp.abs(x)))


def _head_rms(x, gain, ones):
    ms = _dotx(x * x, ones, 2) * (1.0 / HEAD_DIM)
    return x * lax.rsqrt(ms + NORM_EPS) * gain


def _row_tile(n):
    for t in (512, 256, 128):
        if n % t == 0:
            return t
    raise ValueError(f"row count {n} is not a multiple of {ROW_BLK}")


def _cparams(sem, flags=None):
    return pltpu.CompilerParams(dimension_semantics=sem, vmem_limit_bytes=VMEM_LIMIT, flags=flags)


def _in_proj_kernel(x_ref, g_ref, wa_ref, wb_ref, oa_ref, ob_ref):
    x = x_ref[...]
    ms = jnp.mean(x * x, axis=-1, keepdims=True)
    h = (x * lax.rsqrt(ms + NORM_EPS) * g_ref[...]).astype(BF16)
    oa_ref[...] = jnp.dot(h, wa_ref[...], preferred_element_type=F32)
    ob_ref[...] = jnp.dot(h, wb_ref[...], preferred_element_type=F32)


def _in_proj(x, g, wa, wb):
    n, d = x.shape
    tm = 256 if n % 256 == 0 else ROW_BLK
    return pl.pallas_call(
        _in_proj_kernel,
        grid=(n // tm,),
        in_specs=[pl.BlockSpec((tm, d), lambda i: (i, 0)),
                  pl.BlockSpec((1, d), lambda i: (0, 0)),
                  pl.BlockSpec(wa.shape, lambda i: (0, 0)),
                  pl.BlockSpec(wb.shape, lambda i: (0, 0))],
        out_specs=[pl.BlockSpec((tm, wa.shape[1]), lambda i: (i, 0)),
                   pl.BlockSpec((tm, wb.shape[1]), lambda i: (i, 0))],
        out_shape=[jax.ShapeDtypeStruct((n, wa.shape[1]), F32),
                   jax.ShapeDtypeStruct((n, wb.shape[1]), F32)],
        compiler_params=_cparams(("parallel",)),
        name="in_proj",
    )(x, g, wa, wb)


def _gla_block(q, k, v, lw, states, sub, chain):
    rows, w = q.shape
    nsub = rows // sub
    shift = int(math.log2(sub))
    r_i = _iota((rows, rows), 0)
    c_i = _iota((rows, rows), 1)
    same = (r_i >> shift) == (c_i >> shift)
    tri = _mask01(same & (c_i <= r_i))
    blk = _mask01(same)
    sub_of_row = _mask01((_iota((rows, LANE), 0) >> shift) == _iota((rows, LANE), 1))
    bc = bl = dsum = None
    for p in _split(lw, 3):
        t1 = jnp.dot(tri, p, preferred_element_type=F32)
        t2 = jnp.dot(blk, p, preferred_element_type=F32)
        t3 = lax.dot_general(p, sub_of_row, (((0,), (0,)), ((), ())), preferred_element_type=F32)
        bc = t1 if bc is None else bc + t1
        bl = t2 if bl is None else bl + t2
        dsum = t3 if dsum is None else dsum + t3
    dcol = jnp.exp(dsum)
    qt = q * jnp.exp(bc)
    kh = k * jnp.exp(bl - bc)
    ones = _head_ones(w)

    local = _iota((rows, w), 0) & (sub - 1)
    o_intra = jnp.zeros((rows, w), F32)
    for dist in range(sub):
        back = (lambda x: x) if dist == 0 else (lambda x: pltpu.roll(x, dist, 0))
        e = jnp.exp(jnp.where(local >= dist, bc - back(bc), -1e30))
        p = q * back(k) * e
        o_intra = o_intra + _dot(p, ones) * back(v)

    bdmask = (_iota((w, w), 0) >> 6) == (_iota((w, w), 1) >> 6)
    outs = []
    new_states = []
    s = states if chain else None
    for i in range(nsub):
        sl = slice(i * sub, (i + 1) * sub)
        if not chain:
            s = states[i]
        outs.append(_dot(qt[sl], s))
        kv = _dot_tn(kh[sl], v[sl])
        s = s * dcol[:, i:i + 1] + jnp.where(bdmask, kv, 0.0)
        if not chain:
            new_states.append(s)
    o = o_intra + jnp.concatenate(outs, axis=0)
    return o, (s if chain else new_states)


def _retention_block(q, k, v, lgam, s):
    rows, w = q.shape
    width = N_HG * rows
    t_row = _iota((rows, w), 0).astype(F32)
    qd = q * jnp.exp((t_row + 1.0) * lgam)
    kd = k * jnp.exp((rows - 1.0 - t_row) * lgam)
    left = _iota((rows, LANE), 1) < HEAD_DIM

    def split_heads(x):
        x = x.astype(BF16)
        zero = jnp.zeros_like(x)
        return jnp.concatenate([jnp.where(left, x, zero), jnp.where(left, zero, x)], axis=0)

    dist = (_iota((rows, width), 0) - (_iota((rows, width), 1) & (rows - 1))).astype(F32)
    lg_heads = jnp.concatenate([jnp.broadcast_to(lgam[:, h * HEAD_DIM:h * HEAD_DIM + 1], (rows, rows))
                                for h in range(N_HG)], axis=1)
    dmat = jnp.where(dist >= 0.0, jnp.exp(jnp.maximum(dist, 0.0) * lg_heads), 0.0)
    scores = jnp.concatenate([_dot_nt(q[:, p * LANE:(p + 1) * LANE], split_heads(k[:, p * LANE:(p + 1) * LANE]))
                              for p in range(2)], axis=1) * dmat
    o_intra = jnp.concatenate([_dot(scores[:, p * 2 * LANE:(p + 1) * 2 * LANE], split_heads(v[:, p * LANE:(p + 1) * LANE]))
                               for p in range(2)], axis=1)
    o = o_intra + _dot(qd, s)
    ones_rl = jnp.ones((rows, LANE), BF16)
    dsum = None
    for part in _split(lgam, 3):
        t3 = lax.dot_general(part, ones_rl, (((0,), (0,)), ((), ())), preferred_element_type=F32)
        dsum = t3 if dsum is None else dsum + t3
    bdmask = (_iota((w, w), 0) >> 6) == (_iota((w, w), 1) >> 6)
    s = s * jnp.exp(dsum)[:, 0:1] + jnp.where(bdmask, _dot_tn(kd, v), 0.0)
    return o, s


def _lin_mix_math(p, cos, sin, lgam, retw, lb_logits, hgw, qnw, knw, layer, st_ret, st_hg, sub, chain):
    w = GROUP_W
    qa, ka, va, ga, qb, fb, ib, gb, qc, kc, vc = [p[:, i * w:(i + 1) * w] for i in range(11)]
    rows = p.shape[0]
    ones = _head_ones(w)
    lane = _iota((rows, LANE), 1)
    first_half = (lane & (HEAD_DIM - 1)) < (HEAD_DIM // 2)

    def rope(x):
        halves = []
        for hp in range(w // LANE):
            xh = x[:, hp * LANE:(hp + 1) * LANE]
            rot = jnp.where(first_half, -pltpu.roll(xh, LANE - HEAD_DIM // 2, 1), pltpu.roll(xh, HEAD_DIM // 2, 1))
            halves.append(xh * cos + rot * sin)
        return jnp.concatenate(halves, axis=1)

    q_ret = rope(qa)
    k_ret = rope(ka) * (HEAD_DIM ** -0.5)
    if chain:
        o_ret, st_ret = _retention_block(q_ret, k_ret, va, lgam, st_ret)
    else:
        o_ret, st_ret = _gla_block(q_ret, k_ret, va, lgam, st_ret, sub, chain)
    o_a = _head_rms(o_ret, retw, ones) * (ga * _sigmoid(ga))

    lg = [lb_logits[d:d + 1, :] for d in range(lb_logits.shape[0])]
    mx = functools.reduce(jnp.maximum, lg)
    ex = [jnp.exp(row - mx) for row in lg]
    lb = sum(ex[1:layer + 1], jnp.zeros_like(mx)) / sum(ex[1:], ex[0])
    log_sig = jnp.minimum(fb, 0.0) - jnp.log1p(jnp.exp(-jnp.abs(fb)))
    t_a = jnp.broadcast_to(jnp.log(lb), fb.shape)
    t_b = jnp.log1p(-lb) + log_sig
    logf = jnp.maximum(t_a, t_b) + jnp.log1p(jnp.exp(-jnp.abs(t_a - t_b)))
    k_hg = (1.0 - lb) * _sigmoid(-fb)
    o_hg, st_hg = _gla_block(qb, k_hg, ib, logf, st_hg, sub, chain)
    o_b = _head_rms(o_hg, hgw, ones) * (gb * _sigmoid(gb))

    qn = _head_rms(qc, qnw, ones)
    kn = _head_rms(kc, knw, ones)
    return o_a, o_b, qn, kn, vc, st_ret, st_hg


def _lin_mix_prompt_kernel(p_ref, cos_ref, sin_ref, lgam_ref, retw_ref, lbl_ref, hgw_ref, qnw_ref, knw_ref,
                           oab_ref, qs_ref, knf_ref, knb_ref, vb_ref, sret_ref, shg_ref,
                           st_ret, st_hg, *, layer, sub):
    i = pl.program_id(1)

    @pl.when(i == 0)
    def _():
        st_ret[...] = jnp.zeros_like(st_ret)
        st_hg[...] = jnp.zeros_like(st_hg)

    o_a, o_b, qn, kn, vc, s1, s2 = _lin_mix_math(
        p_ref[...], cos_ref[...], sin_ref[...], lgam_ref[...], retw_ref[...], lbl_ref, hgw_ref[...],
        qnw_ref[...], knw_ref[...], layer, st_ret[...], st_hg[...], sub, True)
    st_ret[...] = s1
    st_hg[...] = s2
    oab_ref[...] = jnp.concatenate([o_a, o_b], axis=1)
    qs_ref[...] = (qn * SB_SCALE).astype(BF16)
    knf_ref[...] = kn
    knb_ref[...] = kn.astype(BF16)
    vb_ref[...] = vc.astype(BF16)

    @pl.when(i == pl.num_programs(1) - 1)
    def _():
        sret_ref[...] = s1
        shg_ref[...] = s2


def _lin_mix_prompt(proj, cos, sin, lgam, retw, lbl, hgw, qnw, knw, layer, nb, nblk):
    n = proj.shape[0]
    w = GROUP_W
    row = lambda b, i: (b * nblk + i, 0)
    const = lambda b, i: (0, 0)
    outs = pl.pallas_call(
        functools.partial(_lin_mix_prompt_kernel, layer=layer, sub=16),
        grid=(nb, nblk),
        in_specs=[pl.BlockSpec((ROW_BLK, LIN_COLS), row),
                  pl.BlockSpec((ROW_BLK, LANE), lambda b, i: (i, 0)),
                  pl.BlockSpec((ROW_BLK, LANE), lambda b, i: (i, 0)),
                  pl.BlockSpec((ROW_BLK, w), const), pl.BlockSpec((1, w), const),
                  pl.BlockSpec(lbl.shape, const), pl.BlockSpec((1, w), const),
                  pl.BlockSpec((1, w), const), pl.BlockSpec((1, w), const)],
        out_specs=[pl.BlockSpec((ROW_BLK, 2 * w), row),
                   pl.BlockSpec((ROW_BLK, w), row), pl.BlockSpec((ROW_BLK, w), row),
                   pl.BlockSpec((ROW_BLK, w), row), pl.BlockSpec((ROW_BLK, w), row),
                   pl.BlockSpec((None, w, w), lambda b, i: (b, 0, 0)),
                   pl.BlockSpec((None, w, w), lambda b, i: (b, 0, 0))],
        out_shape=[jax.ShapeDtypeStruct((n, 2 * w), F32),
                   jax.ShapeDtypeStruct((n, w), BF16), jax.ShapeDtypeStruct((n, w), F32),
                   jax.ShapeDtypeStruct((n, w), BF16), jax.ShapeDtypeStruct((n, w), BF16),
                   jax.ShapeDtypeStruct((nb, w, w), F32), jax.ShapeDtypeStruct((nb, w, w), F32)],
        scratch_shapes=[pltpu.VMEM((w, w), F32), pltpu.VMEM((w, w), F32)],
        compiler_params=_cparams(("parallel", "arbitrary")),
        name="lin_mix_prompt",
    )(proj, cos, sin, lgam, retw, lbl, hgw, qnw, knw)
    return outs


def _lin_mix_sample_kernel(p_ref, cos_ref, sin_ref, lgam_ref, retw_ref, lbl_ref, hgw_ref, qnw_ref, knw_ref,
                           sret_in, shg_in,
                           oab_ref, qs_ref, knf_ref, vf_ref, sret_out, shg_out, *, layer, sub):
    w = GROUP_W
    nseq = ROW_BLK // sub
    bdmask = (_iota((w, w), 0) >> 6) == (_iota((w, w), 1) >> 6)
    rep = _mask01(_iota((HEAD_DIM, w), 0) == (_iota((HEAD_DIM, w), 1) & (HEAD_DIM - 1)))
    rep_t = _mask01((_iota((w, HEAD_DIM), 0) & (HEAD_DIM - 1)) == _iota((w, HEAD_DIM), 1))

    def expand(ref):
        return [jnp.where(bdmask, _dotx(ref[j], rep, 3), 0.0) for j in range(nseq)]

    def extract(ref, states):
        for j in range(nseq):
            ref[j] = _dotx(states[j], rep_t, 3)

    o_a, o_b, qn, kn, vc, s1, s2 = _lin_mix_math(
        p_ref[...], cos_ref[...], sin_ref[...], lgam_ref[...], retw_ref[...], lbl_ref, hgw_ref[...],
        qnw_ref[...], knw_ref[...], layer, expand(sret_in), expand(shg_in), sub, False)
    extract(sret_out, s1)
    extract(shg_out, s2)
    oab_ref[...] = jnp.concatenate([o_a, o_b], axis=1)
    qs_ref[...] = qn * SB_SCALE
    knf_ref[...] = kn
    vf_ref[...] = vc


def _lin_mix_sample(proj, cos, sin, lgam, retw, lbl, hgw, qnw, knw, s_ret, s_hg, layer, dec_seq):
    n = proj.shape[0]
    w = GROUP_W
    nseq = ROW_BLK // dec_seq
    row = lambda i: (i, 0)
    const = lambda i: (0, 0)
    st_spec = pl.BlockSpec((nseq, w, HEAD_DIM), lambda i: (i, 0, 0))
    return pl.pallas_call(
        functools.partial(_lin_mix_sample_kernel, layer=layer, sub=dec_seq),
        grid=(n // ROW_BLK,),
        in_specs=[pl.BlockSpec((ROW_BLK, LIN_COLS), row),
                  pl.BlockSpec((ROW_BLK, LANE), const), pl.BlockSpec((ROW_BLK, LANE), const),
                  pl.BlockSpec((ROW_BLK, w), const), pl.BlockSpec((1, w), const),
                  pl.BlockSpec(lbl.shape, const), pl.BlockSpec((1, w), const),
                  pl.BlockSpec((1, w), const), pl.BlockSpec((1, w), const),
                  st_spec, st_spec],
        out_specs=[pl.BlockSpec((ROW_BLK, 2 * w), row),
                   pl.BlockSpec((ROW_BLK, w), row), pl.BlockSpec((ROW_BLK, w), row),
                   pl.BlockSpec((ROW_BLK, w), row), st_spec, st_spec],
        out_shape=[jax.ShapeDtypeStruct((n, 2 * w), F32),
                   jax.ShapeDtypeStruct((n, w), F32), jax.ShapeDtypeStruct((n, w), F32),
                   jax.ShapeDtypeStruct((n, w), F32),
                   jax.ShapeDtypeStruct(s_ret.shape, F32), jax.ShapeDtypeStruct(s_hg.shape, F32)],
        compiler_params=_cparams(("parallel",)),
        name="lin_mix_sample",
    )(proj, cos, sin, lgam, retw, lbl, hgw, qnw, knw, s_ret, s_hg)


def _sb_consts(rows, tmask):
    j_i = _iota((LANE, 2 * LANE), 0)
    s_i = _iota((LANE, 2 * LANE), 1)
    ucat = _mask01((s_i >= LANE) | (j_i >= s_i))
    causal = _iota((rows, LANE), 1) < (_iota((rows, LANE), 0) & tmask)
    return ucat, causal


def _sb_block(z, carry, ucat, causal):
    sp = _softplus2(z)
    if causal is not None:
        sp = jnp.where(causal, sp, 0.0)
    t = _dotx(sp, ucat, 2)
    a = jnp.exp2(z - (carry + t[:, :LANE]))
    if causal is not None:
        a = jnp.where(causal, a, 0.0)
    return a, carry + t[:, LANE:]


def _sb_prompt_kernel(bias_ref, q_ref, k_ref, v_ref, o_ref):
    i = pl.program_id(1)
    qb = SB_QBLK
    kpq = qb // ROW_BLK
    q = q_ref[...]
    left = _iota((ROW_BLK, LANE), 1) < HEAD_DIM
    zero = jnp.zeros((ROW_BLK, LANE), BF16)
    width = N_HG * ROW_BLK
    lane = _iota((qb, width), 1)
    bias_row = jnp.zeros((qb, width), F32)
    for h in range(N_HG):
        bias_row = jnp.where((lane >> 7) == h, bias_ref[h] * LOG2E, bias_row)
    sk = lane & (ROW_BLK - 1)
    tq = _iota((qb, width), 0)
    r2 = _iota((2 * LANE, 2 * LANE), 0)
    c2 = _iota((2 * LANE, 2 * LANE), 1)
    ubd = _mask01(((r2 >> 7) == (c2 >> 7)) & ((r2 & (LANE - 1)) > (c2 & (LANE - 1))))

    def split_heads(x):
        return jnp.concatenate([jnp.where(left, x, zero), jnp.where(left, zero, x)], axis=0)

    pair = lambda x, p: x[:, p * 2 * LANE:(p + 1) * 2 * LANE]

    def steps(kbs, masks, carry, acc0, acc1):
        blocks = []
        for kb in kbs:
            start = pl.multiple_of(kb * ROW_BLK, ROW_BLK)
            blocks.append((k_ref[pl.ds(start, ROW_BLK), :], v_ref[pl.ds(start, ROW_BLK), :]))
        zs = [jnp.concatenate([_dot_nt(q[:, p * LANE:(p + 1) * LANE], split_heads(kblk[:, p * LANE:(p + 1) * LANE]))
                               for p in range(2)], axis=1) + bias_row for kblk, _ in blocks]
        sp_own = [_softplus2(z) for z in zs]
        sps = [s if m is None else jnp.where(m, s, 0.0) for s, m in zip(sp_own, masks)]
        ts = []
        for sp in sps:
            spb = sp.astype(BF16)
            ts.append(jnp.concatenate([jnp.dot(pair(spb, p), ubd, preferred_element_type=F32) for p in range(2)], axis=1))
        abs_ = []
        for z, own, sp, t, m in zip(zs, sp_own, sps, ts, masks):
            a = jnp.exp2((z - own) - (carry + t))
            if m is not None:
                a = jnp.where(m, a, 0.0)
            abs_.append(a.astype(BF16))
            carry = carry + jnp.concatenate(
                [jnp.broadcast_to(jnp.sum(sp[:, h * ROW_BLK:(h + 1) * ROW_BLK], axis=-1, keepdims=True), (qb, ROW_BLK))
                 for h in range(N_HG)], axis=1)
        accs = [acc0, acc1]
        for ab, (_, vblk) in zip(abs_, blocks):
            for p in range(2):
                accs[p] = accs[p] + jnp.dot(pair(ab, p), split_heads(vblk[:, p * LANE:(p + 1) * LANE]),
                                            preferred_element_type=F32)
        return carry, accs[0], accs[1]

    zeros = jnp.zeros((qb, LANE), F32)
    state = (jnp.zeros((qb, width), F32), zeros, zeros)
    state = steps([i * kpq + d for d in reversed(range(kpq))], [sk + d * ROW_BLK < tq for d in reversed(range(kpq))],
                  *state)

    def earlier_blocks(j, c):
        return steps([(i - j) * kpq - 1 - d for d in range(kpq)], [None] * kpq, *c)

    state = lax.fori_loop(0, i, earlier_blocks, state)
    o_ref[...] = jnp.concatenate([state[1], state[2]], axis=1)


def _sb_prompt(bias, qs, kb, vb, nb, tp):
    n, w = qs.shape
    nq = tp // SB_QBLK
    return pl.pallas_call(
        _sb_prompt_kernel,
        grid=(nb, nq),
        in_specs=[pl.BlockSpec(memory_space=pltpu.SMEM),
                  pl.BlockSpec((SB_QBLK, w), lambda b, i: (b * nq + i, 0)),
                  pl.BlockSpec((tp, w), lambda b, i: (b, 0)),
                  pl.BlockSpec((tp, w), lambda b, i: (b, 0))],
        out_specs=pl.BlockSpec((SB_QBLK, w), lambda b, i: (b * nq + i, 0)),
        out_shape=jax.ShapeDtypeStruct((n, w), F32),
        compiler_params=_cparams(("parallel", "arbitrary")),
        name="sb_prompt",
    )(bias, qs, kb, vb)


def _sb_sample_kernel(pt_ref, bias_ref, q_ref, kn_ref, vn_ref, *rest, n_pages, dec_seq):
    del pt_ref
    k_pages = rest[:n_pages]
    v_pages = rest[n_pages:2 * n_pages]
    o_ref = rest[2 * n_pages]
    w = GROUP_W
    rows = N_HG * dec_seq
    q = q_ref[...]
    head_of_lane = _iota((dec_seq, w), 1) >> 6
    qbd = jnp.concatenate([jnp.where(head_of_lane == h, q, 0.0) for h in range(N_HG)], axis=0).astype(BF16)
    bias_col = jnp.concatenate([jnp.full((dec_seq, 1), bias_ref[h] * LOG2E, F32) for h in range(N_HG)], axis=0)
    ucat, causal = _sb_consts(rows, dec_seq - 1)
    pad = jnp.zeros((LANE - dec_seq, w), F32)
    k_new = jnp.concatenate([kn_ref[...], pad], axis=0)
    v_new = jnp.concatenate([vn_ref[...], pad], axis=0)
    a, carry = _sb_block(_dot_nt(qbd, k_new) + bias_col, jnp.zeros((rows, LANE), F32), ucat, causal)
    acc = _dot(a, v_new)
    for p in reversed(range(n_pages)):
        a, carry = _sb_block(_dot_nt(qbd, k_pages[p][...]) + bias_col, carry, ucat, None)
        acc = acc + _dot(a, v_pages[p][...])
    out = jnp.zeros((dec_seq, w), F32)
    for h in range(N_HG):
        out = out + jnp.where(head_of_lane == h, acc[h * dec_seq:(h + 1) * dec_seq], 0.0)
    o_ref[...] = out


def _sb_sample(page_table, bias, qs, kn, vn, cache_k, cache_v, layer, dec_seq):
    n, w = qs.shape
    db, n_pages = page_table.shape
    page = cache_k.shape[2]
    assert page == LANE and dec_seq % 8 == 0
    row = lambda b, pt: (b, 0)
    page_specs = [pl.BlockSpec((None, None, page, w), lambda b, pt, p=p: (layer, pt[b, p], 0, 0))
                  for p in range(n_pages)]
    grid_spec = pltpu.PrefetchScalarGridSpec(
        num_scalar_prefetch=1,
        grid=(db,),
        in_specs=[pl.BlockSpec(memory_space=pltpu.SMEM),
                  pl.BlockSpec((dec_seq, w), row), pl.BlockSpec((dec_seq, w), row), pl.BlockSpec((dec_seq, w), row)]
                 + page_specs + page_specs,
        out_specs=pl.BlockSpec((dec_seq, w), row),
    )
    return pl.pallas_call(
        functools.partial(_sb_sample_kernel, n_pages=n_pages, dec_seq=dec_seq),
        grid_spec=grid_spec,
        out_shape=jax.ShapeDtypeStruct((n, w), F32),
        compiler_params=_cparams(("arbitrary",)),
        name="sb_sample",
    )(page_table, bias, qs, kn, vn, *([cache_k] * n_pages), *([cache_v] * n_pages))


def _rwkv_prologue(rw, prev, mu, w0, w2p, a0, a2p, g2, kkp, ka, rk, ones):
    w = GROUP_W
    xm = rw + (prev - rw) * mu
    r = xm[:, 0:w]
    k = xm[:, w:2 * w]
    v = xm[:, 2 * w:3 * w]
    wa = xm[:, 3 * w:3 * w + LANE]
    gl = xm[:, 3 * w + LANE:]
    wd = w0 + _dot(jnp.tanh(wa), w2p)
    log_decay = -jnp.exp(-_softplus(-wd) - 0.5)
    a = _sigmoid(a0 + _dot(wa, a2p))
    g = _dot(_sigmoid(gl), g2)
    kk = k * kkp
    kk = kk * lax.rsqrt(jnp.maximum(_dotx(kk * kk, ones, 2), 1e-12))
    k_rw = k * (1.0 + (a - 1.0) * ka)
    bonus = _dotx(r * k_rw * rk, ones, 2) * v
    return r, log_decay, k_rw, v, kk, kk * a, g, bonus


def _rwkv_group_norm(o, lnw, lnb, bonus, g, ones):
    mu_h = _dotx(o, ones, 2) * (1.0 / HEAD_DIM)
    dlt = o - mu_h
    var = _dotx(dlt * dlt, ones, 2) * (1.0 / HEAD_DIM)
    return (dlt * lax.rsqrt(var + RWKV_LN_EPS) * lnw + lnb + bonus) * g


def _rwkv_chunk_kernel(rw_ref, mu_ref, w0_ref, w2p_ref, a0_ref, a2p_ref, g2_ref, kkp_ref, ka_ref, rk_ref, lnw_ref, lnb_ref,
                       o_ref, s_out_ref, carry_s, state_s, *, sub):
    i = pl.program_id(1)
    w = GROUP_W
    rows = ROW_BLK
    nsub = rows // sub
    shift = int(math.log2(sub))
    ones = _head_ones(w)

    @pl.when(i == 0)
    def _():
        carry_s[...] = jnp.zeros_like(carry_s)
        state_s[...] = jnp.zeros_like(state_s)

    rw = rw_ref[...]
    prev = jnp.where(_iota((rows, 1), 0) == 0, carry_s[0:1, :], pltpu.roll(rw, 1, 0))
    carry_s[0:1, :] = rw[rows - 1:rows, :]
    r, lw, k, v, kk, b, g, bonus = _rwkv_prologue(
        rw, prev, mu_ref[...], w0_ref[...], w2p_ref[...], a0_ref[...], a2p_ref[...], g2_ref[...],
        kkp_ref[...], ka_ref[...], rk_ref[...], ones)

    r_i = _iota((rows, rows), 0)
    c_i = _iota((rows, rows), 1)
    same = (r_i >> shift) == (c_i >> shift)
    tri = _mask01(same & (c_i <= r_i))
    blk = _mask01(same)
    sub_of_row = _mask01((_iota((rows, LANE), 0) >> shift) == _iota((rows, LANE), 1))
    c = cl = dsum = None
    for part in _split(lw, 3):
        t1 = jnp.dot(tri, part, preferred_element_type=F32)
        t2 = jnp.dot(blk, part, preferred_element_type=F32)
        t3 = lax.dot_general(part, sub_of_row, (((0,), (0,)), ((), ())), preferred_element_type=F32)
        c = t1 if c is None else c + t1
        cl = t2 if cl is None else cl + t2
        dsum = t3 if dsum is None else dsum + t3
    dcol = jnp.exp(dsum)
    kkd = kk * jnp.exp(c - lw)
    rd = r * jnp.exp(c)
    tail = jnp.exp(cl - c)
    khat = k * tail
    bhat = b * tail
    kkw = kk * jnp.exp(-lw)

    local = _iota((rows, w), 0) & (sub - 1)
    y_intra = jnp.zeros((rows, w), F32)
    o_acc = jnp.zeros((rows, w), F32)
    abk = [None]
    abr = []
    for dist in range(sub):
        back = (lambda x: x) if dist == 0 else (lambda x: pltpu.roll(x, dist, 0))
        e = jnp.exp(jnp.where(local >= dist, c - back(c), -1e30))
        kb = back(k) * e
        bb = back(b) * e
        vb = back(v)
        o_acc = o_acc + _dot(r * kb, ones) * vb
        abr.append(_dot(r * bb, ones))
        if dist > 0:
            y_intra = y_intra + _dotx(kkw * kb, ones, 2) * vb
            abk.append(_dotx(kkw * bb, ones, 2))

    bdmask = (_iota((w, w), 0) >> 6) == (_iota((w, w), 1) >> 6)
    m = state_s[...]
    us = []
    o_state = []
    for j in range(nsub):
        sl = slice(j * sub, (j + 1) * sub)
        from_state = _dot(jnp.concatenate([kkd[sl], rd[sl]], axis=0), m)
        y = y_intra[sl] + from_state[:sub]
        o_state.append(from_state[sub:])
        u_rows = []
        for t in range(sub):
            acc = y[t:t + 1, :]
            for s in range(t):
                acc = acc - abk[t - s][j * sub + t:j * sub + t + 1, :] * u_rows[s]
            u_rows.append(acc)
        u = jnp.concatenate(u_rows, axis=0)
        us.append(u)
        kv = _dot_tn(jnp.concatenate([khat[sl], -bhat[sl]], axis=0), jnp.concatenate([v[sl], u], axis=0))
        m = m * dcol[:, j:j + 1] + jnp.where(bdmask, kv, 0.0)
    state_s[...] = m
    u_all = jnp.concatenate(us, axis=0)
    o = o_acc + jnp.concatenate(o_state, axis=0)
    for dist in range(sub):
        o = o - abr[dist] * (u_all if dist == 0 else pltpu.roll(u_all, dist, 0))
    o_ref[...] = _rwkv_group_norm(o, lnw_ref[...], lnb_ref[...], bonus, g, ones)

    @pl.when(i == pl.num_programs(1) - 1)
    def _():
        s_out_ref[...] = m


def _rwkv_prompt(rw, prm, nb, nblk):
    w = GROUP_W
    n, cw = rw.shape
    const = lambda b, i: (0, 0)
    row = lambda b, i: (b * nblk + i, 0)
    return pl.pallas_call(
        functools.partial(_rwkv_chunk_kernel, sub=16),
        grid=(nb, nblk),
        in_specs=[pl.BlockSpec((ROW_BLK, cw), row)] + [pl.BlockSpec(a.shape, const) for a in prm],
        out_specs=[pl.BlockSpec((ROW_BLK, w), row), pl.BlockSpec((None, w, w), lambda b, i: (b, 0, 0))],
        out_shape=[jax.ShapeDtypeStruct((n, w), F32), jax.ShapeDtypeStruct((nb, w, w), F32)],
        scratch_shapes=[pltpu.VMEM((8, cw), F32), pltpu.VMEM((w, w), F32)],
        compiler_params=_cparams(("parallel", "arbitrary")),
        name="rwkv_prompt",
    )(rw, *prm)


def _rwkv_scan(tiles, tseq, s0, op_refs, vt_ref, acc_ref):
    g = len(tiles)
    hd = HEAD_DIM
    kk_ref, w_ref, bk_ref, kr_ref, rr_ref = op_refs
    ones_h = _head_ones(LANE)
    ones_f = jnp.ones((LANE, LANE), BF16)
    j2 = _mask01((_iota((LANE, 2 * LANE), 0) >> 6) == (_iota((LANE, 2 * LANE), 1) >> 7))
    lane_t = _iota((hd, LANE), 1)
    left = lane_t < hd
    x0 = [vt_ref[rb, p, 0:hd, :] for (rb, _, p) in tiles]
    x1 = [vt_ref[rb, p, hd:2 * hd, :] for (rb, _, p) in tiles]
    tile_rows = lambda a, i: a[i * hd:(i + 1) * hd]
    step_rows = 8

    def group(t8, s):
        base = t8 * step_rows
        blks = []
        for ref in op_refs:
            per_tile = []
            for (rb, j, p) in tiles:
                start = rb * ROW_BLK + j * tseq + base
                if not isinstance(start, int):
                    start = pl.multiple_of(start, step_rows)
                per_tile.append(ref[pl.ds(start, step_rows), pl.ds(p * LANE, LANE)])
            blks.append(per_tile)
        for u in range(step_rows):
            row = lambda q, i: blks[q][i][u:u + 1, :]
            s_t = [tile_rows(s, i) for i in range(g)]
            sa = _dotx(jnp.concatenate([s_t[i] * row(0, i) for i in range(g)], axis=0), ones_h, 2)
            msk = [lane_t == (j * tseq + base + u) for (_, j, _) in tiles]
            vsel = jnp.concatenate([jnp.where(msk[i], x0[i], 0.0) for i in range(g)]
                                   + [jnp.where(msk[i], x1[i], 0.0) for i in range(g)], axis=0)
            vc = _dotx(vsel, ones_f, 2)
            new = []
            for i in range(g):
                vcol = jnp.where(left, tile_rows(vc, i), tile_rows(vc, g + i))
                new.append(s_t[i] * row(1, i) - tile_rows(sa, i) * row(2, i) + vcol * row(3, i))
            ro = _dot(jnp.concatenate([new[i] * row(4, i) for i in range(g)], axis=0), j2)
            for i, (rb, _, p) in enumerate(tiles):
                r_i = tile_rows(ro, i)
                acc_ref[rb, p, 0:hd, :] = jnp.where(msk[i], r_i[:, :LANE], acc_ref[rb, p, 0:hd, :])
                acc_ref[rb, p, hd:2 * hd, :] = jnp.where(msk[i], r_i[:, LANE:], acc_ref[rb, p, hd:2 * hd, :])
            s = jnp.concatenate(new, axis=0)
        return s

    assert tseq % step_rows == 0
    if tseq == step_rows:
        return group(0, s0)
    return lax.fori_loop(0, tseq // step_rows, group, s0)


def _rwkv_sample_kernel(rw_ref, ovr_ref, s_in_ref, mu_ref, w0_ref, w2p_ref, a0_ref, a2p_ref, g2_ref, kkp_ref, ka_ref,
                        rk_ref, lnw_ref, lnb_ref, o_ref, s_out_ref,
                        kk_s, w_s, bk_s, kr_s, rr_s, vt_s, acc_s, *, tseq):
    w = GROUP_W
    hd = HEAD_DIM
    ones = _head_ones(w)
    nseq = ROW_BLK // tseq
    rw = rw_ref[0]
    first = (_iota((ROW_BLK, 1), 0) & (tseq - 1)) == 0
    prev = jnp.where(first, ovr_ref[0], pltpu.roll(rw, 1, 0))
    r, log_decay, k_rw, v, kk, bk, g, bonus = _rwkv_prologue(
        rw, prev, mu_ref[...], w0_ref[...], w2p_ref[...], a0_ref[...], a2p_ref[...], g2_ref[...],
        kkp_ref[...], ka_ref[...], rk_ref[...], ones)
    kk_s[...] = kk
    w_s[...] = jnp.exp(log_decay)
    bk_s[...] = bk
    kr_s[...] = k_rw
    rr_s[...] = r
    for p in range(2):
        vt_s[0, p] = v[:, p * LANE:(p + 1) * LANE].T
    acc_s[...] = jnp.zeros_like(acc_s)

    ops = (kk_s, w_s, bk_s, kr_s, rr_s)
    per = 2
    for grp in range(nseq // per):
        tiles = [(0, grp * per + jj, p) for jj in range(per) for p in range(2)]
        rows = pl.ds(grp * per * 2 * hd, per * 2 * hd)
        s_out_ref[rows, :] = _rwkv_scan(tiles, tseq, s_in_ref[rows, :], ops, vt_s, acc_s)

    o = jnp.concatenate([acc_s[0, p].T for p in range(2)], axis=1)
    o_ref[0] = _rwkv_group_norm(o, lnw_ref[...], lnb_ref[...], bonus, g, ones)


def _rwkv_sample(rw, ovr, s_in, prm, tseq):
    w = GROUP_W
    nblocks, _, cw = rw.shape
    nseq = ROW_BLK // tseq
    const = lambda i: (0, 0)
    pspecs = [pl.BlockSpec(a.shape, const) for a in prm]
    blk3 = lambda width: pl.BlockSpec((1, ROW_BLK, width), lambda i: (i, 0, 0))
    st_spec = pl.BlockSpec((nseq * 2 * HEAD_DIM, LANE), lambda i: (i, 0))
    scratch = ([pltpu.VMEM((ROW_BLK, w), F32) for _ in range(5)]
               + [pltpu.VMEM((1, 2, ROW_BLK, LANE), F32), pltpu.VMEM((1, 2, ROW_BLK, LANE), F32)])
    return pl.pallas_call(
        functools.partial(_rwkv_sample_kernel, tseq=tseq),
        grid=(nblocks,),
        in_specs=[blk3(cw), blk3(cw), st_spec] + pspecs,
        out_specs=[blk3(w), st_spec],
        out_shape=[jax.ShapeDtypeStruct((nblocks, ROW_BLK, w), F32), jax.ShapeDtypeStruct(s_in.shape, F32)],
        scratch_shapes=scratch,
        compiler_params=_cparams(("parallel",)),
        name="rwkv_sample",
    )(rw, ovr, s_in, *prm)


ROUTE_W = LANE


def _out_router_kernel(x_ref, oab_ref, oc_ref, od_ref, wo_ref, g2_ref, wr_hi_ref, wr_lo_ref, br_ref,
                       x1_ref, h_ref, gate_ref, *, n_experts, n_groups, null_rows, tm):
    w = GROUP_W
    x1 = (x_ref[...]
          + _dot(oab_ref[...], wo_ref[0:2 * w, :])
          + _dot(oc_ref[...], wo_ref[2 * w:3 * w, :])
          + _dot(od_ref[...], wo_ref[3 * w:4 * w, :]))
    if null_rows:
        rowg = pl.program_id(0) * tm + _iota((tm, 1), 0)
        null = rowg < 0
        for start in null_rows:
            null = null | ((rowg >= start) & (rowg < start + PAD_FRONT))
        x1 = jnp.where(null, 0.0, x1)
    x1_ref[...] = x1
    ms = jnp.mean(x1 * x1, axis=-1, keepdims=True)
    h = x1 * lax.rsqrt(ms + NORM_EPS) * g2_ref[...]
    h_hi = h.astype(BF16)
    h_lo = (h - h_hi.astype(F32)).astype(BF16)
    h_ref[...] = h_hi
    lg = (jnp.dot(h_hi, wr_hi_ref[...], preferred_element_type=F32)
          + jnp.dot(h_lo, wr_hi_ref[...], preferred_element_type=F32)
          + jnp.dot(h_hi, wr_lo_ref[...], preferred_element_type=F32)) + br_ref[...]
    lane = _iota(lg.shape, 1)
    big = jnp.int32(1 << 20)
    neg = jnp.float32(-jnp.inf)
    is_g = (lane >= n_experts) & (lane < n_experts + n_groups)
    gl = jnp.where(is_g, lg, neg)
    gmax = jnp.max(gl, axis=-1, keepdims=True)
    gidx = jnp.min(jnp.where(gl == gmax, lane, big), axis=-1, keepdims=True) - n_experts
    g_w = 1.0 / jnp.sum(jnp.where(is_g, jnp.exp(lg - gmax), 0.0), axis=-1, keepdims=True)
    per = n_experts // n_groups
    in_group = (lane >= gidx * per) & (lane < gidx * per + per)
    el = jnp.where(in_group, lg, neg)
    v1 = jnp.max(el, axis=-1, keepdims=True)
    i1 = jnp.min(jnp.where(el == v1, lane, big), axis=-1, keepdims=True)
    el2 = jnp.where(lane == i1, neg, el)
    v2 = jnp.max(el2, axis=-1, keepdims=True)
    i2 = jnp.min(jnp.where(el2 == v2, lane, big), axis=-1, keepdims=True)
    e21 = jnp.exp(v2 - v1)
    p1 = 1.0 / (1.0 + e21)
    p2 = e21 / (1.0 + e21)
    gate_ref[...] = jnp.where(lane == i1, p1 * g_w, 0.0) + jnp.where(lane == i2, p2 * g_w, 0.0)


def _out_router(x, oab, oc, od, wo, g2, wr_hi, wr_lo, br, n_experts, n_groups, null_rows):
    n, d = x.shape
    w = GROUP_W
    tm = 256 if n % 256 == 0 else ROW_BLK
    row = lambda i: (i, 0)
    const = lambda i: (0, 0)
    return pl.pallas_call(
        functools.partial(_out_router_kernel, n_experts=n_experts, n_groups=n_groups, null_rows=null_rows, tm=tm),
        grid=(n // tm,),
        in_specs=[pl.BlockSpec((tm, d), row), pl.BlockSpec((tm, 2 * w), row), pl.BlockSpec((tm, w), row),
                  pl.BlockSpec((tm, w), row), pl.BlockSpec(wo.shape, const), pl.BlockSpec((1, d), const),
                  pl.BlockSpec(wr_hi.shape, const), pl.BlockSpec(wr_lo.shape, const), pl.BlockSpec((1, ROUTE_W), const)],
        out_specs=[pl.BlockSpec((tm, d), row), pl.BlockSpec((tm, d), row), pl.BlockSpec((tm, ROUTE_W), row)],
        out_shape=[jax.ShapeDtypeStruct((n, d), F32), jax.ShapeDtypeStruct((n, d), BF16),
                   jax.ShapeDtypeStruct((n, ROUTE_W), F32)],
        compiler_params=_cparams(("parallel",)),
        name="out_router",
    )(x, oab, oc, od, wo, g2, wr_hi, wr_lo, br)


MOE_EXPERTS_PER_STEP = 2


def _moe_kernel(x1_ref, h_ref, gate_ref, w1_ref, w3_ref, w2_ref, y_ref):
    step = pl.program_id(1)

    @pl.when(step == 0)
    def _():
        y_ref[...] = x1_ref[...]

    h = h_ref[...]
    gates = gate_ref[...]
    lane = _iota(gates.shape, 1)
    hids = []
    for j in range(MOE_EXPERTS_PER_STEP):
        e = step * MOE_EXPERTS_PER_STEP + j
        ge = jnp.sum(jnp.where(lane == e, gates, 0.0), axis=-1, keepdims=True)
        up = jnp.dot(h, w1_ref[j].astype(BF16), preferred_element_type=F32)
        lin = jnp.dot(h, w3_ref[j].astype(BF16), preferred_element_type=F32)
        hids.append(((up * _sigmoid(up)) * lin * ge).astype(BF16))
    w2 = jnp.concatenate([w2_ref[j].astype(BF16) for j in range(MOE_EXPERTS_PER_STEP)], axis=0)
    y_ref[...] += jnp.dot(jnp.concatenate(hids, axis=1), w2, preferred_element_type=F32)


def _moe_tile(n):
    for t in (1536, 1408, 1280, 1024, 640, 512, 256, 128):
        if n % t == 0:
            return t
    raise ValueError(f"row count {n} is not a multiple of {ROW_BLK}")


def _moe(x1, h, gates, w1, w3, w2, layer):
    n, d = x1.shape
    n_experts, _, f = w1.shape[1:]
    tm = _moe_tile(n)
    eb = MOE_EXPERTS_PER_STEP
    assert n_experts % eb == 0
    row = lambda i, e: (i, 0)
    return pl.pallas_call(
        _moe_kernel,
        grid=(n // tm, n_experts // eb),
        in_specs=[pl.BlockSpec((tm, d), row), pl.BlockSpec((tm, d), row), pl.BlockSpec((tm, ROUTE_W), row),
                  pl.BlockSpec((None, eb, d, f), lambda i, e: (layer, e, 0, 0)),
                  pl.BlockSpec((None, eb, d, f), lambda i, e: (layer, e, 0, 0)),
                  pl.BlockSpec((None, eb, f, d), lambda i, e: (layer, e, 0, 0))],
        out_specs=pl.BlockSpec((tm, d), row),
        out_shape=jax.ShapeDtypeStruct((n, d), F32),
        compiler_params=_cparams(("parallel", "arbitrary")),
        name="moe",
    )(x1, h, gates, w1, w3, w2)


def _rope_tables(pos):
    half = HEAD_DIM // 2
    inv = ROPE_BASE ** (-jnp.arange(half, dtype=F32) / half)
    ang = pos.astype(F32)[:, None] * jnp.tile(inv, LANE // half)[None, :]
    return jnp.cos(ang), jnp.sin(ang)


def _retention_log_gamma():
    lg = jnp.log1p(-jnp.exp2(-5.0 - jnp.arange(N_HG, dtype=F32)))
    return jnp.broadcast_to(jnp.repeat(lg, HEAD_DIM)[None, :], (ROW_BLK, GROUP_W))


def _row(v):
    return v.reshape(1, -1).astype(F32)


def _rwkv_params(p):
    w2 = p['rwkv_w2']
    a2 = p['rwkv_a2']
    w2p = jnp.concatenate([w2, jnp.zeros_like(a2)], axis=0)
    a2p = jnp.concatenate([jnp.zeros_like(w2), a2], axis=0)
    return (_row(p['rwkv_mu']), _row(p['rwkv_w0']), w2p, _row(p['rwkv_a0']), a2p, p['rwkv_g2'],
            _row(p['rwkv_kk']), _row(p['rwkv_ka']), _row(p['rwkv_rk']), _row(p['rwkv_ln_w']), _row(p['rwkv_ln_b']))


def _rwkv_state_to_tiles(s):
    n = s.shape[0]
    return s.reshape(n, 2, 2, HEAD_DIM, HEAD_DIM).transpose(0, 1, 3, 2, 4).reshape(n * 2 * HEAD_DIM, LANE)


def _rwkv_tiles_to_state(t, n):
    return t.reshape(n, 2, HEAD_DIM, 2, HEAD_DIM).transpose(0, 1, 3, 2, 4).reshape(n, N_HG, HEAD_DIM, HEAD_DIM)


def _diag_heads(s):
    return jnp.stack([s[:, h * HEAD_DIM:(h + 1) * HEAD_DIM, h * HEAD_DIM:(h + 1) * HEAD_DIM] for h in range(N_HG)], axis=1)


def kernel(x_prompt, x_sample, cache_sb_k, cache_sb_v, state_ret, state_hgrn, state_rwkv, state_rwkv_shift,
           page_table, meta_tokens, norm1, norm2, w_in, w_out, ret_norm, hgrn_lb_logits, hgrn_norm,
           sb_q_norm, sb_k_norm, sb_bias, rwkv_mu, rwkv_w0, rwkv_w2, rwkv_a0, rwkv_a2, rwkv_g2, rwkv_kk, rwkv_ka,
           rwkv_rk, rwkv_ln_w, rwkv_ln_b, moe_w_group, moe_b_group, moe_w_expert, moe_b_expert,
           moe_w1, moe_w3, moe_w2):
    bp, seq, dm = x_prompt.shape
    db, ds, _ = x_sample.shape
    depth = w_in.shape[0]
    w = GROUP_W
    assert seq % ROW_BLK == 0 and (db * ds) % ROW_BLK == 0 and ROW_BLK % ds == 0 and ds & (ds - 1) == 0
    assert w_in.shape[2] == LIN_COLS + 4 * w and cache_sb_k.shape[3] * cache_sb_k.shape[4] == w
    tp = seq + SB_QBLK
    nblk = tp // ROW_BLK
    tlen = seq + N_META
    past = page_table.shape[1] * cache_sb_k.shape[2]
    n_groups, e_per = moe_w_expert.shape[2:]
    n_experts = n_groups * e_per
    assert n_experts + n_groups <= ROUTE_W

    xp = jnp.concatenate([jnp.zeros((bp, PAD_FRONT, dm), F32),
                          jnp.broadcast_to(meta_tokens[None], (bp, N_META, dm)).astype(F32), x_prompt], axis=1)
    xp = xp.reshape(bp * tp, dm)
    xs = x_sample.reshape(db * ds, dm)
    null_rows = tuple(b * tp for b in range(bp))

    cos_p, sin_p = _rope_tables(jnp.maximum(jnp.arange(tp) - PAD_FRONT, 0))
    cos_s, sin_s = _rope_tables(jnp.tile(past + jnp.arange(ds), ROW_BLK // ds))
    lgam = _retention_log_gamma()
    cache_k = cache_sb_k.reshape(cache_sb_k.shape[:3] + (w,))
    cache_v = cache_sb_v.reshape(cache_sb_v.shape[:3] + (w,))
    tile_heads = lambda v: jnp.tile(v, N_HG)[None, :].astype(F32)

    prompt_rows, sample_rows = [], []
    for l in range(depth):
        wa = w_in[l][:, :LIN_COLS].astype(BF16)
        wb = w_in[l][:, LIN_COLS:].astype(BF16)
        g1 = _row(norm1[l])
        lin_args = (lgam, _row(ret_norm[l]), hgrn_lb_logits.astype(F32), _row(hgrn_norm[l]),
                    tile_heads(sb_q_norm[l]), tile_heads(sb_k_norm[l]))
        rwkv_prm = _rwkv_params(dict(rwkv_mu=rwkv_mu[l], rwkv_w0=rwkv_w0[l], rwkv_w2=rwkv_w2[l], rwkv_a0=rwkv_a0[l],
                                     rwkv_a2=rwkv_a2[l], rwkv_g2=rwkv_g2[l], rwkv_kk=rwkv_kk[l], rwkv_ka=rwkv_ka[l],
                                     rwkv_rk=rwkv_rk[l], rwkv_ln_w=rwkv_ln_w[l], rwkv_ln_b=rwkv_ln_b[l]))
        wo = w_out[l].astype(BF16)
        wr = jnp.zeros((dm, ROUTE_W), F32)
        wr = wr.at[:, :n_experts].set(moe_w_expert[l].reshape(dm, n_experts))
        wr = wr.at[:, n_experts:n_experts + n_groups].set(moe_w_group[l])
        wr_hi = wr.astype(BF16)
        wr_lo = (wr - wr_hi.astype(F32)).astype(BF16)
        br = jnp.zeros((1, ROUTE_W), F32)
        br = br.at[0, :n_experts].set(moe_b_expert[l].reshape(n_experts))
        br = br.at[0, n_experts:n_experts + n_groups].set(moe_b_group[l])
        router = (wo, _row(norm2[l]), wr_hi, wr_lo, br)

        pa, pb = _in_proj(xp, g1, wa, wb)
        oab, qs, knf, knb, vb, s_ret, s_hg = _lin_mix_prompt(pa, cos_p, sin_p, *lin_args, l, bp, nblk)
        oc = _sb_prompt(sb_bias[l].astype(F32), qs, knb, vb, bp, tp)
        od, s_rw = _rwkv_prompt(pb, rwkv_prm, bp, nblk)
        x1, h2, gates = _out_router(xp, oab, oc, od, *router, n_experts, n_groups, null_rows)
        xp = _moe(x1, h2, gates, moe_w1, moe_w3, moe_w2, l)
        real = lambda a: a.reshape(bp, tp, -1)[:, PAD_FRONT:]
        prompt_rows.append((real(knf).reshape(bp, tlen, N_HG, HEAD_DIM),
                            real(pa[:, LIN_COLS - w:]).reshape(bp, tlen, N_HG, HEAD_DIM),
                            _diag_heads(s_ret), _diag_heads(s_hg), _diag_heads(s_rw).swapaxes(-1, -2),
                            pb.reshape(bp, tp, -1)[:, -1]))

        sa, sb = _in_proj(xs, g1, wa, wb)
        stack = lambda s: s.reshape(db, w, HEAD_DIM)
        oab, qs, knf, vf, s_ret, s_hg = _lin_mix_sample(sa, cos_s, sin_s, *lin_args, stack(state_ret[l]),
                                                        stack(state_hgrn[l]), l, ds)
        oc = _sb_sample(page_table, sb_bias[l].astype(F32), qs, knf, vf, cache_k, cache_v, l, ds)
        cw = sb.shape[-1]
        ovr = jnp.concatenate([state_rwkv_shift[l][:, None, :], jnp.zeros((db, ds - 1, cw), F32)], axis=1)
        od, s_rw = _rwkv_sample(sb.reshape(-1, ROW_BLK, cw), ovr.reshape(-1, ROW_BLK, cw),
                                _rwkv_state_to_tiles(state_rwkv[l]), rwkv_prm, ds)
        x1, h2, gates = _out_router(xs, oab, oc, od.reshape(db * ds, w), *router, n_experts, n_groups, ())
        xs = _moe(x1, h2, gates, moe_w1, moe_w3, moe_w2, l)
        sample_rows.append((knf.reshape(db, ds, N_HG, HEAD_DIM), vf.reshape(db, ds, N_HG, HEAD_DIM),
                            s_ret.reshape(db, N_HG, HEAD_DIM, HEAD_DIM), s_hg.reshape(db, N_HG, HEAD_DIM, HEAD_DIM),
                            _rwkv_tiles_to_state(s_rw, db), sb.reshape(db, ds, cw)[:, -1]))

    y_prompt = xp.reshape(bp, tp, dm)[:, SB_QBLK:]
    y_sample = xs.reshape(db, ds, dm)
    stacked_p = [jnp.stack(r) for r in zip(*prompt_rows)]
    stacked_s = [jnp.stack(r) for r in zip(*sample_rows)]
    return (y_prompt, y_sample, *stacked_p, *stacked_s)
```

```python
import functools
import math

import numpy as np
import jax
import jax.numpy as jnp
from jax import lax
from jax.experimental import pallas as pl
from jax.experimental.pallas import tpu as pltpu

F32 = jnp.float32
BF16 = jnp.bfloat16

HEAD_DIM = 64
N_HG = 4
GROUP_W = N_HG * HEAD_DIM
N_META = 16
LANE = 128
ROW_BLK = 128
SB_QBLK = 256
PAD_FRONT = SB_QBLK - N_META
LIN_COLS = 11 * GROUP_W
NORM_EPS = 1e-6
RWKV_LN_EPS = 64e-5
ROPE_BASE = 10000.0
LOG2E = math.log2(math.e)
SB_SCALE = HEAD_DIM ** -0.5 * LOG2E
E_PER_GROUP = 8
VMEM_LIMIT = 56 * 1024 * 1024


def _iota(shape, dim):
    return lax.broadcasted_iota(jnp.int32, shape, dim)


def _mask01(cond):
    return jnp.where(cond, 1.0, 0.0).astype(BF16)


def _dot(a, b):
    return jnp.dot(a.astype(BF16), b.astype(BF16), preferred_element_type=F32)


def _dot_nt(a, b):
    return lax.dot_general(a.astype(BF16), b.astype(BF16), (((1,), (1,)), ((), ())),
                           preferred_element_type=F32)


def _dot_tn(a, b):
    return lax.dot_general(a.astype(BF16), b.astype(BF16), (((0,), (0,)), ((), ())),
                           preferred_element_type=F32)


def _split(x, n):
    parts = []
    r = x
    for _ in range(n):
        h = r.astype(BF16)
        parts.append(h)
        r = r - h.astype(F32)
    return parts


def _dotx(x, m, n=2):
    out = None
    for p in _split(x, n):
        t = jnp.dot(p, m, preferred_element_type=F32)
        out = t if out is None else out + t
    return out


def _xdot(m, x, n=3):
    out = None
    for p in _split(x, n):
        t = jnp.dot(m, p, preferred_element_type=F32)
        out = t if out is None else out + t
    return out


def _head_ones(w):
    return _mask01((_iota((w, w), 0) >> 6) == (_iota((w, w), 1) >> 6))


def _sigmoid(x):
    return 1.0 / (1.0 + jnp.exp(-x))


def _softplus(x):
    return jnp.maximum(x, 0.0) + jnp.log(1.0 + jnp.exp(-jnp.abs(x)))


def _softplus2(x):
    return jnp.maximum(x, 0.0) + jnp.log2(1.0 + jnp.exp2(-jnp.abs(x)))


def _head_rms(x, gain, ones):
    ms = _dotx(x * x, ones, 2) * (1.0 / HEAD_DIM)
    return x * lax.rsqrt(ms + NORM_EPS) * gain


def _row_tile(n):
    for t in (512, 256, 128):
        if n % t == 0:
            return t
    raise ValueError(f"row count {n} is not a multiple of {ROW_BLK}")


def _cparams(sem, flags=None):
    return pltpu.CompilerParams(dimension_semantics=sem, vmem_limit_bytes=VMEM_LIMIT, flags=flags)


def _in_proj_kernel(x_ref, g_ref, wa_ref, wb_ref, oa_ref, ob_ref):
    x = x_ref[...]
    ms = jnp.mean(x * x, axis=-1, keepdims=True)
    h = (x * lax.rsqrt(ms + NORM_EPS) * g_ref[...]).astype(BF16)
    oa_ref[...] = jnp.dot(h, wa_ref[...], preferred_element_type=F32)
    ob_ref[...] = jnp.dot(h, wb_ref[...], preferred_element_type=F32)


def _in_proj(x, g, wa, wb):
    n, d = x.shape
    tm = 256 if n % 256 == 0 else ROW_BLK
    return pl.pallas_call(
        _in_proj_kernel,
        grid=(n // tm,),
        in_specs=[pl.BlockSpec((tm, d), lambda i: (i, 0)),
                  pl.BlockSpec((1, d), lambda i: (0, 0)),
                  pl.BlockSpec(wa.shape, lambda i: (0, 0)),
                  pl.BlockSpec(wb.shape, lambda i: (0, 0))],
        out_specs=[pl.BlockSpec((tm, wa.shape[1]), lambda i: (i, 0)),
                   pl.BlockSpec((tm, wb.shape[1]), lambda i: (i, 0))],
        out_shape=[jax.ShapeDtypeStruct((n, wa.shape[1]), F32),
                   jax.ShapeDtypeStruct((n, wb.shape[1]), F32)],
        compiler_params=_cparams(("parallel",)),
        name="in_proj",
    )(x, g, wa, wb)


def _gla_block(q, k, v, lw, states, sub, chain):
    rows, w = q.shape
    nsub = rows // sub
    shift = int(math.log2(sub))
    r_i = _iota((rows, rows), 0)
    c_i = _iota((rows, rows), 1)
    same = (r_i >> shift) == (c_i >> shift)
    tri = _mask01(same & (c_i <= r_i))
    blk = _mask01(same)
    sub_of_row = _mask01((_iota((rows, LANE), 0) >> shift) == _iota((rows, LANE), 1))
    bc = bl = dsum = None
    for p in _split(lw, 3):
        t1 = jnp.dot(tri, p, preferred_element_type=F32)
        t2 = jnp.dot(blk, p, preferred_element_type=F32)
        t3 = lax.dot_general(p, sub_of_row, (((0,), (0,)), ((), ())), preferred_element_type=F32)
        bc = t1 if bc is None else bc + t1
        bl = t2 if bl is None else bl + t2
        dsum = t3 if dsum is None else dsum + t3
    dcol = jnp.exp(dsum)
    qt = q * jnp.exp(bc)
    kh = k * jnp.exp(bl - bc)
    ones = _head_ones(w)

    local = _iota((rows, w), 0) & (sub - 1)
    o_intra = jnp.zeros((rows, w), F32)
    for dist in range(sub):
        back = (lambda x: x) if dist == 0 else (lambda x: pltpu.roll(x, dist, 0))
        e = jnp.exp(jnp.where(local >= dist, bc - back(bc), -1e30))
        p = q * back(k) * e
        o_intra = o_intra + _dot(p, ones) * back(v)

    bdmask = (_iota((w, w), 0) >> 6) == (_iota((w, w), 1) >> 6)
    outs = []
    new_states = []
    s = states if chain else None
    for i in range(nsub):
        sl = slice(i * sub, (i + 1) * sub)
        if not chain:
            s = states[i]
        outs.append(_dot(qt[sl], s))
        kv = _dot_tn(kh[sl], v[sl])
        s = s * dcol[:, i:i + 1] + jnp.where(bdmask, kv, 0.0)
        if not chain:
            new_states.append(s)
    o = o_intra + jnp.concatenate(outs, axis=0)
    return o, (s if chain else new_states)


def _retention_block(q, k, v, lgam, s):
    rows, w = q.shape
    width = N_HG * rows
    t_row = _iota((rows, w), 0).astype(F32)
    qd = q * jnp.exp((t_row + 1.0) * lgam)
    kd = k * jnp.exp((rows - 1.0 - t_row) * lgam)
    left = _iota((rows, LANE), 1) < HEAD_DIM

    def split_heads(x):
        x = x.astype(BF16)
        zero = jnp.zeros_like(x)
        return jnp.concatenate([jnp.where(left, x, zero), jnp.where(left, zero, x)], axis=0)

    dist = (_iota((rows, width), 0) - (_iota((rows, width), 1) & (rows - 1))).astype(F32)
    lg_heads = jnp.concatenate([jnp.broadcast_to(lgam[:, h * HEAD_DIM:h * HEAD_DIM + 1], (rows, rows))
                                for h in range(N_HG)], axis=1)
    dmat = jnp.where(dist >= 0.0, jnp.exp(jnp.maximum(dist, 0.0) * lg_heads), 0.0)
    scores = jnp.concatenate([_dot_nt(q[:, p * LANE:(p + 1) * LANE], split_heads(k[:, p * LANE:(p + 1) * LANE]))
                              for p in range(2)], axis=1) * dmat
    o_intra = jnp.concatenate([_dot(scores[:, p * 2 * LANE:(p + 1) * 2 * LANE], split_heads(v[:, p * LANE:(p + 1) * LANE]))
                               for p in range(2)], axis=1)
    o = o_intra + _dot(qd, s)
    ones_rl = jnp.ones((rows, LANE), BF16)
    dsum = None
    for part in _split(lgam, 3):
        t3 = lax.dot_general(part, ones_rl, (((0,), (0,)), ((), ())), preferred_element_type=F32)
        dsum = t3 if dsum is None else dsum + t3
    bdmask = (_iota((w, w), 0) >> 6) == (_iota((w, w), 1) >> 6)
    s = s * jnp.exp(dsum)[:, 0:1] + jnp.where(bdmask, _dot_tn(kd, v), 0.0)
    return o, s


def _lin_mix_math(p, cos, sin, lgam, retw, lb_logits, hgw, qnw, knw, layer, st_ret, st_hg, sub, chain):
    w = GROUP_W
    qa, ka, va, ga, qb, fb, ib, gb, qc, kc, vc = [p[:, i * w:(i + 1) * w] for i in range(11)]
    rows = p.shape[0]
    ones = _head_ones(w)
    lane = _iota((rows, LANE), 1)
    first_half = (lane & (HEAD_DIM - 1)) < (HEAD_DIM // 2)

    def rope(x):
        halves = []
        for hp in range(w // LANE):
            xh = x[:, hp * LANE:(hp + 1) * LANE]
            rot = jnp.where(first_half, -pltpu.roll(xh, LANE - HEAD_DIM // 2, 1), pltpu.roll(xh, HEAD_DIM // 2, 1))
            halves.append(xh * cos + rot * sin)
        return jnp.concatenate(halves, axis=1)

    q_ret = rope(qa)
    k_ret = rope(ka) * (HEAD_DIM ** -0.5)
    if chain:
        o_ret, st_ret = _retention_block(q_ret, k_ret, va, lgam, st_ret)
    else:
        o_ret, st_ret = _gla_block(q_ret, k_ret, va, lgam, st_ret, sub, chain)
    o_a = _head_rms(o_ret, retw, ones) * (ga * _sigmoid(ga))

    lg = [lb_logits[d:d + 1, :] for d in range(lb_logits.shape[0])]
    mx = functools.reduce(jnp.maximum, lg)
    ex = [jnp.exp(row - mx) for row in lg]
    lb = sum(ex[1:layer + 1], jnp.zeros_like(mx)) / sum(ex[1:], ex[0])
    log_sig = jnp.minimum(fb, 0.0) - jnp.log1p(jnp.exp(-jnp.abs(fb)))
    t_a = jnp.broadcast_to(jnp.log(lb), fb.shape)
    t_b = jnp.log1p(-lb) + log_sig
    logf = jnp.maximum(t_a, t_b) + jnp.log1p(jnp.exp(-jnp.abs(t_a - t_b)))
    k_hg = (1.0 - lb) * _sigmoid(-fb)
    o_hg, st_hg = _gla_block(qb, k_hg, ib, logf, st_hg, sub, chain)
    o_b = _head_rms(o_hg, hgw, ones) * (gb * _sigmoid(gb))

    qn = _head_rms(qc, qnw, ones)
    kn = _head_rms(kc, knw, ones)
    return o_a, o_b, qn, kn, vc, st_ret, st_hg


def _lin_mix_prompt_kernel(p_ref, cos_ref, sin_ref, lgam_ref, retw_ref, lbl_ref, hgw_ref, qnw_ref, knw_ref,
                           oab_ref, qs_ref, knf_ref, knb_ref, vb_ref, sret_ref, shg_ref,
                           st_ret, st_hg, *, layer, sub):
    i = pl.program_id(1)

    @pl.when(i == 0)
    def _():
        st_ret[...] = jnp.zeros_like(st_ret)
        st_hg[...] = jnp.zeros_like(st_hg)

    o_a, o_b, qn, kn, vc, s1, s2 = _lin_mix_math(
        p_ref[...], cos_ref[...], sin_ref[...], lgam_ref[...], retw_ref[...], lbl_ref, hgw_ref[...],
        qnw_ref[...], knw_ref[...], layer, st_ret[...], st_hg[...], sub, True)
    st_ret[...] = s1
    st_hg[...] = s2
    oab_ref[...] = jnp.concatenate([o_a, o_b], axis=1)
    qs_ref[...] = (qn * SB_SCALE).astype(BF16)
    knf_ref[...] = kn
    knb_ref[...] = kn.astype(BF16)
    vb_ref[...] = vc.astype(BF16)

    @pl.when(i == pl.num_programs(1) - 1)
    def _():
        sret_ref[...] = s1
        shg_ref[...] = s2


def _lin_mix_prompt(proj, cos, sin, lgam, retw, lbl, hgw, qnw, knw, layer, nb, nblk):
    n = proj.shape[0]
    w = GROUP_W
    row = lambda b, i: (b * nblk + i, 0)
    const = lambda b, i: (0, 0)
    outs = pl.pallas_call(
        functools.partial(_lin_mix_prompt_kernel, layer=layer, sub=16),
        grid=(nb, nblk),
        in_specs=[pl.BlockSpec((ROW_BLK, LIN_COLS), row),
                  pl.BlockSpec((ROW_BLK, LANE), lambda b, i: (i, 0)),
                  pl.BlockSpec((ROW_BLK, LANE), lambda b, i: (i, 0)),
                  pl.BlockSpec((ROW_BLK, w), const), pl.BlockSpec((1, w), const),
                  pl.BlockSpec(lbl.shape, const), pl.BlockSpec((1, w), const),
                  pl.BlockSpec((1, w), const), pl.BlockSpec((1, w), const)],
        out_specs=[pl.BlockSpec((ROW_BLK, 2 * w), row),
                   pl.BlockSpec((ROW_BLK, w), row), pl.BlockSpec((ROW_BLK, w), row),
                   pl.BlockSpec((ROW_BLK, w), row), pl.BlockSpec((ROW_BLK, w), row),
                   pl.BlockSpec((None, w, w), lambda b, i: (b, 0, 0)),
                   pl.BlockSpec((None, w, w), lambda b, i: (b, 0, 0))],
        out_shape=[jax.ShapeDtypeStruct((n, 2 * w), F32),
                   jax.ShapeDtypeStruct((n, w), BF16), jax.ShapeDtypeStruct((n, w), F32),
                   jax.ShapeDtypeStruct((n, w), BF16), jax.ShapeDtypeStruct((n, w), BF16),
                   jax.ShapeDtypeStruct((nb, w, w), F32), jax.ShapeDtypeStruct((nb, w, w), F32)],
        scratch_shapes=[pltpu.VMEM((w, w), F32), pltpu.VMEM((w, w), F32)],
        compiler_params=_cparams(("parallel", "arbitrary")),
        name="lin_mix_prompt",
    )(proj, cos, sin, lgam, retw, lbl, hgw, qnw, knw)
    return outs


def _lin_mix_sample_kernel(p_ref, cos_ref, sin_ref, lgam_ref, retw_ref, lbl_ref, hgw_ref, qnw_ref, knw_ref,
                           sret_in, shg_in,
                           oab_ref, qs_ref, knf_ref, vf_ref, sret_out, shg_out, *, layer, sub):
    w = GROUP_W
    nseq = ROW_BLK // sub
    bdmask = (_iota((w, w), 0) >> 6) == (_iota((w, w), 1) >> 6)
    rep = _mask01(_iota((HEAD_DIM, w), 0) == (_iota((HEAD_DIM, w), 1) & (HEAD_DIM - 1)))
    rep_t = _mask01((_iota((w, HEAD_DIM), 0) & (HEAD_DIM - 1)) == _iota((w, HEAD_DIM), 1))

    def expand(ref):
        return [jnp.where(bdmask, _dotx(ref[j], rep, 3), 0.0) for j in range(nseq)]

    def extract(ref, states):
        for j in range(nseq):
            ref[j] = _dotx(states[j], rep_t, 3)

    o_a, o_b, qn, kn, vc, s1, s2 = _lin_mix_math(
        p_ref[...], cos_ref[...], sin_ref[...], lgam_ref[...], retw_ref[...], lbl_ref, hgw_ref[...],
        qnw_ref[...], knw_ref[...], layer, expand(sret_in), expand(shg_in), sub, False)
    extract(sret_out, s1)
    extract(shg_out, s2)
    oab_ref[...] = jnp.concatenate([o_a, o_b], axis=1)
    qs_ref[...] = qn * SB_SCALE
    knf_ref[...] = kn
    vf_ref[...] = vc


def _lin_mix_sample(proj, cos, sin, lgam, retw, lbl, hgw, qnw, knw, s_ret, s_hg, layer, dec_seq):
    n = proj.shape[0]
    w = GROUP_W
    nseq = ROW_BLK // dec_seq
    row = lambda i: (i, 0)
    const = lambda i: (0, 0)
    st_spec = pl.BlockSpec((nseq, w, HEAD_DIM), lambda i: (i, 0, 0))
    return pl.pallas_call(
        functools.partial(_lin_mix_sample_kernel, layer=layer, sub=dec_seq),
        grid=(n // ROW_BLK,),
        in_specs=[pl.BlockSpec((ROW_BLK, LIN_COLS), row),
                  pl.BlockSpec((ROW_BLK, LANE), const), pl.BlockSpec((ROW_BLK, LANE), const),
                  pl.BlockSpec((ROW_BLK, w), const), pl.BlockSpec((1, w), const),
                  pl.BlockSpec(lbl.shape, const), pl.BlockSpec((1, w), const),
                  pl.BlockSpec((1, w), const), pl.BlockSpec((1, w), const),
                  st_spec, st_spec],
        out_specs=[pl.BlockSpec((ROW_BLK, 2 * w), row),
                   pl.BlockSpec((ROW_BLK, w), row), pl.BlockSpec((ROW_BLK, w), row),
                   pl.BlockSpec((ROW_BLK, w), row), st_spec, st_spec],
        out_shape=[jax.ShapeDtypeStruct((n, 2 * w), F32),
                   jax.ShapeDtypeStruct((n, w), F32), jax.ShapeDtypeStruct((n, w), F32),
                   jax.ShapeDtypeStruct((n, w), F32),
                   jax.ShapeDtypeStruct(s_ret.shape, F32), jax.ShapeDtypeStruct(s_hg.shape, F32)],
        compiler_params=_cparams(("parallel",)),
        name="lin_mix_sample",
    )(proj, cos, sin, lgam, retw, lbl, hgw, qnw, knw, s_ret, s_hg)


def _sb_consts(rows, tmask):
    j_i = _iota((LANE, 2 * LANE), 0)
    s_i = _iota((LANE, 2 * LANE), 1)
    ucat = _mask01((s_i >= LANE) | (j_i >= s_i))
    causal = _iota((rows, LANE), 1) < (_iota((rows, LANE), 0) & tmask)
    return ucat, causal


def _sb_block(z, carry, ucat, causal):
    sp = _softplus2(z)
    if causal is not None:
        sp = jnp.where(causal, sp, 0.0)
    t = _dotx(sp, ucat, 2)
    a = jnp.exp2(z - (carry + t[:, :LANE]))
    if causal is not None:
        a = jnp.where(causal, a, 0.0)
    return a, carry + t[:, LANE:]


def _sb_prompt_kernel(bias_ref, q_ref, k_ref, v_ref, o_ref):
    i = pl.program_id(1)
    qb = SB_QBLK
    kpq = qb // ROW_BLK
    q = q_ref[...]
    left = _iota((ROW_BLK, LANE), 1) < HEAD_DIM
    zero = jnp.zeros((ROW_BLK, LANE), BF16)
    width = N_HG * ROW_BLK
    lane = _iota((qb, width), 1)
    bias_row = jnp.zeros((qb, width), F32)
    for h in range(N_HG):
        bias_row = jnp.where((lane >> 7) == h, bias_ref[h] * LOG2E, bias_row)
    sk = lane & (ROW_BLK - 1)
    tq = _iota((qb, width), 0)
    r2 = _iota((2 * LANE, 2 * LANE), 0)
    c2 = _iota((2 * LANE, 2 * LANE), 1)
    ubd = _mask01(((r2 >> 7) == (c2 >> 7)) & ((r2 & (LANE - 1)) > (c2 & (LANE - 1))))

    def split_heads(x):
        return jnp.concatenate([jnp.where(left, x, zero), jnp.where(left, zero, x)], axis=0)

    pair = lambda x, p: x[:, p * 2 * LANE:(p + 1) * 2 * LANE]

    def steps(kbs, masks, carry, acc0, acc1):
        blocks = []
        for kb in kbs:
            start = pl.multiple_of(kb * ROW_BLK, ROW_BLK)
            blocks.append((k_ref[pl.ds(start, ROW_BLK), :], v_ref[pl.ds(start, ROW_BLK), :]))
        zs = [jnp.concatenate([_dot_nt(q[:, p * LANE:(p + 1) * LANE], split_heads(kblk[:, p * LANE:(p + 1) * LANE]))
                               for p in range(2)], axis=1) + bias_row for kblk, _ in blocks]
        sp_own = [_softplus2(z) for z in zs]
        sps = [s if m is None else jnp.where(m, s, 0.0) for s, m in zip(sp_own, masks)]
        ts = []
        for sp in sps:
            spb = sp.astype(BF16)
            ts.append(jnp.concatenate([jnp.dot(pair(spb, p), ubd, preferred_element_type=F32) for p in range(2)], axis=1))
        abs_ = []
        for z, own, sp, t, m in zip(zs, sp_own, sps, ts, masks):
            a = jnp.exp2((z - own) - (carry + t))
            if m is not None:
                a = jnp.where(m, a, 0.0)
            abs_.append(a.astype(BF16))
            carry = carry + jnp.concatenate(
                [jnp.broadcast_to(jnp.sum(sp[:, h * ROW_BLK:(h + 1) * ROW_BLK], axis=-1, keepdims=True), (qb, ROW_BLK))
                 for h in range(N_HG)], axis=1)
        accs = [acc0, acc1]
        for ab, (_, vblk) in zip(abs_, blocks):
            for p in range(2):
                accs[p] = accs[p] + jnp.dot(pair(ab, p), split_heads(vblk[:, p * LANE:(p + 1) * LANE]),
                                            preferred_element_type=F32)
        return carry, accs[0], accs[1]

    zeros = jnp.zeros((qb, LANE), F32)
    state = (jnp.zeros((qb, width), F32), zeros, zeros)
    state = steps([i * kpq + d for d in reversed(range(kpq))], [sk + d * ROW_BLK < tq for d in reversed(range(kpq))],
                  *state)

    def earlier_blocks(j, c):
        return steps([(i - j) * kpq - 1 - d for d in range(kpq)], [None] * kpq, *c)

    state = lax.fori_loop(0, i, earlier_blocks, state)
    o_ref[...] = jnp.concatenate([state[1], state[2]], axis=1)


def _sb_prompt(bias, qs, kb, vb, nb, tp):
    n, w = qs.shape
    nq = tp // SB_QBLK
    return pl.pallas_call(
        _sb_prompt_kernel,
        grid=(nb, nq),
        in_specs=[pl.BlockSpec(memory_space=pltpu.SMEM),
                  pl.BlockSpec((SB_QBLK, w), lambda b, i: (b * nq + i, 0)),
                  pl.BlockSpec((tp, w), lambda b, i: (b, 0)),
                  pl.BlockSpec((tp, w), lambda b, i: (b, 0))],
        out_specs=pl.BlockSpec((SB_QBLK, w), lambda b, i: (b * nq + i, 0)),
        out_shape=jax.ShapeDtypeStruct((n, w), F32),
        compiler_params=_cparams(("parallel", "arbitrary")),
        name="sb_prompt",
    )(bias, qs, kb, vb)


def _sb_sample_kernel(pt_ref, bias_ref, q_ref, kn_ref, vn_ref, *rest, n_pages, dec_seq):
    del pt_ref
    k_pages = rest[:n_pages]
    v_pages = rest[n_pages:2 * n_pages]
    o_ref = rest[2 * n_pages]
    w = GROUP_W
    rows = N_HG * dec_seq
    q = q_ref[...]
    head_of_lane = _iota((dec_seq, w), 1) >> 6
    qbd = jnp.concatenate([jnp.where(head_of_lane == h, q, 0.0) for h in range(N_HG)], axis=0).astype(BF16)
    bias_col = jnp.concatenate([jnp.full((dec_seq, 1), bias_ref[h] * LOG2E, F32) for h in range(N_HG)], axis=0)
    ucat, causal = _sb_consts(rows, dec_seq - 1)
    pad = jnp.zeros((LANE - dec_seq, w), F32)
    k_new = jnp.concatenate([kn_ref[...], pad], axis=0)
    v_new = jnp.concatenate([vn_ref[...], pad], axis=0)
    a, carry = _sb_block(_dot_nt(qbd, k_new) + bias_col, jnp.zeros((rows, LANE), F32), ucat, causal)
    acc = _dot(a, v_new)
    for p in reversed(range(n_pages)):
        a, carry = _sb_block(_dot_nt(qbd, k_pages[p][...]) + bias_col, carry, ucat, None)
        acc = acc + _dot(a, v_pages[p][...])
    out = jnp.zeros((dec_seq, w), F32)
    for h in range(N_HG):
        out = out + jnp.where(head_of_lane == h, acc[h * dec_seq:(h + 1) * dec_seq], 0.0)
    o_ref[...] = out


def _sb_sample(page_table, bias, qs, kn, vn, cache_k, cache_v, layer, dec_seq):
    n, w = qs.shape
    db, n_pages = page_table.shape
    page = cache_k.shape[2]
    assert page == LANE and dec_seq % 8 == 0
    row = lambda b, pt: (b, 0)
    page_specs = [pl.BlockSpec((None, None, page, w), lambda b, pt, p=p: (layer, pt[b, p], 0, 0))
                  for p in range(n_pages)]
    grid_spec = pltpu.PrefetchScalarGridSpec(
        num_scalar_prefetch=1,
        grid=(db,),
        in_specs=[pl.BlockSpec(memory_space=pltpu.SMEM),
                  pl.BlockSpec((dec_seq, w), row), pl.BlockSpec((dec_seq, w), row), pl.BlockSpec((dec_seq, w), row)]
                 + page_specs + page_specs,
        out_specs=pl.BlockSpec((dec_seq, w), row),
    )
    return pl.pallas_call(
        functools.partial(_sb_sample_kernel, n_pages=n_pages, dec_seq=dec_seq),
        grid_spec=grid_spec,
        out_shape=jax.ShapeDtypeStruct((n, w), F32),
        compiler_params=_cparams(("arbitrary",)),
        name="sb_sample",
    )(page_table, bias, qs, kn, vn, *([cache_k] * n_pages), *([cache_v] * n_pages))


def _rwkv_prologue(rw, prev, mu, w0, w2p, a0, a2p, g2, kkp, ka, rk, ones):
    w = GROUP_W
    xm = rw + (prev - rw) * mu
    r = xm[:, 0:w]
    k = xm[:, w:2 * w]
    v = xm[:, 2 * w:3 * w]
    wa = xm[:, 3 * w:3 * w + LANE]
    gl = xm[:, 3 * w + LANE:]
    wd = w0 + _dot(jnp.tanh(wa), w2p)
    log_decay = -jnp.exp(-_softplus(-wd) - 0.5)
    a = _sigmoid(a0 + _dot(wa, a2p))
    g = _dot(_sigmoid(gl), g2)
    kk = k * kkp
    kk = kk * lax.rsqrt(jnp.maximum(_dotx(kk * kk, ones, 2), 1e-12))
    k_rw = k * (1.0 + (a - 1.0) * ka)
    bonus = _dotx(r * k_rw * rk, ones, 2) * v
    return r, log_decay, k_rw, v, kk, kk * a, g, bonus


RWKV_SUB = 64


def _rwkv_group_norm(o, lnw, lnb, bonus, g, ones):
    mu_h = _dotx(o, ones, 2) * (1.0 / HEAD_DIM)
    dlt = o - mu_h
    var = _dotx(dlt * dlt, ones, 2) * (1.0 / HEAD_DIM)
    return (dlt * lax.rsqrt(var + RWKV_LN_EPS) * lnw + lnb + bonus) * g


def _rwkv_chunk_kernel(rw_ref, mu_ref, w0_ref, w2p_ref, a0_ref, a2p_ref, g2_ref, kkp_ref, ka_ref, rk_ref, lnw_ref, lnb_ref,
                       o_ref, s_out_ref, carry_s, state_s, *, sub, nseq):
    prm = (mu_ref, w0_ref, w2p_ref, a0_ref, a2p_ref, g2_ref, kkp_ref, ka_ref, rk_ref, lnw_ref, lnb_ref)
    seqs = [_rwkv_chunk_sequence(q, rw_ref, prm, o_ref, s_out_ref, carry_s, state_s, sub) for q in range(nseq)]
    while all([next(seq, False) for seq in seqs]):
        pass


def _rwkv_chunk_sequence(seq, rw_ref, prm, o_ref, s_out_ref, carry_s, state_s, sub):
    mu_ref, w0_ref, w2p_ref, a0_ref, a2p_ref, g2_ref, kkp_ref, ka_ref, rk_ref, lnw_ref, lnb_ref = prm
    i = pl.program_id(0)
    w = GROUP_W
    rows = ROW_BLK
    nsub = rows // sub
    shift = int(math.log2(sub))
    ones = _head_ones(w)

    @pl.when(i == 0)
    def _():
        carry_s[seq] = jnp.zeros(carry_s.shape[1:], F32)
        state_s[seq] = jnp.zeros(state_s.shape[1:], F32)

    rw = rw_ref[seq]
    prev = jnp.where(_iota((rows, 1), 0) == 0, carry_s[seq, 0:1, :], pltpu.roll(rw, 1, 0))
    carry_s[seq, 0:1, :] = rw[rows - 1:rows, :]
    r, lw, k, v, kk, b, g, bonus = _rwkv_prologue(
        rw, prev, mu_ref[...], w0_ref[...], w2p_ref[...], a0_ref[...], a2p_ref[...], g2_ref[...],
        kkp_ref[...], ka_ref[...], rk_ref[...], ones)
    yield True

    r_i = _iota((rows, rows), 0)
    c_i = _iota((rows, rows), 1)
    same = (r_i >> shift) == (c_i >> shift)
    tri = _mask01(same & (c_i <= r_i))
    blk = _mask01(same)
    sub_of_row = _mask01((_iota((rows, LANE), 0) >> shift) == _iota((rows, LANE), 1))
    c = cl = dsum = None
    for part in _split(lw, 3):
        t1 = jnp.dot(tri, part, preferred_element_type=F32)
        t2 = jnp.dot(blk, part, preferred_element_type=F32)
        t3 = lax.dot_general(part, sub_of_row, (((0,), (0,)), ((), ())), preferred_element_type=F32)
        c = t1 if c is None else c + t1
        cl = t2 if cl is None else cl + t2
        dsum = t3 if dsum is None else dsum + t3
    yield True
    dcol = jnp.exp(dsum)
    kkd = kk * jnp.exp(c - lw)
    rd = r * jnp.exp(c)
    grow = jnp.exp(-c)
    kt = k * grow
    bt = b * grow
    tail = jnp.exp(cl - c)
    khat = k * tail
    bhat = b * tail

    left = _iota((rows, LANE), 1) < HEAD_DIM

    def split_heads(x):
        zero = jnp.zeros_like(x)
        return jnp.concatenate([jnp.where(left, x, zero), jnp.where(left, zero, x)], axis=0)

    lanes_p = lambda x, p: x[:, p * LANE:(p + 1) * LANE]
    wide = 2 * LANE
    t_i = _iota((rows, wide), 0)
    s_i = _iota((rows, wide), 1) & (rows - 1)
    same_sub = (t_i >> shift) == (s_i >> shift)
    incl = same_sub & (s_i <= t_i)
    strict = same_sub & (s_i < t_i)

    def dot3(x, y_hi, y_lo):
        x_hi, x_lo = _split(x, 2)
        nt = lambda a_, b_: lax.dot_general(a_, b_, (((1,), (1,)), ((), ())), preferred_element_type=F32)
        return nt(x_hi, y_hi) + nt(x_hi, y_lo) + nt(x_lo, y_hi)

    o_parts, y_parts, abr, abk = [], [], [], []
    for p in range(2):
        keys = jnp.concatenate([split_heads(lanes_p(kt, p)), split_heads(lanes_p(bt, p))], axis=0)
        keys_hi, keys_lo = _split(keys, 2)
        c_r = lax.dot_general(lanes_p(rd, p).astype(BF16), keys_hi, (((1,), (1,)), ((), ())),
                              preferred_element_type=F32)
        c_kk = dot3(lanes_p(kkd, p), keys_hi, keys_lo)
        akr = jnp.where(incl, c_r[:, :wide], 0.0)
        abr.append(jnp.where(incl, c_r[:, wide:], 0.0))
        akk = jnp.where(strict, c_kk[:, :wide], 0.0)
        abk.append(jnp.where(strict, c_kk[:, wide:], 0.0))
        v_heads = split_heads(lanes_p(v, p)).astype(BF16)
        o_parts.append(jnp.dot(akr.astype(BF16), v_heads, preferred_element_type=F32))
        y_parts.append(_dotx(akk, v_heads, 2))
    yield True
    o_acc = jnp.concatenate(o_parts, axis=1)
    y_intra = jnp.concatenate(y_parts, axis=1)

    def mm3(x, y):
        x_hi, x_lo = _split(x, 2)
        y_hi, y_lo = _split(y, 2)
        mm = lambda a_, b_: jnp.dot(a_, b_, preferred_element_type=F32)
        return mm(x_hi, y_hi) + mm(x_hi, y_lo) + mm(x_lo, y_hi)

    head_of_lane = _iota((rows, w), 1) >> 6
    eye = (_iota((rows, rows), 0) == _iota((rows, rows), 1)).astype(F32)
    powers = [-abk[h // 2][:, (h % 2) * rows:(h % 2 + 1) * rows] for h in range(N_HG)]
    invs = [eye + n for n in powers]
    for _ in range(shift - 1):
        powers = [mm3(n, n) for n in powers]
        yield True
        invs = [t + mm3(t, n) for t, n in zip(invs, powers)]
    yield True
    u_intra = jnp.zeros((rows, w), F32)
    g_mat = jnp.zeros((rows, w), F32)
    for h in range(N_HG):
        u_intra = u_intra + mm3(invs[h], jnp.where(head_of_lane == h, y_intra, 0.0))
        g_mat = g_mat + mm3(invs[h], jnp.where(head_of_lane == h, kkd, 0.0))
    yield True

    bdmask = (_iota((w, w), 0) >> 6) == (_iota((w, w), 1) >> 6)
    m = state_s[seq]
    us = []
    o_state = []
    for j in range(nsub):
        sl = slice(j * sub, (j + 1) * sub)
        from_state = _dot(jnp.concatenate([g_mat[sl], rd[sl]], axis=0), m)
        yield True
        u = u_intra[sl] + from_state[:sub]
        o_state.append(from_state[sub:])
        us.append(u)
        kv = _dot_tn(jnp.concatenate([khat[sl], -bhat[sl]], axis=0), jnp.concatenate([v[sl], u], axis=0))
        yield True
        m = m * dcol[:, j:j + 1] + jnp.where(bdmask, kv, 0.0)
    state_s[seq] = m
    u_all = jnp.concatenate(us, axis=0)
    o_u = jnp.concatenate([jnp.dot(abr[p].astype(BF16), split_heads(lanes_p(u_all, p)).astype(BF16),
                                   preferred_element_type=F32) for p in range(2)], axis=1)
    yield True
    o = o_acc + jnp.concatenate(o_state, axis=0) - o_u
    o_ref[seq] = _rwkv_group_norm(o, lnw_ref[...], lnb_ref[...], bonus, g, ones)

    @pl.when(i == pl.num_programs(0) - 1)
    def _():
        s_out_ref[seq] = m


def _rwkv_prompt(rw, prm, nb, nblk):
    w = GROUP_W
    cw = rw.shape[-1]
    const = lambda i: (0, 0)
    blk = lambda i: (0, i, 0)
    return pl.pallas_call(
        functools.partial(_rwkv_chunk_kernel, sub=RWKV_SUB, nseq=nb),
        grid=(nblk,),
        in_specs=[pl.BlockSpec((nb, ROW_BLK, cw), blk)] + [pl.BlockSpec(a.shape, const) for a in prm],
        out_specs=[pl.BlockSpec((nb, ROW_BLK, w), blk), pl.BlockSpec((nb, w, w), lambda i: (0, 0, 0))],
        out_shape=[jax.ShapeDtypeStruct((nb, nblk * ROW_BLK, w), F32), jax.ShapeDtypeStruct((nb, w, w), F32)],
        scratch_shapes=[pltpu.VMEM((nb, 8, cw), F32), pltpu.VMEM((nb, w, w), F32)],
        compiler_params=_cparams(("arbitrary",)),
        name="rwkv_prompt",
    )(rw, *prm)


def _rwkv_scan(tiles, tseq, s0, op_refs, vt_ref, acc_ref):
    g = len(tiles)
    hd = HEAD_DIM
    kk_ref, w_ref, bk_ref, kr_ref, rr_ref = op_refs
    ones_h = _head_ones(LANE)
    ones_f = jnp.ones((LANE, LANE), BF16)
    j2 = _mask01((_iota((LANE, 2 * LANE), 0) >> 6) == (_iota((LANE, 2 * LANE), 1) >> 7))
    lane_t = _iota((hd, LANE), 1)
    left = lane_t < hd
    x0 = [vt_ref[rb, p, 0:hd, :] for (rb, _, p) in tiles]
    x1 = [vt_ref[rb, p, hd:2 * hd, :] for (rb, _, p) in tiles]
    tile_rows = lambda a, i: a[i * hd:(i + 1) * hd]
    step_rows = 8

    def group(t8, s):
        base = t8 * step_rows
        blks = []
        for ref in op_refs:
            per_tile = []
            for (rb, j, p) in tiles:
                start = rb * ROW_BLK + j * tseq + base
                if not isinstance(start, int):
                    start = pl.multiple_of(start, step_rows)
                per_tile.append(ref[pl.ds(start, step_rows), pl.ds(p * LANE, LANE)])
            blks.append(per_tile)
        for u in range(step_rows):
            row = lambda q, i: blks[q][i][u:u + 1, :]
            s_t = [tile_rows(s, i) for i in range(g)]
            sa = _dotx(jnp.concatenate([s_t[i] * row(0, i) for i in range(g)], axis=0), ones_h, 2)
            msk = [lane_t == (j * tseq + base + u) for (_, j, _) in tiles]
            vsel = jnp.concatenate([jnp.where(msk[i], x0[i], 0.0) for i in range(g)]
                                   + [jnp.where(msk[i], x1[i], 0.0) for i in range(g)], axis=0)
            vc = _dotx(vsel, ones_f, 2)
            new = []
            for i in range(g):
                vcol = jnp.where(left, tile_rows(vc, i), tile_rows(vc, g + i))
                new.append(s_t[i] * row(1, i) - tile_rows(sa, i) * row(2, i) + vcol * row(3, i))
            ro = _dot(jnp.concatenate([new[i] * row(4, i) for i in range(g)], axis=0), j2)
            for i, (rb, _, p) in enumerate(tiles):
                r_i = tile_rows(ro, i)
                acc_ref[rb, p, 0:hd, :] = jnp.where(msk[i], r_i[:, :LANE], acc_ref[rb, p, 0:hd, :])
                acc_ref[rb, p, hd:2 * hd, :] = jnp.where(msk[i], r_i[:, LANE:], acc_ref[rb, p, hd:2 * hd, :])
            s = jnp.concatenate(new, axis=0)
        return s

    assert tseq % step_rows == 0
    if tseq == step_rows:
        return group(0, s0)
    return lax.fori_loop(0, tseq // step_rows, group, s0)


def _rwkv_sample_kernel(rw_ref, ovr_ref, s_in_ref, mu_ref, w0_ref, w2p_ref, a0_ref, a2p_ref, g2_ref, kkp_ref, ka_ref,
                        rk_ref, lnw_ref, lnb_ref, o_ref, s_out_ref,
                        kk_s, w_s, bk_s, kr_s, rr_s, vt_s, acc_s, *, tseq):
    w = GROUP_W
    hd = HEAD_DIM
    ones = _head_ones(w)
    nseq = ROW_BLK // tseq
    rw = rw_ref[0]
    first = (_iota((ROW_BLK, 1), 0) & (tseq - 1)) == 0
    prev = jnp.where(first, ovr_ref[0], pltpu.roll(rw, 1, 0))
    r, log_decay, k_rw, v, kk, bk, g, bonus = _rwkv_prologue(
        rw, prev, mu_ref[...], w0_ref[...], w2p_ref[...], a0_ref[...], a2p_ref[...], g2_ref[...],
        kkp_ref[...], ka_ref[...], rk_ref[...], ones)
    kk_s[...] = kk
    w_s[...] = jnp.exp(log_decay)
    bk_s[...] = bk
    kr_s[...] = k_rw
    rr_s[...] = r
    for p in range(2):
        vt_s[0, p] = v[:, p * LANE:(p + 1) * LANE].T
    acc_s[...] = jnp.zeros_like(acc_s)

    ops = (kk_s, w_s, bk_s, kr_s, rr_s)
    per = 2
    for grp in range(nseq // per):
        tiles = [(0, grp * per + jj, p) for jj in range(per) for p in range(2)]
        rows = pl.ds(grp * per * 2 * hd, per * 2 * hd)
        s_out_ref[rows, :] = _rwkv_scan(tiles, tseq, s_in_ref[rows, :], ops, vt_s, acc_s)

    o = jnp.concatenate([acc_s[0, p].T for p in range(2)], axis=1)
    o_ref[0] = _rwkv_group_norm(o, lnw_ref[...], lnb_ref[...], bonus, g, ones)


def _rwkv_sample(rw, ovr, s_in, prm, tseq):
    w = GROUP_W
    nblocks, _, cw = rw.shape
    nseq = ROW_BLK // tseq
    const = lambda i: (0, 0)
    pspecs = [pl.BlockSpec(a.shape, const) for a in prm]
    blk3 = lambda width: pl.BlockSpec((1, ROW_BLK, width), lambda i: (i, 0, 0))
    st_spec = pl.BlockSpec((nseq * 2 * HEAD_DIM, LANE), lambda i: (i, 0))
    scratch = ([pltpu.VMEM((ROW_BLK, w), F32) for _ in range(5)]
               + [pltpu.VMEM((1, 2, ROW_BLK, LANE), F32), pltpu.VMEM((1, 2, ROW_BLK, LANE), F32)])
    return pl.pallas_call(
        functools.partial(_rwkv_sample_kernel, tseq=tseq),
        grid=(nblocks,),
        in_specs=[blk3(cw), blk3(cw), st_spec] + pspecs,
        out_specs=[blk3(w), st_spec],
        out_shape=[jax.ShapeDtypeStruct((nblocks, ROW_BLK, w), F32), jax.ShapeDtypeStruct(s_in.shape, F32)],
        scratch_shapes=scratch,
        compiler_params=_cparams(("parallel",)),
        name="rwkv_sample",
    )(rw, ovr, s_in, *prm)


ROUTE_W = LANE


def _out_router_kernel(x_ref, oab_ref, oc_ref, od_ref, wo_ref, g2_ref, wr_hi_ref, wr_lo_ref, br_ref,
                       x1_ref, h_ref, gate_ref, *, n_experts, n_groups, null_rows, tm):
    w = GROUP_W
    x1 = (x_ref[...]
          + _dot(oab_ref[...], wo_ref[0:2 * w, :])
          + _dot(oc_ref[...], wo_ref[2 * w:3 * w, :])
          + _dot(od_ref[...], wo_ref[3 * w:4 * w, :]))
    if null_rows:
        rowg = pl.program_id(0) * tm + _iota((tm, 1), 0)
        null = rowg < 0
        for start in null_rows:
            null = null | ((rowg >= start) & (rowg < start + PAD_FRONT))
        x1 = jnp.where(null, 0.0, x1)
    x1_ref[...] = x1
    ms = jnp.mean(x1 * x1, axis=-1, keepdims=True)
    h = x1 * lax.rsqrt(ms + NORM_EPS) * g2_ref[...]
    h_hi = h.astype(BF16)
    h_lo = (h - h_hi.astype(F32)).astype(BF16)
    h_ref[...] = h_hi
    lg = (jnp.dot(h_hi, wr_hi_ref[...], preferred_element_type=F32)
          + jnp.dot(h_lo, wr_hi_ref[...], preferred_element_type=F32)
          + jnp.dot(h_hi, wr_lo_ref[...], preferred_element_type=F32)) + br_ref[...]
    lane = _iota(lg.shape, 1)
    big = jnp.int32(1 << 20)
    neg = jnp.float32(-jnp.inf)
    is_g = (lane >= n_experts) & (lane < n_experts + n_groups)
    gl = jnp.where(is_g, lg, neg)
    gmax = jnp.max(gl, axis=-1, keepdims=True)
    gidx = jnp.min(jnp.where(gl == gmax, lane, big), axis=-1, keepdims=True) - n_experts
    g_w = 1.0 / jnp.sum(jnp.where(is_g, jnp.exp(lg - gmax), 0.0), axis=-1, keepdims=True)
    per = n_experts // n_groups
    in_group = (lane >= gidx * per) & (lane < gidx * per + per)
    el = jnp.where(in_group, lg, neg)
    v1 = jnp.max(el, axis=-1, keepdims=True)
    i1 = jnp.min(jnp.where(el == v1, lane, big), axis=-1, keepdims=True)
    el2 = jnp.where(lane == i1, neg, el)
    v2 = jnp.max(el2, axis=-1, keepdims=True)
    i2 = jnp.min(jnp.where(el2 == v2, lane, big), axis=-1, keepdims=True)
    e21 = jnp.exp(v2 - v1)
    p1 = 1.0 / (1.0 + e21)
    p2 = e21 / (1.0 + e21)
    gate_ref[...] = jnp.where(lane == i1, p1 * g_w, 0.0) + jnp.where(lane == i2, p2 * g_w, 0.0)


def _out_router(x, oab, oc, od, wo, g2, wr_hi, wr_lo, br, n_experts, n_groups, null_rows):
    n, d = x.shape
    w = GROUP_W
    tm = 256 if n % 256 == 0 else ROW_BLK
    row = lambda i: (i, 0)
    const = lambda i: (0, 0)
    return pl.pallas_call(
        functools.partial(_out_router_kernel, n_experts=n_experts, n_groups=n_groups, null_rows=null_rows, tm=tm),
        grid=(n // tm,),
        in_specs=[pl.BlockSpec((tm, d), row), pl.BlockSpec((tm, 2 * w), row), pl.BlockSpec((tm, w), row),
                  pl.BlockSpec((tm, w), row), pl.BlockSpec(wo.shape, const), pl.BlockSpec((1, d), const),
                  pl.BlockSpec(wr_hi.shape, const), pl.BlockSpec(wr_lo.shape, const), pl.BlockSpec((1, ROUTE_W), const)],
        out_specs=[pl.BlockSpec((tm, d), row), pl.BlockSpec((tm, d), row), pl.BlockSpec((tm, ROUTE_W), row)],
        out_shape=[jax.ShapeDtypeStruct((n, d), F32), jax.ShapeDtypeStruct((n, d), BF16),
                   jax.ShapeDtypeStruct((n, ROUTE_W), F32)],
        compiler_params=_cparams(("parallel",)),
        name="out_router",
    )(x, oab, oc, od, wo, g2, wr_hi, wr_lo, br)


MOE_EXPERTS_PER_STEP = 2


def _moe_kernel(x1_ref, h_ref, gate_ref, w1_ref, w3_ref, w2_ref, y_ref):
    step = pl.program_id(1)

    @pl.when(step == 0)
    def _():
        y_ref[...] = x1_ref[...]

    h = h_ref[...]
    gates = gate_ref[...]
    lane = _iota(gates.shape, 1)
    hids = []
    for j in range(MOE_EXPERTS_PER_STEP):
        e = step * MOE_EXPERTS_PER_STEP + j
        ge = jnp.sum(jnp.where(lane == e, gates, 0.0), axis=-1, keepdims=True)
        up = jnp.dot(h, w1_ref[j].astype(BF16), preferred_element_type=F32)
        lin = jnp.dot(h, w3_ref[j].astype(BF16), preferred_element_type=F32)
        hids.append(((up * _sigmoid(up)) * lin * ge).astype(BF16))
    w2 = jnp.concatenate([w2_ref[j].astype(BF16) for j in range(MOE_EXPERTS_PER_STEP)], axis=0)
    y_ref[...] += jnp.dot(jnp.concatenate(hids, axis=1), w2, preferred_element_type=F32)


def _moe_tile(n):
    for t in (1536, 1408, 1280, 1024, 640, 512, 256, 128):
        if n % t == 0:
            return t
    raise ValueError(f"row count {n} is not a multiple of {ROW_BLK}")


def _moe(x1, h, gates, w1, w3, w2, layer):
    n, d = x1.shape
    n_experts, _, f = w1.shape[1:]
    tm = _moe_tile(n)
    eb = MOE_EXPERTS_PER_STEP
    assert n_experts % eb == 0
    row = lambda i, e: (i, 0)
    return pl.pallas_call(
        _moe_kernel,
        grid=(n // tm, n_experts // eb),
        in_specs=[pl.BlockSpec((tm, d), row), pl.BlockSpec((tm, d), row), pl.BlockSpec((tm, ROUTE_W), row),
                  pl.BlockSpec((None, eb, d, f), lambda i, e: (layer, e, 0, 0)),
                  pl.BlockSpec((None, eb, d, f), lambda i, e: (layer, e, 0, 0)),
                  pl.BlockSpec((None, eb, f, d), lambda i, e: (layer, e, 0, 0))],
        out_specs=pl.BlockSpec((tm, d), row),
        out_shape=jax.ShapeDtypeStruct((n, d), F32),
        compiler_params=_cparams(("parallel", "arbitrary")),
        name="moe",
    )(x1, h, gates, w1, w3, w2)


def _rope_tables(pos):
    half = HEAD_DIM // 2
    inv = ROPE_BASE ** (-jnp.arange(half, dtype=F32) / half)
    ang = pos.astype(F32)[:, None] * jnp.tile(inv, LANE // half)[None, :]
    return jnp.cos(ang), jnp.sin(ang)


def _retention_log_gamma():
    lg = jnp.log1p(-jnp.exp2(-5.0 - jnp.arange(N_HG, dtype=F32)))
    return jnp.broadcast_to(jnp.repeat(lg, HEAD_DIM)[None, :], (ROW_BLK, GROUP_W))


def _row(v):
    return v.reshape(1, -1).astype(F32)


def _rwkv_params(p):
    w2 = p['rwkv_w2']
    a2 = p['rwkv_a2']
    w2p = jnp.concatenate([w2, jnp.zeros_like(a2)], axis=0)
    a2p = jnp.concatenate([jnp.zeros_like(w2), a2], axis=0)
    return (_row(p['rwkv_mu']), _row(p['rwkv_w0']), w2p, _row(p['rwkv_a0']), a2p, p['rwkv_g2'],
            _row(p['rwkv_kk']), _row(p['rwkv_ka']), _row(p['rwkv_rk']), _row(p['rwkv_ln_w']), _row(p['rwkv_ln_b']))


def _rwkv_state_to_tiles(s):
    n = s.shape[0]
    return s.reshape(n, 2, 2, HEAD_DIM, HEAD_DIM).transpose(0, 1, 3, 2, 4).reshape(n * 2 * HEAD_DIM, LANE)


def _rwkv_tiles_to_state(t, n):
    return t.reshape(n, 2, HEAD_DIM, 2, HEAD_DIM).transpose(0, 1, 3, 2, 4).reshape(n, N_HG, HEAD_DIM, HEAD_DIM)


def _diag_heads(s):
    return jnp.stack([s[:, h * HEAD_DIM:(h + 1) * HEAD_DIM, h * HEAD_DIM:(h + 1) * HEAD_DIM] for h in range(N_HG)], axis=1)


def kernel(x_prompt, x_sample, cache_sb_k, cache_sb_v, state_ret, state_hgrn, state_rwkv, state_rwkv_shift,
           page_table, meta_tokens, norm1, norm2, w_in, w_out, ret_norm, hgrn_lb_logits, hgrn_norm,
           sb_q_norm, sb_k_norm, sb_bias, rwkv_mu, rwkv_w0, rwkv_w2, rwkv_a0, rwkv_a2, rwkv_g2, rwkv_kk, rwkv_ka,
           rwkv_rk, rwkv_ln_w, rwkv_ln_b, moe_w_group, moe_b_group, moe_w_expert, moe_b_expert,
           moe_w1, moe_w3, moe_w2):
    bp, seq, dm = x_prompt.shape
    db, ds, _ = x_sample.shape
    depth = w_in.shape[0]
    w = GROUP_W
    assert seq % ROW_BLK == 0 and (db * ds) % ROW_BLK == 0 and ROW_BLK % ds == 0 and ds & (ds - 1) == 0
    assert w_in.shape[2] == LIN_COLS + 4 * w and cache_sb_k.shape[3] * cache_sb_k.shape[4] == w
    tp = seq + SB_QBLK
    nblk = tp // ROW_BLK
    tlen = seq + N_META
    past = page_table.shape[1] * cache_sb_k.shape[2]
    n_groups, e_per = moe_w_expert.shape[2:]
    n_experts = n_groups * e_per
    assert n_experts + n_groups <= ROUTE_W

    xp = jnp.concatenate([jnp.zeros((bp, PAD_FRONT, dm), F32),
                          jnp.broadcast_to(meta_tokens[None], (bp, N_META, dm)).astype(F32), x_prompt], axis=1)
    xp = xp.reshape(bp * tp, dm)
    xs = x_sample.reshape(db * ds, dm)
    null_rows = tuple(b * tp for b in range(bp))

    cos_p, sin_p = _rope_tables(jnp.maximum(jnp.arange(tp) - PAD_FRONT, 0))
    cos_s, sin_s = _rope_tables(jnp.tile(past + jnp.arange(ds), ROW_BLK // ds))
    lgam = _retention_log_gamma()
    cache_k = cache_sb_k.reshape(cache_sb_k.shape[:3] + (w,))
    cache_v = cache_sb_v.reshape(cache_sb_v.shape[:3] + (w,))
    tile_heads = lambda v: jnp.tile(v, N_HG)[None, :].astype(F32)

    prompt_rows, sample_rows = [], []
    for l in range(depth):
        wa = w_in[l][:, :LIN_COLS].astype(BF16)
        wb = w_in[l][:, LIN_COLS:].astype(BF16)
        g1 = _row(norm1[l])
        lin_args = (lgam, _row(ret_norm[l]), hgrn_lb_logits.astype(F32), _row(hgrn_norm[l]),
                    tile_heads(sb_q_norm[l]), tile_heads(sb_k_norm[l]))
        rwkv_prm = _rwkv_params(dict(rwkv_mu=rwkv_mu[l], rwkv_w0=rwkv_w0[l], rwkv_w2=rwkv_w2[l], rwkv_a0=rwkv_a0[l],
                                     rwkv_a2=rwkv_a2[l], rwkv_g2=rwkv_g2[l], rwkv_kk=rwkv_kk[l], rwkv_ka=rwkv_ka[l],
                                     rwkv_rk=rwkv_rk[l], rwkv_ln_w=rwkv_ln_w[l], rwkv_ln_b=rwkv_ln_b[l]))
        wo = w_out[l].astype(BF16)
        wr = jnp.zeros((dm, ROUTE_W), F32)
        wr = wr.at[:, :n_experts].set(moe_w_expert[l].reshape(dm, n_experts))
        wr = wr.at[:, n_experts:n_experts + n_groups].set(moe_w_group[l])
        wr_hi = wr.astype(BF16)
        wr_lo = (wr - wr_hi.astype(F32)).astype(BF16)
        br = jnp.zeros((1, ROUTE_W), F32)
        br = br.at[0, :n_experts].set(moe_b_expert[l].reshape(n_experts))
        br = br.at[0, n_experts:n_experts + n_groups].set(moe_b_group[l])
        router = (wo, _row(norm2[l]), wr_hi, wr_lo, br)

        pa, pb = _in_proj(xp, g1, wa, wb)
        oab, qs, knf, knb, vb, s_ret, s_hg = _lin_mix_prompt(pa, cos_p, sin_p, *lin_args, l, bp, nblk)
        oc = _sb_prompt(sb_bias[l].astype(F32), qs, knb, vb, bp, tp)
        od, s_rw = _rwkv_prompt(pb.reshape(bp, tp, -1), rwkv_prm, bp, nblk)
        x1, h2, gates = _out_router(xp, oab, oc, od.reshape(bp * tp, w), *router, n_experts, n_groups, null_rows)
        xp = _moe(x1, h2, gates, moe_w1, moe_w3, moe_w2, l)
        real = lambda a: a.reshape(bp, tp, -1)[:, PAD_FRONT:]
        prompt_rows.append((real(knf).reshape(bp, tlen, N_HG, HEAD_DIM),
                            real(pa[:, LIN_COLS - w:]).reshape(bp, tlen, N_HG, HEAD_DIM),
                            _diag_heads(s_ret), _diag_heads(s_hg), _diag_heads(s_rw).swapaxes(-1, -2),
                            pb.reshape(bp, tp, -1)[:, -1]))

        sa, sb = _in_proj(xs, g1, wa, wb)
        stack = lambda s: s.reshape(db, w, HEAD_DIM)
        oab, qs, knf, vf, s_ret, s_hg = _lin_mix_sample(sa, cos_s, sin_s, *lin_args, stack(state_ret[l]),
                                                        stack(state_hgrn[l]), l, ds)
        oc = _sb_sample(page_table, sb_bias[l].astype(F32), qs, knf, vf, cache_k, cache_v, l, ds)
        cw = sb.shape[-1]
        ovr = jnp.concatenate([state_rwkv_shift[l][:, None, :], jnp.zeros((db, ds - 1, cw), F32)], axis=1)
        od, s_rw = _rwkv_sample(sb.reshape(-1, ROW_BLK, cw), ovr.reshape(-1, ROW_BLK, cw),
                                _rwkv_state_to_tiles(state_rwkv[l]), rwkv_prm, ds)
        x1, h2, gates = _out_router(xs, oab, oc, od.reshape(db * ds, w), *router, n_experts, n_groups, ())
        xs = _moe(x1, h2, gates, moe_w1, moe_w3, moe_w2, l)
        sample_rows.append((knf.reshape(db, ds, N_HG, HEAD_DIM), vf.reshape(db, ds, N_HG, HEAD_DIM),
                            s_ret.reshape(db, N_HG, HEAD_DIM, HEAD_DIM), s_hg.reshape(db, N_HG, HEAD_DIM, HEAD_DIM),
                            _rwkv_tiles_to_state(s_rw, db), sb.reshape(db, ds, cw)[:, -1]))

    y_prompt = xp.reshape(bp, tp, dm)[:, SB_QBLK:]
    y_sample = xs.reshape(db, ds, dm)
    stacked_p = [jnp.stack(r) for r in zip(*prompt_rows)]
    stacked_s = [jnp.stack(r) for r in zip(*sample_rows)]
    return (y_prompt, y_sample, *stacked_p, *stacked_s)
```

```python
import functools
import math

import numpy as np
import jax
import jax.numpy as jnp
from jax import lax
from jax.experimental import pallas as pl
from jax.experimental.pallas import tpu as pltpu

F32 = jnp.float32
BF16 = jnp.bfloat16

HEAD_DIM = 64
N_HG = 4
GROUP_W = N_HG * HEAD_DIM
N_META = 16
LANE = 128
ROW_BLK = 128
SB_QBLK = 256
PAD_FRONT = SB_QBLK - N_META
LIN_COLS = 11 * GROUP_W
NORM_EPS = 1e-6
RWKV_LN_EPS = 64e-5
ROPE_BASE = 10000.0
LOG2E = math.log2(math.e)
SB_SCALE = HEAD_DIM ** -0.5 * LOG2E
E_PER_GROUP = 8
VMEM_LIMIT = 56 * 1024 * 1024


def _iota(shape, dim):
    return lax.broadcasted_iota(jnp.int32, shape, dim)


def _mask01(cond):
    return jnp.where(cond, 1.0, 0.0).astype(BF16)


def _dot(a, b):
    return jnp.dot(a.astype(BF16), b.astype(BF16), preferred_element_type=F32)


def _dot_nt(a, b):
    return lax.dot_general(a.astype(BF16), b.astype(BF16), (((1,), (1,)), ((), ())),
                           preferred_element_type=F32)


def _dot_tn(a, b):
    return lax.dot_general(a.astype(BF16), b.astype(BF16), (((0,), (0,)), ((), ())),
                           preferred_element_type=F32)


def _split(x, n):
    parts = []
    r = x
    for _ in range(n):
        h = r.astype(BF16)
        parts.append(h)
        r = r - h.astype(F32)
    return parts


def _dotx(x, m, n=2):
    out = None
    for p in _split(x, n):
        t = jnp.dot(p, m, preferred_element_type=F32)
        out = t if out is None else out + t
    return out


def _xdot(m, x, n=3):
    out = None
    for p in _split(x, n):
        t = jnp.dot(m, p, preferred_element_type=F32)
        out = t if out is None else out + t
    return out


def _head_ones(w):
    return _mask01((_iota((w, w), 0) >> 6) == (_iota((w, w), 1) >> 6))


def _sigmoid(x):
    return 1.0 / (1.0 + jnp.exp(-x))


def _softplus(x):
    return jnp.maximum(x, 0.0) + jnp.log(1.0 + jnp.exp(-jnp.abs(x)))


def _softplus2(x):
    return jnp.maximum(x, 0.0) + jnp.log2(1.0 + jnp.exp2(-jnp.abs(x)))


def _head_rms(x, gain, ones):
    ms = _dotx(x * x, ones, 2) * (1.0 / HEAD_DIM)
    return x * lax.rsqrt(ms + NORM_EPS) * gain


def _row_tile(n):
    for t in (512, 256, 128):
        if n % t == 0:
            return t
    raise ValueError(f"row count {n} is not a multiple of {ROW_BLK}")


def _cparams(sem, flags=None):
    return pltpu.CompilerParams(dimension_semantics=sem, vmem_limit_bytes=VMEM_LIMIT, flags=flags)


def _in_proj_kernel(x_ref, g_ref, wa_ref, wb_ref, oa_ref, ob_ref):
    x = x_ref[...]
    ms = jnp.mean(x * x, axis=-1, keepdims=True)
    h = (x * lax.rsqrt(ms + NORM_EPS) * g_ref[...]).astype(BF16)
    oa_ref[...] = jnp.dot(h, wa_ref[...], preferred_element_type=F32)
    ob_ref[...] = jnp.dot(h, wb_ref[...], preferred_element_type=F32)


def _in_proj(x, g, wa, wb):
    n, d = x.shape
    tm = 256 if n % 256 == 0 else ROW_BLK
    return pl.pallas_call(
        _in_proj_kernel,
        grid=(n // tm,),
        in_specs=[pl.BlockSpec((tm, d), lambda i: (i, 0)),
                  pl.BlockSpec((1, d), lambda i: (0, 0)),
                  pl.BlockSpec(wa.shape, lambda i: (0, 0)),
                  pl.BlockSpec(wb.shape, lambda i: (0, 0))],
        out_specs=[pl.BlockSpec((tm, wa.shape[1]), lambda i: (i, 0)),
                   pl.BlockSpec((tm, wb.shape[1]), lambda i: (i, 0))],
        out_shape=[jax.ShapeDtypeStruct((n, wa.shape[1]), F32),
                   jax.ShapeDtypeStruct((n, wb.shape[1]), F32)],
        compiler_params=_cparams(("parallel",)),
        name="in_proj",
    )(x, g, wa, wb)


def _drain(steps):
    while True:
        try:
            next(steps)
        except StopIteration as stop:
            return stop.value


def _lockstep(sequences):
    while all([next(seq, False) for seq in sequences]):
        pass


def _gla_block(q, k, v, lw, states, sub, chain):
    rows, w = q.shape
    nsub = rows // sub
    shift = int(math.log2(sub))
    r_i = _iota((rows, rows), 0)
    c_i = _iota((rows, rows), 1)
    same = (r_i >> shift) == (c_i >> shift)
    tri = _mask01(same & (c_i <= r_i))
    blk = _mask01(same)
    sub_of_row = _mask01((_iota((rows, LANE), 0) >> shift) == _iota((rows, LANE), 1))
    bc = bl = dsum = None
    for p in _split(lw, 3):
        t1 = jnp.dot(tri, p, preferred_element_type=F32)
        t2 = jnp.dot(blk, p, preferred_element_type=F32)
        t3 = lax.dot_general(p, sub_of_row, (((0,), (0,)), ((), ())), preferred_element_type=F32)
        bc = t1 if bc is None else bc + t1
        bl = t2 if bl is None else bl + t2
        dsum = t3 if dsum is None else dsum + t3
    dcol = jnp.exp(dsum)
    qt = q * jnp.exp(bc)
    kh = k * jnp.exp(bl - bc)
    ones = _head_ones(w)

    local = _iota((rows, w), 0) & (sub - 1)
    o_intra = jnp.zeros((rows, w), F32)
    for dist in range(sub):
        back = (lambda x: x) if dist == 0 else (lambda x: pltpu.roll(x, dist, 0))
        e = jnp.exp(jnp.where(local >= dist, bc - back(bc), -1e30))
        p = q * back(k) * e
        o_intra = o_intra + _dot(p, ones) * back(v)

    bdmask = (_iota((w, w), 0) >> 6) == (_iota((w, w), 1) >> 6)
    rows_of = lambda i: slice(i * sub, (i + 1) * sub)
    kvs = [jnp.where(bdmask, _dot_tn(kh[rows_of(i)], v[rows_of(i)]), 0.0) for i in range(nsub)]
    if chain:
        outs = []
        s = states
        for i in range(nsub):
            outs.append(_dot(qt[rows_of(i)], s))
            yield True
            s = s * dcol[:, i:i + 1] + kvs[i]
        new_states = s
    else:
        outs = [_dot(qt[rows_of(i)], states[i]) for i in range(nsub)]
        new_states = [states[i] * dcol[:, i:i + 1] + kvs[i] for i in range(nsub)]
    o = o_intra + jnp.concatenate(outs, axis=0)
    return o, new_states


def _retention_block(q, k, v, lgam, s):
    rows, w = q.shape
    width = N_HG * rows
    t_row = _iota((rows, w), 0).astype(F32)
    qd = q * jnp.exp((t_row + 1.0) * lgam)
    kd = k * jnp.exp((rows - 1.0 - t_row) * lgam)
    left = _iota((rows, LANE), 1) < HEAD_DIM

    def split_heads(x):
        x = x.astype(BF16)
        zero = jnp.zeros_like(x)
        return jnp.concatenate([jnp.where(left, x, zero), jnp.where(left, zero, x)], axis=0)

    dist = (_iota((rows, width), 0) - (_iota((rows, width), 1) & (rows - 1))).astype(F32)
    lg_heads = jnp.concatenate([jnp.broadcast_to(lgam[:, h * HEAD_DIM:h * HEAD_DIM + 1], (rows, rows))
                                for h in range(N_HG)], axis=1)
    dmat = jnp.where(dist >= 0.0, jnp.exp(jnp.maximum(dist, 0.0) * lg_heads), 0.0)
    scores = jnp.concatenate([_dot_nt(q[:, p * LANE:(p + 1) * LANE], split_heads(k[:, p * LANE:(p + 1) * LANE]))
                              for p in range(2)], axis=1) * dmat
    o_intra = jnp.concatenate([_dot(scores[:, p * 2 * LANE:(p + 1) * 2 * LANE], split_heads(v[:, p * LANE:(p + 1) * LANE]))
                               for p in range(2)], axis=1)
    o = o_intra + _dot(qd, s)
    ones_rl = jnp.ones((rows, LANE), BF16)
    dsum = None
    for part in _split(lgam, 3):
        t3 = lax.dot_general(part, ones_rl, (((0,), (0,)), ((), ())), preferred_element_type=F32)
        dsum = t3 if dsum is None else dsum + t3
    bdmask = (_iota((w, w), 0) >> 6) == (_iota((w, w), 1) >> 6)
    s = s * jnp.exp(dsum)[:, 0:1] + jnp.where(bdmask, _dot_tn(kd, v), 0.0)
    return o, s


def _lin_mix_math(p, cos, sin, lgam, retw, lb_logits, hgw, qnw, knw, layer, st_ret, st_hg, sub, chain):
    w = GROUP_W
    qa, ka, va, ga, qb, fb, ib, gb, qc, kc, vc = [p[:, i * w:(i + 1) * w] for i in range(11)]
    rows = p.shape[0]
    ones = _head_ones(w)
    lane = _iota((rows, LANE), 1)
    first_half = (lane & (HEAD_DIM - 1)) < (HEAD_DIM // 2)

    def rope(x):
        halves = []
        for hp in range(w // LANE):
            xh = x[:, hp * LANE:(hp + 1) * LANE]
            rot = jnp.where(first_half, -pltpu.roll(xh, LANE - HEAD_DIM // 2, 1), pltpu.roll(xh, HEAD_DIM // 2, 1))
            halves.append(xh * cos + rot * sin)
        return jnp.concatenate(halves, axis=1)

    q_ret = rope(qa)
    k_ret = rope(ka) * (HEAD_DIM ** -0.5)
    if chain:
        o_ret, st_ret = _retention_block(q_ret, k_ret, va, lgam, st_ret)
    else:
        o_ret, st_ret = yield from _gla_block(q_ret, k_ret, va, lgam, st_ret, sub, chain)
    yield True
    o_a = _head_rms(o_ret, retw, ones) * (ga * _sigmoid(ga))

    lg = [lb_logits[d:d + 1, :] for d in range(lb_logits.shape[0])]
    mx = functools.reduce(jnp.maximum, lg)
    ex = [jnp.exp(row - mx) for row in lg]
    lb = sum(ex[1:layer + 1], jnp.zeros_like(mx)) / sum(ex[1:], ex[0])
    log_sig = jnp.minimum(fb, 0.0) - jnp.log1p(jnp.exp(-jnp.abs(fb)))
    t_a = jnp.broadcast_to(jnp.log(lb), fb.shape)
    t_b = jnp.log1p(-lb) + log_sig
    logf = jnp.maximum(t_a, t_b) + jnp.log1p(jnp.exp(-jnp.abs(t_a - t_b)))
    k_hg = (1.0 - lb) * _sigmoid(-fb)
    yield True
    o_hg, st_hg = yield from _gla_block(qb, k_hg, ib, logf, st_hg, sub, chain)
    o_b = _head_rms(o_hg, hgw, ones) * (gb * _sigmoid(gb))

    qn = _head_rms(qc, qnw, ones)
    kn = _head_rms(kc, knw, ones)
    return o_a, o_b, qn, kn, vc, st_ret, st_hg


def _lin_mix_prompt_kernel(p_ref, cos_ref, sin_ref, lgam_ref, retw_ref, lbl_ref, hgw_ref, qnw_ref, knw_ref,
                           oab_ref, qs_ref, knf_ref, knb_ref, vb_ref, sret_ref, shg_ref,
                           st_ret, st_hg, *, layer, sub, nseq):
    i = pl.program_id(0)

    def sequence(b):
        @pl.when(i == 0)
        def _():
            st_ret[b] = jnp.zeros(st_ret.shape[1:], F32)
            st_hg[b] = jnp.zeros(st_hg.shape[1:], F32)

        o_a, o_b, qn, kn, vc, s1, s2 = yield from _lin_mix_math(
            p_ref[b], cos_ref[...], sin_ref[...], lgam_ref[...], retw_ref[...], lbl_ref, hgw_ref[...],
            qnw_ref[...], knw_ref[...], layer, st_ret[b], st_hg[b], sub, True)
        st_ret[b] = s1
        st_hg[b] = s2
        oab_ref[b] = jnp.concatenate([o_a, o_b], axis=1)
        qs_ref[b] = (qn * SB_SCALE).astype(BF16)
        knf_ref[b] = kn
        knb_ref[b] = kn.astype(BF16)
        vb_ref[b] = vc.astype(BF16)

        @pl.when(i == pl.num_programs(0) - 1)
        def _():
            sret_ref[b] = s1
            shg_ref[b] = s2

    _lockstep([sequence(b) for b in range(nseq)])


def _lin_mix_prompt(proj, cos, sin, lgam, retw, lbl, hgw, qnw, knw, layer, nb, nblk):
    n = proj.shape[0]
    tp = n // nb
    w = GROUP_W
    blk = lambda i: (0, i, 0)
    const = lambda i: (0, 0)
    whole = lambda i: (0, 0, 0)
    outs = pl.pallas_call(
        functools.partial(_lin_mix_prompt_kernel, layer=layer, sub=16, nseq=nb),
        grid=(nblk,),
        in_specs=[pl.BlockSpec((nb, ROW_BLK, LIN_COLS), blk),
                  pl.BlockSpec((ROW_BLK, LANE), lambda i: (i, 0)),
                  pl.BlockSpec((ROW_BLK, LANE), lambda i: (i, 0)),
                  pl.BlockSpec((ROW_BLK, w), const), pl.BlockSpec((1, w), const),
                  pl.BlockSpec(lbl.shape, const), pl.BlockSpec((1, w), const),
                  pl.BlockSpec((1, w), const), pl.BlockSpec((1, w), const)],
        out_specs=[pl.BlockSpec((nb, ROW_BLK, 2 * w), blk),
                   pl.BlockSpec((nb, ROW_BLK, w), blk), pl.BlockSpec((nb, ROW_BLK, w), blk),
                   pl.BlockSpec((nb, ROW_BLK, w), blk), pl.BlockSpec((nb, ROW_BLK, w), blk),
                   pl.BlockSpec((nb, w, w), whole), pl.BlockSpec((nb, w, w), whole)],
        out_shape=[jax.ShapeDtypeStruct((nb, tp, 2 * w), F32),
                   jax.ShapeDtypeStruct((nb, tp, w), BF16), jax.ShapeDtypeStruct((nb, tp, w), F32),
                   jax.ShapeDtypeStruct((nb, tp, w), BF16), jax.ShapeDtypeStruct((nb, tp, w), BF16),
                   jax.ShapeDtypeStruct((nb, w, w), F32), jax.ShapeDtypeStruct((nb, w, w), F32)],
        scratch_shapes=[pltpu.VMEM((nb, w, w), F32), pltpu.VMEM((nb, w, w), F32)],
        compiler_params=_cparams(("arbitrary",)),
        name="lin_mix_prompt",
    )(proj.reshape(nb, tp, LIN_COLS), cos, sin, lgam, retw, lbl, hgw, qnw, knw)
    return [o.reshape(n, o.shape[-1]) for o in outs[:5]] + list(outs[5:])


def _lin_mix_sample_kernel(p_ref, cos_ref, sin_ref, lgam_ref, retw_ref, lbl_ref, hgw_ref, qnw_ref, knw_ref,
                           sret_in, shg_in,
                           oab_ref, qs_ref, knf_ref, vf_ref, sret_out, shg_out, *, layer, sub):
    w = GROUP_W
    nseq = ROW_BLK // sub
    bdmask = (_iota((w, w), 0) >> 6) == (_iota((w, w), 1) >> 6)
    rep = _mask01(_iota((HEAD_DIM, w), 0) == (_iota((HEAD_DIM, w), 1) & (HEAD_DIM - 1)))
    rep_t = _mask01((_iota((w, HEAD_DIM), 0) & (HEAD_DIM - 1)) == _iota((w, HEAD_DIM), 1))

    def expand(ref):
        return [jnp.where(bdmask, _dotx(ref[j], rep, 3), 0.0) for j in range(nseq)]

    def extract(ref, states):
        for j in range(nseq):
            ref[j] = _dotx(states[j], rep_t, 3)

    o_a, o_b, qn, kn, vc, s1, s2 = _drain(_lin_mix_math(
        p_ref[...], cos_ref[...], sin_ref[...], lgam_ref[...], retw_ref[...], lbl_ref, hgw_ref[...],
        qnw_ref[...], knw_ref[...], layer, expand(sret_in), expand(shg_in), sub, False))
    extract(sret_out, s1)
    extract(shg_out, s2)
    oab_ref[...] = jnp.concatenate([o_a, o_b], axis=1)
    qs_ref[...] = qn * SB_SCALE
    knf_ref[...] = kn
    vf_ref[...] = vc


def _lin_mix_sample(proj, cos, sin, lgam, retw, lbl, hgw, qnw, knw, s_ret, s_hg, layer, dec_seq):
    n = proj.shape[0]
    w = GROUP_W
    nseq = ROW_BLK // dec_seq
    row = lambda i: (i, 0)
    const = lambda i: (0, 0)
    st_spec = pl.BlockSpec((nseq, w, HEAD_DIM), lambda i: (i, 0, 0))
    return pl.pallas_call(
        functools.partial(_lin_mix_sample_kernel, layer=layer, sub=dec_seq),
        grid=(n // ROW_BLK,),
        in_specs=[pl.BlockSpec((ROW_BLK, LIN_COLS), row),
                  pl.BlockSpec((ROW_BLK, LANE), const), pl.BlockSpec((ROW_BLK, LANE), const),
                  pl.BlockSpec((ROW_BLK, w), const), pl.BlockSpec((1, w), const),
                  pl.BlockSpec(lbl.shape, const), pl.BlockSpec((1, w), const),
                  pl.BlockSpec((1, w), const), pl.BlockSpec((1, w), const),
                  st_spec, st_spec],
        out_specs=[pl.BlockSpec((ROW_BLK, 2 * w), row),
                   pl.BlockSpec((ROW_BLK, w), row), pl.BlockSpec((ROW_BLK, w), row),
                   pl.BlockSpec((ROW_BLK, w), row), st_spec, st_spec],
        out_shape=[jax.ShapeDtypeStruct((n, 2 * w), F32),
                   jax.ShapeDtypeStruct((n, w), F32), jax.ShapeDtypeStruct((n, w), F32),
                   jax.ShapeDtypeStruct((n, w), F32),
                   jax.ShapeDtypeStruct(s_ret.shape, F32), jax.ShapeDtypeStruct(s_hg.shape, F32)],
        compiler_params=_cparams(("parallel",)),
        name="lin_mix_sample",
    )(proj, cos, sin, lgam, retw, lbl, hgw, qnw, knw, s_ret, s_hg)


def _sb_consts(rows, tmask):
    j_i = _iota((LANE, 2 * LANE), 0)
    s_i = _iota((LANE, 2 * LANE), 1)
    ucat = _mask01((s_i >= LANE) | (j_i >= s_i))
    causal = _iota((rows, LANE), 1) < (_iota((rows, LANE), 0) & tmask)
    return ucat, causal


def _sb_prompt_kernel(bias_ref, q_ref, k_ref, v_ref, o_ref):
    i = pl.program_id(1)
    qb = SB_QBLK
    kpq = qb // ROW_BLK
    q = q_ref[...]
    left = _iota((ROW_BLK, LANE), 1) < HEAD_DIM
    zero = jnp.zeros((ROW_BLK, LANE), BF16)
    width = N_HG * ROW_BLK
    lane = _iota((qb, width), 1)
    bias_row = jnp.zeros((qb, width), F32)
    for h in range(N_HG):
        bias_row = jnp.where((lane >> 7) == h, bias_ref[h] * LOG2E, bias_row)
    sk = lane & (ROW_BLK - 1)
    tq = _iota((qb, width), 0)
    r2 = _iota((2 * LANE, 2 * LANE), 0)
    c2 = _iota((2 * LANE, 2 * LANE), 1)
    ubd = _mask01(((r2 >> 7) == (c2 >> 7)) & ((r2 & (LANE - 1)) > (c2 & (LANE - 1))))

    def split_heads(x):
        return jnp.concatenate([jnp.where(left, x, zero), jnp.where(left, zero, x)], axis=0)

    pair = lambda x, p: x[:, p * 2 * LANE:(p + 1) * 2 * LANE]

    def steps(kbs, masks, carry, acc0, acc1):
        blocks = []
        for kb in kbs:
            start = pl.multiple_of(kb * ROW_BLK, ROW_BLK)
            blocks.append((k_ref[pl.ds(start, ROW_BLK), :], v_ref[pl.ds(start, ROW_BLK), :]))
        zs = [jnp.concatenate([_dot_nt(q[:, p * LANE:(p + 1) * LANE], split_heads(kblk[:, p * LANE:(p + 1) * LANE]))
                               for p in range(2)], axis=1) + bias_row for kblk, _ in blocks]
        sp_own = [_softplus2(z) for z in zs]
        sps = [s if m is None else jnp.where(m, s, 0.0) for s, m in zip(sp_own, masks)]
        ts = []
        for sp in sps:
            spb = sp.astype(BF16)
            ts.append(jnp.concatenate([jnp.dot(pair(spb, p), ubd, preferred_element_type=F32) for p in range(2)], axis=1))
        abs_ = []
        for z, own, sp, t, m in zip(zs, sp_own, sps, ts, masks):
            a = jnp.exp2((z - own) - (carry + t))
            if m is not None:
                a = jnp.where(m, a, 0.0)
            abs_.append(a.astype(BF16))
            carry = carry + jnp.concatenate(
                [jnp.broadcast_to(jnp.sum(sp[:, h * ROW_BLK:(h + 1) * ROW_BLK], axis=-1, keepdims=True), (qb, ROW_BLK))
                 for h in range(N_HG)], axis=1)
        accs = [acc0, acc1]
        for ab, (_, vblk) in zip(abs_, blocks):
            for p in range(2):
                accs[p] = accs[p] + jnp.dot(pair(ab, p), split_heads(vblk[:, p * LANE:(p + 1) * LANE]),
                                            preferred_element_type=F32)
        return carry, accs[0], accs[1]

    zeros = jnp.zeros((qb, LANE), F32)
    state = (jnp.zeros((qb, width), F32), zeros, zeros)
    state = steps([i * kpq + d for d in reversed(range(kpq))], [sk + d * ROW_BLK < tq for d in reversed(range(kpq))],
                  *state)

    def earlier_blocks(j, c):
        return steps([(i - j) * kpq - 1 - d for d in range(kpq)], [None] * kpq, *c)

    state = lax.fori_loop(0, i, earlier_blocks, state)
    o_ref[...] = jnp.concatenate([state[1], state[2]], axis=1)


def _sb_prompt(bias, qs, kb, vb, nb, tp):
    n, w = qs.shape
    nq = tp // SB_QBLK
    return pl.pallas_call(
        _sb_prompt_kernel,
        grid=(nb, nq),
        in_specs=[pl.BlockSpec(memory_space=pltpu.SMEM),
                  pl.BlockSpec((SB_QBLK, w), lambda b, i: (b * nq + i, 0)),
                  pl.BlockSpec((tp, w), lambda b, i: (b, 0)),
                  pl.BlockSpec((tp, w), lambda b, i: (b, 0))],
        out_specs=pl.BlockSpec((SB_QBLK, w), lambda b, i: (b * nq + i, 0)),
        out_shape=jax.ShapeDtypeStruct((n, w), F32),
        compiler_params=_cparams(("parallel", "arbitrary")),
        name="sb_prompt",
    )(bias, qs, kb, vb)


def _sb_sample_kernel(pt_ref, bias_ref, q_ref, kn_ref, vn_ref, *rest, n_pages, dec_seq):
    del pt_ref
    k_pages = rest[:n_pages]
    v_pages = rest[n_pages:2 * n_pages]
    o_ref = rest[2 * n_pages]
    w = GROUP_W
    rows = N_HG * dec_seq
    q = q_ref[...]
    head_of_lane = _iota((dec_seq, w), 1) >> 6
    qbd = jnp.concatenate([jnp.where(head_of_lane == h, q, 0.0) for h in range(N_HG)], axis=0).astype(BF16)
    bias_col = jnp.concatenate([jnp.full((dec_seq, 1), bias_ref[h] * LOG2E, F32) for h in range(N_HG)], axis=0)
    ucat, causal = _sb_consts(rows, dec_seq - 1)
    pad = jnp.zeros((LANE - dec_seq, w), F32)
    k_new = jnp.concatenate([kn_ref[...], pad], axis=0)
    v_new = jnp.concatenate([vn_ref[...], pad], axis=0)
    key_blocks = [k_new] + [k_pages[p][...] for p in reversed(range(n_pages))]
    val_blocks = [v_new] + [v_pages[p][...] for p in reversed(range(n_pages))]
    zs = [_dot_nt(qbd, kb) + bias_col for kb in key_blocks]
    sps = [_softplus2(z) for z in zs]
    sps[0] = jnp.where(causal, sps[0], 0.0)
    ts = [_dotx(sp, ucat, 2) for sp in sps]
    carry = jnp.zeros((rows, LANE), F32)
    weights = []
    for z, t in zip(zs, ts):
        weights.append(jnp.exp2(z - (carry + t[:, :LANE])))
        carry = carry + t[:, LANE:]
    weights[0] = jnp.where(causal, weights[0], 0.0)
    acc = None
    for a, vb in zip(weights, val_blocks):
        term = _dot(a, vb)
        acc = term if acc is None else acc + term
    out = jnp.zeros((dec_seq, w), F32)
    for h in range(N_HG):
        out = out + jnp.where(head_of_lane == h, acc[h * dec_seq:(h + 1) * dec_seq], 0.0)
    o_ref[...] = out


def _sb_sample(page_table, bias, qs, kn, vn, cache_k, cache_v, layer, dec_seq):
    n, w = qs.shape
    db, n_pages = page_table.shape
    page = cache_k.shape[2]
    assert page == LANE and dec_seq % 8 == 0
    row = lambda b, pt: (b, 0)
    page_specs = [pl.BlockSpec((None, None, page, w), lambda b, pt, p=p: (layer, pt[b, p], 0, 0))
                  for p in range(n_pages)]
    grid_spec = pltpu.PrefetchScalarGridSpec(
        num_scalar_prefetch=1,
        grid=(db,),
        in_specs=[pl.BlockSpec(memory_space=pltpu.SMEM),
                  pl.BlockSpec((dec_seq, w), row), pl.BlockSpec((dec_seq, w), row), pl.BlockSpec((dec_seq, w), row)]
                 + page_specs + page_specs,
        out_specs=pl.BlockSpec((dec_seq, w), row),
    )
    return pl.pallas_call(
        functools.partial(_sb_sample_kernel, n_pages=n_pages, dec_seq=dec_seq),
        grid_spec=grid_spec,
        out_shape=jax.ShapeDtypeStruct((n, w), F32),
        compiler_params=_cparams(("arbitrary",)),
        name="sb_sample",
    )(page_table, bias, qs, kn, vn, *([cache_k] * n_pages), *([cache_v] * n_pages))


def _rwkv_prologue(rw, prev, mu, w0, w2p, a0, a2p, g2, kkp, ka, rk, ones):
    w = GROUP_W
    xm = rw + (prev - rw) * mu
    r = xm[:, 0:w]
    k = xm[:, w:2 * w]
    v = xm[:, 2 * w:3 * w]
    wa = xm[:, 3 * w:3 * w + LANE]
    gl = xm[:, 3 * w + LANE:]
    wd = w0 + _dot(jnp.tanh(wa), w2p)
    log_decay = -jnp.exp(-_softplus(-wd) - 0.5)
    a = _sigmoid(a0 + _dot(wa, a2p))
    g = _dot(_sigmoid(gl), g2)
    kk = k * kkp
    kk = kk * lax.rsqrt(jnp.maximum(_dotx(kk * kk, ones, 2), 1e-12))
    k_rw = k * (1.0 + (a - 1.0) * ka)
    bonus = _dotx(r * k_rw * rk, ones, 2) * v
    return r, log_decay, k_rw, v, kk, kk * a, g, bonus


RWKV_SUB = 64


def _rwkv_group_norm(o, lnw, lnb, bonus, g, ones):
    mu_h = _dotx(o, ones, 2) * (1.0 / HEAD_DIM)
    dlt = o - mu_h
    var = _dotx(dlt * dlt, ones, 2) * (1.0 / HEAD_DIM)
    return (dlt * lax.rsqrt(var + RWKV_LN_EPS) * lnw + lnb + bonus) * g


def _rwkv_chunk_kernel(rw_ref, mu_ref, w0_ref, w2p_ref, a0_ref, a2p_ref, g2_ref, kkp_ref, ka_ref, rk_ref, lnw_ref, lnb_ref,
                       o_ref, s_out_ref, carry_s, state_s, *, sub, nseq):
    prm = (mu_ref, w0_ref, w2p_ref, a0_ref, a2p_ref, g2_ref, kkp_ref, ka_ref, rk_ref, lnw_ref, lnb_ref)
    _lockstep([_rwkv_chunk_sequence(q, rw_ref, prm, o_ref, s_out_ref, carry_s, state_s, sub) for q in range(nseq)])


def _rwkv_chunk_sequence(seq, rw_ref, prm, o_ref, s_out_ref, carry_s, state_s, sub):
    mu_ref, w0_ref, w2p_ref, a0_ref, a2p_ref, g2_ref, kkp_ref, ka_ref, rk_ref, lnw_ref, lnb_ref = prm
    i = pl.program_id(0)
    w = GROUP_W
    rows = ROW_BLK
    nsub = rows // sub
    shift = int(math.log2(sub))
    ones = _head_ones(w)

    @pl.when(i == 0)
    def _():
        carry_s[seq] = jnp.zeros(carry_s.shape[1:], F32)
        state_s[seq] = jnp.zeros(state_s.shape[1:], F32)

    rw = rw_ref[seq]
    prev = jnp.where(_iota((rows, 1), 0) == 0, carry_s[seq, 0:1, :], pltpu.roll(rw, 1, 0))
    carry_s[seq, 0:1, :] = rw[rows - 1:rows, :]
    r, lw, k, v, kk, b, g, bonus = _rwkv_prologue(
        rw, prev, mu_ref[...], w0_ref[...], w2p_ref[...], a0_ref[...], a2p_ref[...], g2_ref[...],
        kkp_ref[...], ka_ref[...], rk_ref[...], ones)
    yield True

    r_i = _iota((rows, rows), 0)
    c_i = _iota((rows, rows), 1)
    same = (r_i >> shift) == (c_i >> shift)
    tri = _mask01(same & (c_i <= r_i))
    blk = _mask01(same)
    sub_of_row = _mask01((_iota((rows, LANE), 0) >> shift) == _iota((rows, LANE), 1))
    c = cl = dsum = None
    for part in _split(lw, 3):
        t1 = jnp.dot(tri, part, preferred_element_type=F32)
        t2 = jnp.dot(blk, part, preferred_element_type=F32)
        t3 = lax.dot_general(part, sub_of_row, (((0,), (0,)), ((), ())), preferred_element_type=F32)
        c = t1 if c is None else c + t1
        cl = t2 if cl is None else cl + t2
        dsum = t3 if dsum is None else dsum + t3
    yield True
    dcol = jnp.exp(dsum)
    kkd = kk * jnp.exp(c - lw)
    rd = r * jnp.exp(c)
    grow = jnp.exp(-c)
    kt = k * grow
    bt = b * grow
    tail = jnp.exp(cl - c)
    khat = k * tail
    bhat = b * tail

    left = _iota((rows, LANE), 1) < HEAD_DIM

    def split_heads(x):
        zero = jnp.zeros_like(x)
        return jnp.concatenate([jnp.where(left, x, zero), jnp.where(left, zero, x)], axis=0)

    lanes_p = lambda x, p: x[:, p * LANE:(p + 1) * LANE]
    wide = 2 * LANE
    t_i = _iota((rows, wide), 0)
    s_i = _iota((rows, wide), 1) & (rows - 1)
    same_sub = (t_i >> shift) == (s_i >> shift)
    incl = same_sub & (s_i <= t_i)
    strict = same_sub & (s_i < t_i)

    def dot3(x, y_hi, y_lo):
        x_hi, x_lo = _split(x, 2)
        nt = lambda a_, b_: lax.dot_general(a_, b_, (((1,), (1,)), ((), ())), preferred_element_type=F32)
        return nt(x_hi, y_hi) + nt(x_hi, y_lo) + nt(x_lo, y_hi)

    o_parts, y_parts, abr, abk = [], [], [], []
    for p in range(2):
        keys = jnp.concatenate([split_heads(lanes_p(kt, p)), split_heads(lanes_p(bt, p))], axis=0)
        keys_hi, keys_lo = _split(keys, 2)
        c_r = lax.dot_general(lanes_p(rd, p).astype(BF16), keys_hi, (((1,), (1,)), ((), ())),
                              preferred_element_type=F32)
        c_kk = dot3(lanes_p(kkd, p), keys_hi, keys_lo)
        akr = jnp.where(incl, c_r[:, :wide], 0.0)
        abr.append(jnp.where(incl, c_r[:, wide:], 0.0))
        akk = jnp.where(strict, c_kk[:, :wide], 0.0)
        abk.append(jnp.where(strict, c_kk[:, wide:], 0.0))
        v_heads = split_heads(lanes_p(v, p)).astype(BF16)
        o_parts.append(jnp.dot(akr.astype(BF16), v_heads, preferred_element_type=F32))
        y_parts.append(_dotx(akk, v_heads, 2))
    yield True
    o_acc = jnp.concatenate(o_parts, axis=1)
    y_intra = jnp.concatenate(y_parts, axis=1)

    def mm3(x, y):
        x_hi, x_lo = _split(x, 2)
        y_hi, y_lo = _split(y, 2)
        mm = lambda a_, b_: jnp.dot(a_, b_, preferred_element_type=F32)
        return mm(x_hi, y_hi) + mm(x_hi, y_lo) + mm(x_lo, y_hi)

    head_of_lane = _iota((rows, w), 1) >> 6
    eye = (_iota((rows, rows), 0) == _iota((rows, rows), 1)).astype(F32)
    powers = [-abk[h // 2][:, (h % 2) * rows:(h % 2 + 1) * rows] for h in range(N_HG)]
    invs = [eye + n for n in powers]
    for _ in range(shift - 1):
        powers = [mm3(n, n) for n in powers]
        yield True
        invs = [t + mm3(t, n) for t, n in zip(invs, powers)]
    yield True
    u_intra = jnp.zeros((rows, w), F32)
    g_mat = jnp.zeros((rows, w), F32)
    for h in range(N_HG):
        u_intra = u_intra + mm3(invs[h], jnp.where(head_of_lane == h, y_intra, 0.0))
        g_mat = g_mat + mm3(invs[h], jnp.where(head_of_lane == h, kkd, 0.0))
    yield True

    bdmask = (_iota((w, w), 0) >> 6) == (_iota((w, w), 1) >> 6)
    m = state_s[seq]
    us = []
    o_state = []
    for j in range(nsub):
        sl = slice(j * sub, (j + 1) * sub)
        from_state = _dot(jnp.concatenate([g_mat[sl], rd[sl]], axis=0), m)
        yield True
        u = u_intra[sl] + from_state[:sub]
        o_state.append(from_state[sub:])
        us.append(u)
        kv = _dot_tn(jnp.concatenate([khat[sl], -bhat[sl]], axis=0), jnp.concatenate([v[sl], u], axis=0))
        yield True
        m = m * dcol[:, j:j + 1] + jnp.where(bdmask, kv, 0.0)
    state_s[seq] = m
    u_all = jnp.concatenate(us, axis=0)
    o_u = jnp.concatenate([jnp.dot(abr[p].astype(BF16), split_heads(lanes_p(u_all, p)).astype(BF16),
                                   preferred_element_type=F32) for p in range(2)], axis=1)
    yield True
    o = o_acc + jnp.concatenate(o_state, axis=0) - o_u
    o_ref[seq] = _rwkv_group_norm(o, lnw_ref[...], lnb_ref[...], bonus, g, ones)

    @pl.when(i == pl.num_programs(0) - 1)
    def _():
        s_out_ref[seq] = m


def _rwkv_prompt(rw, prm, nb, nblk):
    w = GROUP_W
    cw = rw.shape[-1]
    const = lambda i: (0, 0)
    blk = lambda i: (0, i, 0)
    return pl.pallas_call(
        functools.partial(_rwkv_chunk_kernel, sub=RWKV_SUB, nseq=nb),
        grid=(nblk,),
        in_specs=[pl.BlockSpec((nb, ROW_BLK, cw), blk)] + [pl.BlockSpec(a.shape, const) for a in prm],
        out_specs=[pl.BlockSpec((nb, ROW_BLK, w), blk), pl.BlockSpec((nb, w, w), lambda i: (0, 0, 0))],
        out_shape=[jax.ShapeDtypeStruct((nb, nblk * ROW_BLK, w), F32), jax.ShapeDtypeStruct((nb, w, w), F32)],
        scratch_shapes=[pltpu.VMEM((nb, 8, cw), F32), pltpu.VMEM((nb, w, w), F32)],
        compiler_params=_cparams(("arbitrary",)),
        name="rwkv_prompt",
    )(rw, *prm)


def _rwkv_scan(tiles, tseq, s0, op_refs, vt_ref, acc_ref):
    g = len(tiles)
    hd = HEAD_DIM
    kk_ref, w_ref, bk_ref, kr_ref, rr_ref = op_refs
    ones_h = _head_ones(LANE)
    ones_f = jnp.ones((LANE, LANE), BF16)
    j2 = _mask01((_iota((LANE, 2 * LANE), 0) >> 6) == (_iota((LANE, 2 * LANE), 1) >> 7))
    lane_t = _iota((hd, LANE), 1)
    left = lane_t < hd
    x0 = [vt_ref[rb, p, 0:hd, :] for (rb, _, p) in tiles]
    x1 = [vt_ref[rb, p, hd:2 * hd, :] for (rb, _, p) in tiles]
    tile_rows = lambda a, i: a[i * hd:(i + 1) * hd]
    step_rows = 8

    def group(t8, s):
        base = t8 * step_rows
        blks = []
        for ref in op_refs:
            per_tile = []
            for (rb, j, p) in tiles:
                start = rb * ROW_BLK + j * tseq + base
                if not isinstance(start, int):
                    start = pl.multiple_of(start, step_rows)
                per_tile.append(ref[pl.ds(start, step_rows), pl.ds(p * LANE, LANE)])
            blks.append(per_tile)
        for u in range(step_rows):
            row = lambda q, i: blks[q][i][u:u + 1, :]
            s_t = [tile_rows(s, i) for i in range(g)]
            sa = _dotx(jnp.concatenate([s_t[i] * row(0, i) for i in range(g)], axis=0), ones_h, 2)
            msk = [lane_t == (j * tseq + base + u) for (_, j, _) in tiles]
            vsel = jnp.concatenate([jnp.where(msk[i], x0[i], 0.0) for i in range(g)]
                                   + [jnp.where(msk[i], x1[i], 0.0) for i in range(g)], axis=0)
            vc = _dotx(vsel, ones_f, 2)
            new = []
            for i in range(g):
                vcol = jnp.where(left, tile_rows(vc, i), tile_rows(vc, g + i))
                new.append(s_t[i] * row(1, i) - tile_rows(sa, i) * row(2, i) + vcol * row(3, i))
            ro = _dot(jnp.concatenate([new[i] * row(4, i) for i in range(g)], axis=0), j2)
            for i, (rb, _, p) in enumerate(tiles):
                r_i = tile_rows(ro, i)
                acc_ref[rb, p, 0:hd, :] = jnp.where(msk[i], r_i[:, :LANE], acc_ref[rb, p, 0:hd, :])
                acc_ref[rb, p, hd:2 * hd, :] = jnp.where(msk[i], r_i[:, LANE:], acc_ref[rb, p, hd:2 * hd, :])
            s = jnp.concatenate(new, axis=0)
        return s

    assert tseq % step_rows == 0
    if tseq == step_rows:
        return group(0, s0)
    return lax.fori_loop(0, tseq // step_rows, group, s0)


def _rwkv_sample_kernel(rw_ref, ovr_ref, s_in_ref, mu_ref, w0_ref, w2p_ref, a0_ref, a2p_ref, g2_ref, kkp_ref, ka_ref,
                        rk_ref, lnw_ref, lnb_ref, o_ref, s_out_ref,
                        kk_s, w_s, bk_s, kr_s, rr_s, vt_s, acc_s, *, tseq):
    w = GROUP_W
    hd = HEAD_DIM
    ones = _head_ones(w)
    nseq = ROW_BLK // tseq
    rw = rw_ref[0]
    first = (_iota((ROW_BLK, 1), 0) & (tseq - 1)) == 0
    prev = jnp.where(first, ovr_ref[0], pltpu.roll(rw, 1, 0))
    r, log_decay, k_rw, v, kk, bk, g, bonus = _rwkv_prologue(
        rw, prev, mu_ref[...], w0_ref[...], w2p_ref[...], a0_ref[...], a2p_ref[...], g2_ref[...],
        kkp_ref[...], ka_ref[...], rk_ref[...], ones)
    kk_s[...] = kk
    w_s[...] = jnp.exp(log_decay)
    bk_s[...] = bk
    kr_s[...] = k_rw
    rr_s[...] = r
    for p in range(2):
        vt_s[0, p] = v[:, p * LANE:(p + 1) * LANE].T
    acc_s[...] = jnp.zeros_like(acc_s)

    ops = (kk_s, w_s, bk_s, kr_s, rr_s)
    per = 8
    for grp in range(nseq // per):
        tiles = [(0, grp * per + jj, p) for jj in range(per) for p in range(2)]
        rows = pl.ds(grp * per * 2 * hd, per * 2 * hd)
        s_out_ref[rows, :] = _rwkv_scan(tiles, tseq, s_in_ref[rows, :], ops, vt_s, acc_s)

    o = jnp.concatenate([acc_s[0, p].T for p in range(2)], axis=1)
    o_ref[0] = _rwkv_group_norm(o, lnw_ref[...], lnb_ref[...], bonus, g, ones)


def _rwkv_sample(rw, ovr, s_in, prm, tseq):
    w = GROUP_W
    nblocks, _, cw = rw.shape
    nseq = ROW_BLK // tseq
    const = lambda i: (0, 0)
    pspecs = [pl.BlockSpec(a.shape, const) for a in prm]
    blk3 = lambda width: pl.BlockSpec((1, ROW_BLK, width), lambda i: (i, 0, 0))
    st_spec = pl.BlockSpec((nseq * 2 * HEAD_DIM, LANE), lambda i: (i, 0))
    scratch = ([pltpu.VMEM((ROW_BLK, w), F32) for _ in range(5)]
               + [pltpu.VMEM((1, 2, ROW_BLK, LANE), F32), pltpu.VMEM((1, 2, ROW_BLK, LANE), F32)])
    return pl.pallas_call(
        functools.partial(_rwkv_sample_kernel, tseq=tseq),
        grid=(nblocks,),
        in_specs=[blk3(cw), blk3(cw), st_spec] + pspecs,
        out_specs=[blk3(w), st_spec],
        out_shape=[jax.ShapeDtypeStruct((nblocks, ROW_BLK, w), F32), jax.ShapeDtypeStruct(s_in.shape, F32)],
        scratch_shapes=scratch,
        compiler_params=_cparams(("parallel",)),
        name="rwkv_sample",
    )(rw, ovr, s_in, *prm)


ROUTE_W = LANE


def _out_router_kernel(x_ref, oab_ref, oc_ref, od_ref, wo_ref, g2_ref, wr_hi_ref, wr_lo_ref, br_ref,
                       x1_ref, h_ref, gate_ref, *, n_experts, n_groups, null_rows, tm):
    w = GROUP_W
    x1 = (x_ref[...]
          + _dot(oab_ref[...], wo_ref[0:2 * w, :])
          + _dot(oc_ref[...], wo_ref[2 * w:3 * w, :])
          + _dot(od_ref[...], wo_ref[3 * w:4 * w, :]))
    if null_rows:
        rowg = pl.program_id(0) * tm + _iota((tm, 1), 0)
        null = rowg < 0
        for start in null_rows:
            null = null | ((rowg >= start) & (rowg < start + PAD_FRONT))
        x1 = jnp.where(null, 0.0, x1)
    x1_ref[...] = x1
    ms = jnp.mean(x1 * x1, axis=-1, keepdims=True)
    h = x1 * lax.rsqrt(ms + NORM_EPS) * g2_ref[...]
    h_hi = h.astype(BF16)
    h_lo = (h - h_hi.astype(F32)).astype(BF16)
    h_ref[...] = h_hi
    lg = (jnp.dot(h_hi, wr_hi_ref[...], preferred_element_type=F32)
          + jnp.dot(h_lo, wr_hi_ref[...], preferred_element_type=F32)
          + jnp.dot(h_hi, wr_lo_ref[...], preferred_element_type=F32)) + br_ref[...]
    lane = _iota(lg.shape, 1)
    big = jnp.int32(1 << 20)
    neg = jnp.float32(-jnp.inf)
    is_g = (lane >= n_experts) & (lane < n_experts + n_groups)
    gl = jnp.where(is_g, lg, neg)
    gmax = jnp.max(gl, axis=-1, keepdims=True)
    gidx = jnp.min(jnp.where(gl == gmax, lane, big), axis=-1, keepdims=True) - n_experts
    g_w = 1.0 / jnp.sum(jnp.where(is_g, jnp.exp(lg - gmax), 0.0), axis=-1, keepdims=True)
    per = n_experts // n_groups
    in_group = (lane >= gidx * per) & (lane < gidx * per + per)
    el = jnp.where(in_group, lg, neg)
    v1 = jnp.max(el, axis=-1, keepdims=True)
    i1 = jnp.min(jnp.where(el == v1, lane, big), axis=-1, keepdims=True)
    el2 = jnp.where(lane == i1, neg, el)
    v2 = jnp.max(el2, axis=-1, keepdims=True)
    i2 = jnp.min(jnp.where(el2 == v2, lane, big), axis=-1, keepdims=True)
    e21 = jnp.exp(v2 - v1)
    p1 = 1.0 / (1.0 + e21)
    p2 = e21 / (1.0 + e21)
    gate_ref[...] = jnp.where(lane == i1, p1 * g_w, 0.0) + jnp.where(lane == i2, p2 * g_w, 0.0)


def _out_router(x, oab, oc, od, wo, g2, wr_hi, wr_lo, br, n_experts, n_groups, null_rows):
    n, d = x.shape
    w = GROUP_W
    tm = 256 if n % 256 == 0 else ROW_BLK
    row = lambda i: (i, 0)
    const = lambda i: (0, 0)
    return pl.pallas_call(
        functools.partial(_out_router_kernel, n_experts=n_experts, n_groups=n_groups, null_rows=null_rows, tm=tm),
        grid=(n // tm,),
        in_specs=[pl.BlockSpec((tm, d), row), pl.BlockSpec((tm, 2 * w), row), pl.BlockSpec((tm, w), row),
                  pl.BlockSpec((tm, w), row), pl.BlockSpec(wo.shape, const), pl.BlockSpec((1, d), const),
                  pl.BlockSpec(wr_hi.shape, const), pl.BlockSpec(wr_lo.shape, const), pl.BlockSpec((1, ROUTE_W), const)],
        out_specs=[pl.BlockSpec((tm, d), row), pl.BlockSpec((tm, d), row), pl.BlockSpec((tm, ROUTE_W), row)],
        out_shape=[jax.ShapeDtypeStruct((n, d), F32), jax.ShapeDtypeStruct((n, d), BF16),
                   jax.ShapeDtypeStruct((n, ROUTE_W), F32)],
        compiler_params=_cparams(("parallel",)),
        name="out_router",
    )(x, oab, oc, od, wo, g2, wr_hi, wr_lo, br)


MOE_EXPERTS_PER_STEP = 2


def _moe_kernel(x1_ref, h_ref, gate_ref, w1_ref, w3_ref, w2_ref, y_ref):
    step = pl.program_id(1)

    @pl.when(step == 0)
    def _():
        y_ref[...] = x1_ref[...]

    h = h_ref[...]
    gates = gate_ref[...]
    lane = _iota(gates.shape, 1)
    hids = []
    for j in range(MOE_EXPERTS_PER_STEP):
        e = step * MOE_EXPERTS_PER_STEP + j
        ge = jnp.sum(jnp.where(lane == e, gates, 0.0), axis=-1, keepdims=True)
        up = jnp.dot(h, w1_ref[j].astype(BF16), preferred_element_type=F32)
        lin = jnp.dot(h, w3_ref[j].astype(BF16), preferred_element_type=F32)
        hids.append(((up * _sigmoid(up)) * lin * ge).astype(BF16))
    w2 = jnp.concatenate([w2_ref[j].astype(BF16) for j in range(MOE_EXPERTS_PER_STEP)], axis=0)
    y_ref[...] += jnp.dot(jnp.concatenate(hids, axis=1), w2, preferred_element_type=F32)


def _moe_tile(n):
    for t in (1536, 1408, 1280, 1024, 640, 512, 256, 128):
        if n % t == 0:
            return t
    raise ValueError(f"row count {n} is not a multiple of {ROW_BLK}")


def _moe(x1, h, gates, w1, w3, w2, layer):
    n, d = x1.shape
    n_experts, _, f = w1.shape[1:]
    tm = _moe_tile(n)
    eb = MOE_EXPERTS_PER_STEP
    assert n_experts % eb == 0
    row = lambda i, e: (i, 0)
    return pl.pallas_call(
        _moe_kernel,
        grid=(n // tm, n_experts // eb),
        in_specs=[pl.BlockSpec((tm, d), row), pl.BlockSpec((tm, d), row), pl.BlockSpec((tm, ROUTE_W), row),
                  pl.BlockSpec((None, eb, d, f), lambda i, e: (layer, e, 0, 0)),
                  pl.BlockSpec((None, eb, d, f), lambda i, e: (layer, e, 0, 0)),
                  pl.BlockSpec((None, eb, f, d), lambda i, e: (layer, e, 0, 0))],
        out_specs=pl.BlockSpec((tm, d), row),
        out_shape=jax.ShapeDtypeStruct((n, d), F32),
        compiler_params=_cparams(("parallel", "arbitrary")),
        name="moe",
    )(x1, h, gates, w1, w3, w2)


def _rope_tables(pos):
    half = HEAD_DIM // 2
    inv = ROPE_BASE ** (-jnp.arange(half, dtype=F32) / half)
    ang = pos.astype(F32)[:, None] * jnp.tile(inv, LANE // half)[None, :]
    return jnp.cos(ang), jnp.sin(ang)


def _retention_log_gamma():
    lg = jnp.log1p(-jnp.exp2(-5.0 - jnp.arange(N_HG, dtype=F32)))
    return jnp.broadcast_to(jnp.repeat(lg, HEAD_DIM)[None, :], (ROW_BLK, GROUP_W))


def _row(v):
    return v.reshape(1, -1).astype(F32)


def _rwkv_params(p):
    w2 = p['rwkv_w2']
    a2 = p['rwkv_a2']
    w2p = jnp.concatenate([w2, jnp.zeros_like(a2)], axis=0)
    a2p = jnp.concatenate([jnp.zeros_like(w2), a2], axis=0)
    return (_row(p['rwkv_mu']), _row(p['rwkv_w0']), w2p, _row(p['rwkv_a0']), a2p, p['rwkv_g2'],
            _row(p['rwkv_kk']), _row(p['rwkv_ka']), _row(p['rwkv_rk']), _row(p['rwkv_ln_w']), _row(p['rwkv_ln_b']))


def _rwkv_state_to_tiles(s):
    n = s.shape[0]
    return s.reshape(n, 2, 2, HEAD_DIM, HEAD_DIM).transpose(0, 1, 3, 2, 4).reshape(n * 2 * HEAD_DIM, LANE)


def _rwkv_tiles_to_state(t, n):
    return t.reshape(n, 2, HEAD_DIM, 2, HEAD_DIM).transpose(0, 1, 3, 2, 4).reshape(n, N_HG, HEAD_DIM, HEAD_DIM)


def _diag_heads(s):
    return jnp.stack([s[:, h * HEAD_DIM:(h + 1) * HEAD_DIM, h * HEAD_DIM:(h + 1) * HEAD_DIM] for h in range(N_HG)], axis=1)


def kernel(x_prompt, x_sample, cache_sb_k, cache_sb_v, state_ret, state_hgrn, state_rwkv, state_rwkv_shift,
           page_table, meta_tokens, norm1, norm2, w_in, w_out, ret_norm, hgrn_lb_logits, hgrn_norm,
           sb_q_norm, sb_k_norm, sb_bias, rwkv_mu, rwkv_w0, rwkv_w2, rwkv_a0, rwkv_a2, rwkv_g2, rwkv_kk, rwkv_ka,
           rwkv_rk, rwkv_ln_w, rwkv_ln_b, moe_w_group, moe_b_group, moe_w_expert, moe_b_expert,
           moe_w1, moe_w3, moe_w2):
    bp, seq, dm = x_prompt.shape
    db, ds, _ = x_sample.shape
    depth = w_in.shape[0]
    w = GROUP_W
    assert seq % ROW_BLK == 0 and (db * ds) % ROW_BLK == 0 and ROW_BLK % ds == 0 and ds & (ds - 1) == 0
    assert w_in.shape[2] == LIN_COLS + 4 * w and cache_sb_k.shape[3] * cache_sb_k.shape[4] == w
    tp = seq + SB_QBLK
    nblk = tp // ROW_BLK
    tlen = seq + N_META
    past = page_table.shape[1] * cache_sb_k.shape[2]
    n_groups, e_per = moe_w_expert.shape[2:]
    n_experts = n_groups * e_per
    assert n_experts + n_groups <= ROUTE_W

    xp = jnp.concatenate([jnp.zeros((bp, PAD_FRONT, dm), F32),
                          jnp.broadcast_to(meta_tokens[None], (bp, N_META, dm)).astype(F32), x_prompt], axis=1)
    xp = xp.reshape(bp * tp, dm)
    xs = x_sample.reshape(db * ds, dm)
    null_rows = tuple(b * tp for b in range(bp))

    cos_p, sin_p = _rope_tables(jnp.maximum(jnp.arange(tp) - PAD_FRONT, 0))
    cos_s, sin_s = _rope_tables(jnp.tile(past + jnp.arange(ds), ROW_BLK // ds))
    lgam = _retention_log_gamma()
    cache_k = cache_sb_k.reshape(cache_sb_k.shape[:3] + (w,))
    cache_v = cache_sb_v.reshape(cache_sb_v.shape[:3] + (w,))
    tile_heads = lambda v: jnp.tile(v, N_HG)[None, :].astype(F32)

    prompt_rows, sample_rows = [], []
    for l in range(depth):
        wa = w_in[l][:, :LIN_COLS].astype(BF16)
        wb = w_in[l][:, LIN_COLS:].astype(BF16)
        g1 = _row(norm1[l])
        lin_args = (lgam, _row(ret_norm[l]), hgrn_lb_logits.astype(F32), _row(hgrn_norm[l]),
                    tile_heads(sb_q_norm[l]), tile_heads(sb_k_norm[l]))
        rwkv_prm = _rwkv_params(dict(rwkv_mu=rwkv_mu[l], rwkv_w0=rwkv_w0[l], rwkv_w2=rwkv_w2[l], rwkv_a0=rwkv_a0[l],
                                     rwkv_a2=rwkv_a2[l], rwkv_g2=rwkv_g2[l], rwkv_kk=rwkv_kk[l], rwkv_ka=rwkv_ka[l],
                                     rwkv_rk=rwkv_rk[l], rwkv_ln_w=rwkv_ln_w[l], rwkv_ln_b=rwkv_ln_b[l]))
        wo = w_out[l].astype(BF16)
        wr = jnp.zeros((dm, ROUTE_W), F32)
        wr = wr.at[:, :n_experts].set(moe_w_expert[l].reshape(dm, n_experts))
        wr = wr.at[:, n_experts:n_experts + n_groups].set(moe_w_group[l])
        wr_hi = wr.astype(BF16)
        wr_lo = (wr - wr_hi.astype(F32)).astype(BF16)
        br = jnp.zeros((1, ROUTE_W), F32)
        br = br.at[0, :n_experts].set(moe_b_expert[l].reshape(n_experts))
        br = br.at[0, n_experts:n_experts + n_groups].set(moe_b_group[l])
        router = (wo, _row(norm2[l]), wr_hi, wr_lo, br)

        pa, pb = _in_proj(xp, g1, wa, wb)
        oab, qs, knf, knb, vb, s_ret, s_hg = _lin_mix_prompt(pa, cos_p, sin_p, *lin_args, l, bp, nblk)
        oc = _sb_prompt(sb_bias[l].astype(F32), qs, knb, vb, bp, tp)
        od, s_rw = _rwkv_prompt(pb.reshape(bp, tp, -1), rwkv_prm, bp, nblk)
        x1, h2, gates = _out_router(xp, oab, oc, od.reshape(bp * tp, w), *router, n_experts, n_groups, null_rows)
        xp = _moe(x1, h2, gates, moe_w1, moe_w3, moe_w2, l)
        real = lambda a: a.reshape(bp, tp, -1)[:, PAD_FRONT:]
        prompt_rows.append((real(knf).reshape(bp, tlen, N_HG, HEAD_DIM),
                            real(pa[:, LIN_COLS - w:]).reshape(bp, tlen, N_HG, HEAD_DIM),
                            _diag_heads(s_ret), _diag_heads(s_hg), _diag_heads(s_rw).swapaxes(-1, -2),
                            pb.reshape(bp, tp, -1)[:, -1]))

        sa, sb = _in_proj(xs, g1, wa, wb)
        stack = lambda s: s.reshape(db, w, HEAD_DIM)
        oab, qs, knf, vf, s_ret, s_hg = _lin_mix_sample(sa, cos_s, sin_s, *lin_args, stack(state_ret[l]),
                                                        stack(state_hgrn[l]), l, ds)
        oc = _sb_sample(page_table, sb_bias[l].astype(F32), qs, knf, vf, cache_k, cache_v, l, ds)
        cw = sb.shape[-1]
        ovr = jnp.concatenate([state_rwkv_shift[l][:, None, :], jnp.zeros((db, ds - 1, cw), F32)], axis=1)
        od, s_rw = _rwkv_sample(sb.reshape(-1, ROW_BLK, cw), ovr.reshape(-1, ROW_BLK, cw),
                                _rwkv_state_to_tiles(state_rwkv[l]), rwkv_prm, ds)
        x1, h2, gates = _out_router(xs, oab, oc, od.reshape(db * ds, w), *router, n_experts, n_groups, ())
        xs = _moe(x1, h2, gates, moe_w1, moe_w3, moe_w2, l)
        sample_rows.append((knf.reshape(db, ds, N_HG, HEAD_DIM), vf.reshape(db, ds, N_HG, HEAD_DIM),
                            s_ret.reshape(db, N_HG, HEAD_DIM, HEAD_DIM), s_hg.reshape(db, N_HG, HEAD_DIM, HEAD_DIM),
                            _rwkv_tiles_to_state(s_rw, db), sb.reshape(db, ds, cw)[:, -1]))

    y_prompt = xp.reshape(bp, tp, dm)[:, SB_QBLK:]
    y_sample = xs.reshape(db, ds, dm)
    stacked_p = [jnp.stack(r) for r in zip(*prompt_rows)]
    stacked_s = [jnp.stack(r) for r in zip(*sample_rows)]
    return (y_prompt, y_sample, *stacked_p, *stacked_s)
```

```python
import functools
import math

import numpy as np
import jax
import jax.numpy as jnp
from jax import lax
from jax.experimental import pallas as pl
from jax.experimental.pallas import tpu as pltpu

F32 = jnp.float32
BF16 = jnp.bfloat16

HEAD_DIM = 64
N_HG = 4
GROUP_W = N_HG * HEAD_DIM
N_META = 16
LANE = 128
ROW_BLK = 128
SB_QBLK = 256
PAD_FRONT = SB_QBLK - N_META
LIN_COLS = 11 * GROUP_W
NORM_EPS = 1e-6
RWKV_LN_EPS = 64e-5
ROPE_BASE = 10000.0
LOG2E = math.log2(math.e)
SB_SCALE = HEAD_DIM ** -0.5 * LOG2E
E_PER_GROUP = 8
VMEM_LIMIT = 56 * 1024 * 1024


def _iota(shape, dim):
    return lax.broadcasted_iota(jnp.int32, shape, dim)


def _mask01(cond):
    return jnp.where(cond, 1.0, 0.0).astype(BF16)


def _dot(a, b):
    return jnp.dot(a.astype(BF16), b.astype(BF16), preferred_element_type=F32)


def _dot_nt(a, b):
    return lax.dot_general(a.astype(BF16), b.astype(BF16), (((1,), (1,)), ((), ())),
                           preferred_element_type=F32)


def _dot_tn(a, b):
    return lax.dot_general(a.astype(BF16), b.astype(BF16), (((0,), (0,)), ((), ())),
                           preferred_element_type=F32)


def _split(x, n):
    parts = []
    r = x
    for _ in range(n):
        h = r.astype(BF16)
        parts.append(h)
        r = r - h.astype(F32)
    return parts


def _dotx(x, m, n=2):
    out = None
    for p in _split(x, n):
        t = jnp.dot(p, m, preferred_element_type=F32)
        out = t if out is None else out + t
    return out


def _xdot(m, x, n=3):
    out = None
    for p in _split(x, n):
        t = jnp.dot(m, p, preferred_element_type=F32)
        out = t if out is None else out + t
    return out


def _head_ones(w):
    return _mask01((_iota((w, w), 0) >> 6) == (_iota((w, w), 1) >> 6))


def _sigmoid(x):
    return 1.0 / (1.0 + jnp.exp(-x))


def _softplus(x):
    return jnp.maximum(x, 0.0) + jnp.log(1.0 + jnp.exp(-jnp.abs(x)))


def _softplus2(x):
    return jnp.maximum(x, 0.0) + jnp.log2(1.0 + jnp.exp2(-jnp.abs(x)))


def _head_rms(x, gain, ones):
    ms = _dotx(x * x, ones, 2) * (1.0 / HEAD_DIM)
    return x * lax.rsqrt(ms + NORM_EPS) * gain


def _row_tile(n):
    for t in (512, 256, 128):
        if n % t == 0:
            return t
    raise ValueError(f"row count {n} is not a multiple of {ROW_BLK}")


def _cparams(sem, flags=None):
    return pltpu.CompilerParams(dimension_semantics=sem, vmem_limit_bytes=VMEM_LIMIT, flags=flags)


def _in_proj_kernel(x_ref, g_ref, wa_ref, wb_ref, oa_ref, ob_ref):
    x = x_ref[...]
    ms = jnp.mean(x * x, axis=-1, keepdims=True)
    h = (x * lax.rsqrt(ms + NORM_EPS) * g_ref[...]).astype(BF16)
    oa_ref[...] = jnp.dot(h, wa_ref[...], preferred_element_type=F32)
    ob_ref[...] = jnp.dot(h, wb_ref[...], preferred_element_type=F32)


def _in_proj(x, g, wa, wb):
    n, d = x.shape
    tm = 256 if n % 256 == 0 else ROW_BLK
    return pl.pallas_call(
        _in_proj_kernel,
        grid=(n // tm,),
        in_specs=[pl.BlockSpec((tm, d), lambda i: (i, 0)),
                  pl.BlockSpec((1, d), lambda i: (0, 0)),
                  pl.BlockSpec(wa.shape, lambda i: (0, 0)),
                  pl.BlockSpec(wb.shape, lambda i: (0, 0))],
        out_specs=[pl.BlockSpec((tm, wa.shape[1]), lambda i: (i, 0)),
                   pl.BlockSpec((tm, wb.shape[1]), lambda i: (i, 0))],
        out_shape=[jax.ShapeDtypeStruct((n, wa.shape[1]), F32),
                   jax.ShapeDtypeStruct((n, wb.shape[1]), F32)],
        compiler_params=_cparams(("parallel",)),
        name="in_proj",
    )(x, g, wa, wb)


def _drain(steps):
    while True:
        try:
            next(steps)
        except StopIteration as stop:
            return stop.value


def _lockstep(sequences):
    while all([next(seq, False) for seq in sequences]):
        pass


def _gla_block(q, k, v, lw, states, sub, chain):
    rows, w = q.shape
    nsub = rows // sub
    shift = int(math.log2(sub))
    r_i = _iota((rows, rows), 0)
    c_i = _iota((rows, rows), 1)
    same = (r_i >> shift) == (c_i >> shift)
    tri = _mask01(same & (c_i <= r_i))
    blk = _mask01(same)
    sub_of_row = _mask01((_iota((rows, LANE), 0) >> shift) == _iota((rows, LANE), 1))
    bc = bl = dsum = None
    for p in _split(lw, 3):
        t1 = jnp.dot(tri, p, preferred_element_type=F32)
        t2 = jnp.dot(blk, p, preferred_element_type=F32)
        t3 = lax.dot_general(p, sub_of_row, (((0,), (0,)), ((), ())), preferred_element_type=F32)
        bc = t1 if bc is None else bc + t1
        bl = t2 if bl is None else bl + t2
        dsum = t3 if dsum is None else dsum + t3
    dcol = jnp.exp(dsum)
    qt = q * jnp.exp(bc)
    kh = k * jnp.exp(bl - bc)
    ones = _head_ones(w)

    local = _iota((rows, w), 0) & (sub - 1)
    o_intra = jnp.zeros((rows, w), F32)
    for dist in range(sub):
        back = (lambda x: x) if dist == 0 else (lambda x: pltpu.roll(x, dist, 0))
        e = jnp.exp(jnp.where(local >= dist, bc - back(bc), -1e30))
        p = q * back(k) * e
        o_intra = o_intra + _dot(p, ones) * back(v)

    bdmask = (_iota((w, w), 0) >> 6) == (_iota((w, w), 1) >> 6)
    rows_of = lambda i: slice(i * sub, (i + 1) * sub)
    kvs = [jnp.where(bdmask, _dot_tn(kh[rows_of(i)], v[rows_of(i)]), 0.0) for i in range(nsub)]
    if chain:
        outs = []
        s = states
        for i in range(nsub):
            outs.append(_dot(qt[rows_of(i)], s))
            yield True
            s = s * dcol[:, i:i + 1] + kvs[i]
        new_states = s
    else:
        outs = [_dot(qt[rows_of(i)], states[i]) for i in range(nsub)]
        new_states = [states[i] * dcol[:, i:i + 1] + kvs[i] for i in range(nsub)]
    o = o_intra + jnp.concatenate(outs, axis=0)
    return o, new_states


def _retention_block(q, k, v, lgam, s):
    rows, w = q.shape
    width = N_HG * rows
    t_row = _iota((rows, w), 0).astype(F32)
    qd = q * jnp.exp((t_row + 1.0) * lgam)
    kd = k * jnp.exp((rows - 1.0 - t_row) * lgam)
    left = _iota((rows, LANE), 1) < HEAD_DIM

    def split_heads(x):
        x = x.astype(BF16)
        zero = jnp.zeros_like(x)
        return jnp.concatenate([jnp.where(left, x, zero), jnp.where(left, zero, x)], axis=0)

    dist = (_iota((rows, width), 0) - (_iota((rows, width), 1) & (rows - 1))).astype(F32)
    lg_heads = jnp.concatenate([jnp.broadcast_to(lgam[:, h * HEAD_DIM:h * HEAD_DIM + 1], (rows, rows))
                                for h in range(N_HG)], axis=1)
    dmat = jnp.where(dist >= 0.0, jnp.exp(jnp.maximum(dist, 0.0) * lg_heads), 0.0)
    scores = jnp.concatenate([_dot_nt(q[:, p * LANE:(p + 1) * LANE], split_heads(k[:, p * LANE:(p + 1) * LANE]))
                              for p in range(2)], axis=1) * dmat
    o_intra = jnp.concatenate([_dot(scores[:, p * 2 * LANE:(p + 1) * 2 * LANE], split_heads(v[:, p * LANE:(p + 1) * LANE]))
                               for p in range(2)], axis=1)
    o = o_intra + _dot(qd, s)
    ones_rl = jnp.ones((rows, LANE), BF16)
    dsum = None
    for part in _split(lgam, 3):
        t3 = lax.dot_general(part, ones_rl, (((0,), (0,)), ((), ())), preferred_element_type=F32)
        dsum = t3 if dsum is None else dsum + t3
    bdmask = (_iota((w, w), 0) >> 6) == (_iota((w, w), 1) >> 6)
    s = s * jnp.exp(dsum)[:, 0:1] + jnp.where(bdmask, _dot_tn(kd, v), 0.0)
    return o, s


def _lin_mix_math(p, cos, sin, lgam, retw, lb_logits, hgw, qnw, knw, layer, st_ret, st_hg, sub, chain):
    w = GROUP_W
    qa, ka, va, ga, qb, fb, ib, gb, qc, kc, vc = [p[:, i * w:(i + 1) * w] for i in range(11)]
    rows = p.shape[0]
    ones = _head_ones(w)
    lane = _iota((rows, LANE), 1)
    first_half = (lane & (HEAD_DIM - 1)) < (HEAD_DIM // 2)

    def rope(x):
        halves = []
        for hp in range(w // LANE):
            xh = x[:, hp * LANE:(hp + 1) * LANE]
            rot = jnp.where(first_half, -pltpu.roll(xh, LANE - HEAD_DIM // 2, 1), pltpu.roll(xh, HEAD_DIM // 2, 1))
            halves.append(xh * cos + rot * sin)
        return jnp.concatenate(halves, axis=1)

    q_ret = rope(qa)
    k_ret = rope(ka) * (HEAD_DIM ** -0.5)
    if chain:
        o_ret, st_ret = _retention_block(q_ret, k_ret, va, lgam, st_ret)
    else:
        o_ret, st_ret = yield from _gla_block(q_ret, k_ret, va, lgam, st_ret, sub, chain)
    yield True
    o_a = _head_rms(o_ret, retw, ones) * (ga * _sigmoid(ga))

    lg = [lb_logits[d:d + 1, :] for d in range(lb_logits.shape[0])]
    mx = functools.reduce(jnp.maximum, lg)
    ex = [jnp.exp(row - mx) for row in lg]
    lb = sum(ex[1:layer + 1], jnp.zeros_like(mx)) / sum(ex[1:], ex[0])
    log_sig = jnp.minimum(fb, 0.0) - jnp.log1p(jnp.exp(-jnp.abs(fb)))
    t_a = jnp.broadcast_to(jnp.log(lb), fb.shape)
    t_b = jnp.log1p(-lb) + log_sig
    logf = jnp.maximum(t_a, t_b) + jnp.log1p(jnp.exp(-jnp.abs(t_a - t_b)))
    k_hg = (1.0 - lb) * _sigmoid(-fb)
    yield True
    o_hg, st_hg = yield from _gla_block(qb, k_hg, ib, logf, st_hg, sub, chain)
    o_b = _head_rms(o_hg, hgw, ones) * (gb * _sigmoid(gb))

    qn = _head_rms(qc, qnw, ones)
    kn = _head_rms(kc, knw, ones)
    return o_a, o_b, qn, kn, vc, st_ret, st_hg


def _lin_mix_prompt_kernel(p_ref, cos_ref, sin_ref, lgam_ref, retw_ref, lbl_ref, hgw_ref, qnw_ref, knw_ref,
                           oab_ref, qs_ref, knf_ref, knb_ref, vb_ref, sret_ref, shg_ref,
                           st_ret, st_hg, *, layer, sub, nseq):
    i = pl.program_id(0)

    def sequence(b):
        @pl.when(i == 0)
        def _():
            st_ret[b] = jnp.zeros(st_ret.shape[1:], F32)
            st_hg[b] = jnp.zeros(st_hg.shape[1:], F32)

        o_a, o_b, qn, kn, vc, s1, s2 = yield from _lin_mix_math(
            p_ref[b], cos_ref[...], sin_ref[...], lgam_ref[...], retw_ref[...], lbl_ref, hgw_ref[...],
            qnw_ref[...], knw_ref[...], layer, st_ret[b], st_hg[b], sub, True)
        st_ret[b] = s1
        st_hg[b] = s2
        oab_ref[b] = jnp.concatenate([o_a, o_b], axis=1)
        qs_ref[b] = (qn * SB_SCALE).astype(BF16)
        knf_ref[b] = kn
        knb_ref[b] = kn.astype(BF16)
        vb_ref[b] = vc.astype(BF16)

        @pl.when(i == pl.num_programs(0) - 1)
        def _():
            sret_ref[b] = s1
            shg_ref[b] = s2

    _lockstep([sequence(b) for b in range(nseq)])


def _lin_mix_prompt(proj, cos, sin, lgam, retw, lbl, hgw, qnw, knw, layer, nb, nblk):
    n = proj.shape[0]
    tp = n // nb
    w = GROUP_W
    blk = lambda i: (0, i, 0)
    const = lambda i: (0, 0)
    whole = lambda i: (0, 0, 0)
    outs = pl.pallas_call(
        functools.partial(_lin_mix_prompt_kernel, layer=layer, sub=16, nseq=nb),
        grid=(nblk,),
        in_specs=[pl.BlockSpec((nb, ROW_BLK, LIN_COLS), blk),
                  pl.BlockSpec((ROW_BLK, LANE), lambda i: (i, 0)),
                  pl.BlockSpec((ROW_BLK, LANE), lambda i: (i, 0)),
                  pl.BlockSpec((ROW_BLK, w), const), pl.BlockSpec((1, w), const),
                  pl.BlockSpec(lbl.shape, const), pl.BlockSpec((1, w), const),
                  pl.BlockSpec((1, w), const), pl.BlockSpec((1, w), const)],
        out_specs=[pl.BlockSpec((nb, ROW_BLK, 2 * w), blk),
                   pl.BlockSpec((nb, ROW_BLK, w), blk), pl.BlockSpec((nb, ROW_BLK, w), blk),
                   pl.BlockSpec((nb, ROW_BLK, w), blk), pl.BlockSpec((nb, ROW_BLK, w), blk),
                   pl.BlockSpec((nb, w, w), whole), pl.BlockSpec((nb, w, w), whole)],
        out_shape=[jax.ShapeDtypeStruct((nb, tp, 2 * w), F32),
                   jax.ShapeDtypeStruct((nb, tp, w), BF16), jax.ShapeDtypeStruct((nb, tp, w), F32),
                   jax.ShapeDtypeStruct((nb, tp, w), BF16), jax.ShapeDtypeStruct((nb, tp, w), BF16),
                   jax.ShapeDtypeStruct((nb, w, w), F32), jax.ShapeDtypeStruct((nb, w, w), F32)],
        scratch_shapes=[pltpu.VMEM((nb, w, w), F32), pltpu.VMEM((nb, w, w), F32)],
        compiler_params=_cparams(("arbitrary",)),
        name="lin_mix_prompt",
    )(proj.reshape(nb, tp, LIN_COLS), cos, sin, lgam, retw, lbl, hgw, qnw, knw)
    return [o.reshape(n, o.shape[-1]) for o in outs[:5]] + list(outs[5:])


def _lin_mix_sample_kernel(p_ref, cos_ref, sin_ref, lgam_ref, retw_ref, lbl_ref, hgw_ref, qnw_ref, knw_ref,
                           sret_in, shg_in,
                           oab_ref, qs_ref, knf_ref, vf_ref, sret_out, shg_out, *, layer, sub):
    w = GROUP_W
    nseq = ROW_BLK // sub
    bdmask = (_iota((w, w), 0) >> 6) == (_iota((w, w), 1) >> 6)
    rep = _mask01(_iota((HEAD_DIM, w), 0) == (_iota((HEAD_DIM, w), 1) & (HEAD_DIM - 1)))
    rep_t = _mask01((_iota((w, HEAD_DIM), 0) & (HEAD_DIM - 1)) == _iota((w, HEAD_DIM), 1))

    def expand(ref):
        return [jnp.where(bdmask, _dotx(ref[j], rep, 3), 0.0) for j in range(nseq)]

    def extract(ref, states):
        for j in range(nseq):
            ref[j] = _dotx(states[j], rep_t, 3)

    o_a, o_b, qn, kn, vc, s1, s2 = _drain(_lin_mix_math(
        p_ref[...], cos_ref[...], sin_ref[...], lgam_ref[...], retw_ref[...], lbl_ref, hgw_ref[...],
        qnw_ref[...], knw_ref[...], layer, expand(sret_in), expand(shg_in), sub, False))
    extract(sret_out, s1)
    extract(shg_out, s2)
    oab_ref[...] = jnp.concatenate([o_a, o_b], axis=1)
    qs_ref[...] = qn * SB_SCALE
    knf_ref[...] = kn
    vf_ref[...] = vc


def _lin_mix_sample(proj, cos, sin, lgam, retw, lbl, hgw, qnw, knw, s_ret, s_hg, layer, dec_seq):
    n = proj.shape[0]
    w = GROUP_W
    nseq = ROW_BLK // dec_seq
    row = lambda i: (i, 0)
    const = lambda i: (0, 0)
    st_spec = pl.BlockSpec((nseq, w, HEAD_DIM), lambda i: (i, 0, 0))
    return pl.pallas_call(
        functools.partial(_lin_mix_sample_kernel, layer=layer, sub=dec_seq),
        grid=(n // ROW_BLK,),
        in_specs=[pl.BlockSpec((ROW_BLK, LIN_COLS), row),
                  pl.BlockSpec((ROW_BLK, LANE), const), pl.BlockSpec((ROW_BLK, LANE), const),
                  pl.BlockSpec((ROW_BLK, w), const), pl.BlockSpec((1, w), const),
                  pl.BlockSpec(lbl.shape, const), pl.BlockSpec((1, w), const),
                  pl.BlockSpec((1, w), const), pl.BlockSpec((1, w), const),
                  st_spec, st_spec],
        out_specs=[pl.BlockSpec((ROW_BLK, 2 * w), row),
                   pl.BlockSpec((ROW_BLK, w), row), pl.BlockSpec((ROW_BLK, w), row),
                   pl.BlockSpec((ROW_BLK, w), row), st_spec, st_spec],
        out_shape=[jax.ShapeDtypeStruct((n, 2 * w), F32),
                   jax.ShapeDtypeStruct((n, w), F32), jax.ShapeDtypeStruct((n, w), F32),
                   jax.ShapeDtypeStruct((n, w), F32),
                   jax.ShapeDtypeStruct(s_ret.shape, F32), jax.ShapeDtypeStruct(s_hg.shape, F32)],
        compiler_params=_cparams(("parallel",)),
        name="lin_mix_sample",
    )(proj, cos, sin, lgam, retw, lbl, hgw, qnw, knw, s_ret, s_hg)


def _sb_consts(rows, tmask):
    j_i = _iota((LANE, 2 * LANE), 0)
    s_i = _iota((LANE, 2 * LANE), 1)
    ucat = _mask01((s_i >= LANE) | (j_i >= s_i))
    causal = _iota((rows, LANE), 1) < (_iota((rows, LANE), 0) & tmask)
    return ucat, causal


def _sb_prompt_kernel(bias_ref, q_ref, k_ref, v_ref, o_ref):
    i = pl.program_id(1)
    qb = SB_QBLK
    kpq = qb // ROW_BLK
    q = q_ref[...]
    left = _iota((ROW_BLK, LANE), 1) < HEAD_DIM
    zero = jnp.zeros((ROW_BLK, LANE), BF16)
    width = N_HG * ROW_BLK
    lane = _iota((qb, width), 1)
    bias_row = jnp.zeros((qb, width), F32)
    for h in range(N_HG):
        bias_row = jnp.where((lane >> 7) == h, bias_ref[h] * LOG2E, bias_row)
    sk = lane & (ROW_BLK - 1)
    tq = _iota((qb, width), 0)
    r2 = _iota((2 * LANE, 2 * LANE), 0)
    c2 = _iota((2 * LANE, 2 * LANE), 1)
    ubd = _mask01(((r2 >> 7) == (c2 >> 7)) & ((r2 & (LANE - 1)) > (c2 & (LANE - 1))))

    def split_heads(x):
        return jnp.concatenate([jnp.where(left, x, zero), jnp.where(left, zero, x)], axis=0)

    pair = lambda x, p: x[:, p * 2 * LANE:(p + 1) * 2 * LANE]

    def steps(kbs, masks, carry, acc0, acc1):
        blocks = []
        for kb in kbs:
            start = pl.multiple_of(kb * ROW_BLK, ROW_BLK)
            blocks.append((k_ref[pl.ds(start, ROW_BLK), :], v_ref[pl.ds(start, ROW_BLK), :]))
        zs = [jnp.concatenate([_dot_nt(q[:, p * LANE:(p + 1) * LANE], split_heads(kblk[:, p * LANE:(p + 1) * LANE]))
                               for p in range(2)], axis=1) + bias_row for kblk, _ in blocks]
        sp_own = [_softplus2(z) for z in zs]
        sps = [s if m is None else jnp.where(m, s, 0.0) for s, m in zip(sp_own, masks)]
        ts = []
        for sp in sps:
            spb = sp.astype(BF16)
            ts.append(jnp.concatenate([jnp.dot(pair(spb, p), ubd, preferred_element_type=F32) for p in range(2)], axis=1))
        abs_ = []
        for z, own, sp, t, m in zip(zs, sp_own, sps, ts, masks):
            a = jnp.exp2((z - own) - (carry + t))
            if m is not None:
                a = jnp.where(m, a, 0.0)
            abs_.append(a.astype(BF16))
            carry = carry + jnp.concatenate(
                [jnp.broadcast_to(jnp.sum(sp[:, h * ROW_BLK:(h + 1) * ROW_BLK], axis=-1, keepdims=True), (qb, ROW_BLK))
                 for h in range(N_HG)], axis=1)
        accs = [acc0, acc1]
        for ab, (_, vblk) in zip(abs_, blocks):
            for p in range(2):
                accs[p] = accs[p] + jnp.dot(pair(ab, p), split_heads(vblk[:, p * LANE:(p + 1) * LANE]),
                                            preferred_element_type=F32)
        return carry, accs[0], accs[1]

    zeros = jnp.zeros((qb, LANE), F32)
    state = (jnp.zeros((qb, width), F32), zeros, zeros)
    state = steps([i * kpq + d for d in reversed(range(kpq))], [sk + d * ROW_BLK < tq for d in reversed(range(kpq))],
                  *state)

    def earlier_blocks(j, c):
        return steps([(i - j) * kpq - 1 - d for d in range(kpq)], [None] * kpq, *c)

    state = lax.fori_loop(0, i, earlier_blocks, state)
    o_ref[...] = jnp.concatenate([state[1], state[2]], axis=1)


def _sb_prompt(bias, qs, kb, vb, nb, tp):
    n, w = qs.shape
    nq = tp // SB_QBLK
    return pl.pallas_call(
        _sb_prompt_kernel,
        grid=(nb, nq),
        in_specs=[pl.BlockSpec(memory_space=pltpu.SMEM),
                  pl.BlockSpec((SB_QBLK, w), lambda b, i: (b * nq + i, 0)),
                  pl.BlockSpec((tp, w), lambda b, i: (b, 0)),
                  pl.BlockSpec((tp, w), lambda b, i: (b, 0))],
        out_specs=pl.BlockSpec((SB_QBLK, w), lambda b, i: (b * nq + i, 0)),
        out_shape=jax.ShapeDtypeStruct((n, w), F32),
        compiler_params=_cparams(("parallel", "arbitrary")),
        name="sb_prompt",
    )(bias, qs, kb, vb)


def _sb_sample_kernel(pt_ref, bias_ref, q_ref, kn_ref, vn_ref, *rest, n_pages, dec_seq):
    del pt_ref
    k_pages = rest[:n_pages]
    v_pages = rest[n_pages:2 * n_pages]
    o_ref = rest[2 * n_pages]
    w = GROUP_W
    rows = N_HG * dec_seq
    q = q_ref[...]
    head_of_lane = _iota((dec_seq, w), 1) >> 6
    qbd = jnp.concatenate([jnp.where(head_of_lane == h, q, 0.0) for h in range(N_HG)], axis=0).astype(BF16)
    bias_col = jnp.concatenate([jnp.full((dec_seq, 1), bias_ref[h] * LOG2E, F32) for h in range(N_HG)], axis=0)
    ucat, causal = _sb_consts(rows, dec_seq - 1)
    pad = jnp.zeros((LANE - dec_seq, w), F32)
    k_new = jnp.concatenate([kn_ref[...], pad], axis=0)
    v_new = jnp.concatenate([vn_ref[...], pad], axis=0)
    key_blocks = [k_new] + [k_pages[p][...] for p in reversed(range(n_pages))]
    val_blocks = [v_new] + [v_pages[p][...] for p in reversed(range(n_pages))]
    zs = [_dot_nt(qbd, kb) + bias_col for kb in key_blocks]
    sps = [_softplus2(z) for z in zs]
    sps[0] = jnp.where(causal, sps[0], 0.0)
    ts = [_dotx(sp, ucat, 2) for sp in sps]
    carry = jnp.zeros((rows, LANE), F32)
    weights = []
    for z, t in zip(zs, ts):
        weights.append(jnp.exp2(z - (carry + t[:, :LANE])))
        carry = carry + t[:, LANE:]
    weights[0] = jnp.where(causal, weights[0], 0.0)
    acc = None
    for a, vb in zip(weights, val_blocks):
        term = _dot(a, vb)
        acc = term if acc is None else acc + term
    out = jnp.zeros((dec_seq, w), F32)
    for h in range(N_HG):
        out = out + jnp.where(head_of_lane == h, acc[h * dec_seq:(h + 1) * dec_seq], 0.0)
    o_ref[...] = out


def _sb_sample(page_table, bias, qs, kn, vn, cache_k, cache_v, layer, dec_seq):
    n, w = qs.shape
    db, n_pages = page_table.shape
    page = cache_k.shape[2]
    assert page == LANE and dec_seq % 8 == 0
    row = lambda b, pt: (b, 0)
    page_specs = [pl.BlockSpec((None, None, page, w), lambda b, pt, p=p: (layer, pt[b, p], 0, 0))
                  for p in range(n_pages)]
    grid_spec = pltpu.PrefetchScalarGridSpec(
        num_scalar_prefetch=1,
        grid=(db,),
        in_specs=[pl.BlockSpec(memory_space=pltpu.SMEM),
                  pl.BlockSpec((dec_seq, w), row), pl.BlockSpec((dec_seq, w), row), pl.BlockSpec((dec_seq, w), row)]
                 + page_specs + page_specs,
        out_specs=pl.BlockSpec((dec_seq, w), row),
    )
    return pl.pallas_call(
        functools.partial(_sb_sample_kernel, n_pages=n_pages, dec_seq=dec_seq),
        grid_spec=grid_spec,
        out_shape=jax.ShapeDtypeStruct((n, w), F32),
        compiler_params=_cparams(("arbitrary",)),
        name="sb_sample",
    )(page_table, bias, qs, kn, vn, *([cache_k] * n_pages), *([cache_v] * n_pages))


def _rwkv_prologue(rw, prev, mu, w0, w2p, a0, a2p, g2, kkp, ka, rk, ones):
    w = GROUP_W
    xm = rw + (prev - rw) * mu
    r = xm[:, 0:w]
    k = xm[:, w:2 * w]
    v = xm[:, 2 * w:3 * w]
    wa = xm[:, 3 * w:3 * w + LANE]
    gl = xm[:, 3 * w + LANE:]
    wd = w0 + _dot(jnp.tanh(wa), w2p)
    log_decay = -jnp.exp(-_softplus(-wd) - 0.5)
    a = _sigmoid(a0 + _dot(wa, a2p))
    g = _dot(_sigmoid(gl), g2)
    kk = k * kkp
    kk = kk * lax.rsqrt(jnp.maximum(_dotx(kk * kk, ones, 2), 1e-12))
    k_rw = k * (1.0 + (a - 1.0) * ka)
    bonus = _dotx(r * k_rw * rk, ones, 2) * v
    return r, log_decay, k_rw, v, kk, kk * a, g, bonus


RWKV_SUB = 64


def _rwkv_group_norm(o, lnw, lnb, bonus, g, ones):
    mu_h = _dotx(o, ones, 2) * (1.0 / HEAD_DIM)
    dlt = o - mu_h
    var = _dotx(dlt * dlt, ones, 2) * (1.0 / HEAD_DIM)
    return (dlt * lax.rsqrt(var + RWKV_LN_EPS) * lnw + lnb + bonus) * g


def _rwkv_chunk_kernel(rw_ref, mu_ref, w0_ref, w2p_ref, a0_ref, a2p_ref, g2_ref, kkp_ref, ka_ref, rk_ref, lnw_ref, lnb_ref,
                       o_ref, s_out_ref, carry_s, state_s, *, sub, nseq):
    prm = (mu_ref, w0_ref, w2p_ref, a0_ref, a2p_ref, g2_ref, kkp_ref, ka_ref, rk_ref, lnw_ref, lnb_ref)
    _lockstep([_rwkv_chunk_sequence(q, rw_ref, prm, o_ref, s_out_ref, carry_s, state_s, sub) for q in range(nseq)])


def _rwkv_chunk_sequence(seq, rw_ref, prm, o_ref, s_out_ref, carry_s, state_s, sub):
    mu_ref, w0_ref, w2p_ref, a0_ref, a2p_ref, g2_ref, kkp_ref, ka_ref, rk_ref, lnw_ref, lnb_ref = prm
    i = pl.program_id(0)
    w = GROUP_W
    rows = ROW_BLK
    nsub = rows // sub
    shift = int(math.log2(sub))
    ones = _head_ones(w)

    @pl.when(i == 0)
    def _():
        carry_s[seq] = jnp.zeros(carry_s.shape[1:], F32)
        state_s[seq] = jnp.zeros(state_s.shape[1:], F32)

    rw = rw_ref[seq]
    prev = jnp.where(_iota((rows, 1), 0) == 0, carry_s[seq, 0:1, :], pltpu.roll(rw, 1, 0))
    carry_s[seq, 0:1, :] = rw[rows - 1:rows, :]
    r, lw, k, v, kk, b, g, bonus = _rwkv_prologue(
        rw, prev, mu_ref[...], w0_ref[...], w2p_ref[...], a0_ref[...], a2p_ref[...], g2_ref[...],
        kkp_ref[...], ka_ref[...], rk_ref[...], ones)
    yield True

    r_i = _iota((rows, rows), 0)
    c_i = _iota((rows, rows), 1)
    same = (r_i >> shift) == (c_i >> shift)
    tri = _mask01(same & (c_i <= r_i))
    blk = _mask01(same)
    sub_of_row = _mask01((_iota((rows, LANE), 0) >> shift) == _iota((rows, LANE), 1))
    c = cl = dsum = None
    for part in _split(lw, 3):
        t1 = jnp.dot(tri, part, preferred_element_type=F32)
        t2 = jnp.dot(blk, part, preferred_element_type=F32)
        t3 = lax.dot_general(part, sub_of_row, (((0,), (0,)), ((), ())), preferred_element_type=F32)
        c = t1 if c is None else c + t1
        cl = t2 if cl is None else cl + t2
        dsum = t3 if dsum is None else dsum + t3
    yield True
    dcol = jnp.exp(dsum)
    kkd = kk * jnp.exp(c - lw)
    rd = r * jnp.exp(c)
    grow = jnp.exp(-c)
    kt = k * grow
    bt = b * grow
    tail = jnp.exp(cl - c)
    khat = k * tail
    bhat = b * tail

    left = _iota((rows, LANE), 1) < HEAD_DIM

    def split_heads(x):
        zero = jnp.zeros_like(x)
        return jnp.concatenate([jnp.where(left, x, zero), jnp.where(left, zero, x)], axis=0)

    lanes_p = lambda x, p: x[:, p * LANE:(p + 1) * LANE]
    wide = 2 * LANE
    t_i = _iota((rows, wide), 0)
    s_i = _iota((rows, wide), 1) & (rows - 1)
    same_sub = (t_i >> shift) == (s_i >> shift)
    incl = same_sub & (s_i <= t_i)
    strict = same_sub & (s_i < t_i)

    def dot3(x, y_hi, y_lo):
        x_hi, x_lo = _split(x, 2)
        nt = lambda a_, b_: lax.dot_general(a_, b_, (((1,), (1,)), ((), ())), preferred_element_type=F32)
        return nt(x_hi, y_hi) + nt(x_hi, y_lo) + nt(x_lo, y_hi)

    o_parts, y_parts, abr, abk = [], [], [], []
    for p in range(2):
        keys = jnp.concatenate([split_heads(lanes_p(kt, p)), split_heads(lanes_p(bt, p))], axis=0)
        keys_hi, keys_lo = _split(keys, 2)
        c_r = lax.dot_general(lanes_p(rd, p).astype(BF16), keys_hi, (((1,), (1,)), ((), ())),
                              preferred_element_type=F32)
        c_kk = dot3(lanes_p(kkd, p), keys_hi, keys_lo)
        akr = jnp.where(incl, c_r[:, :wide], 0.0)
        abr.append(jnp.where(incl, c_r[:, wide:], 0.0))
        akk = jnp.where(strict, c_kk[:, :wide], 0.0)
        abk.append(jnp.where(strict, c_kk[:, wide:], 0.0))
        v_heads = split_heads(lanes_p(v, p)).astype(BF16)
        o_parts.append(jnp.dot(akr.astype(BF16), v_heads, preferred_element_type=F32))
        y_parts.append(_dotx(akk, v_heads, 2))
    yield True
    o_acc = jnp.concatenate(o_parts, axis=1)
    y_intra = jnp.concatenate(y_parts, axis=1)

    def mm3(x, y):
        x_hi, x_lo = _split(x, 2)
        y_hi, y_lo = _split(y, 2)
        mm = lambda a_, b_: jnp.dot(a_, b_, preferred_element_type=F32)
        return mm(x_hi, y_hi) + mm(x_hi, y_lo) + mm(x_lo, y_hi)

    head_of_lane = _iota((rows, w), 1) >> 6
    eye = (_iota((rows, rows), 0) == _iota((rows, rows), 1)).astype(F32)
    powers = [-abk[h // 2][:, (h % 2) * rows:(h % 2 + 1) * rows] for h in range(N_HG)]
    invs = [eye + n for n in powers]
    for _ in range(shift - 1):
        powers = [mm3(n, n) for n in powers]
        yield True
        invs = [t + mm3(t, n) for t, n in zip(invs, powers)]
    yield True
    u_intra = jnp.zeros((rows, w), F32)
    g_mat = jnp.zeros((rows, w), F32)
    for h in range(N_HG):
        u_intra = u_intra + mm3(invs[h], jnp.where(head_of_lane == h, y_intra, 0.0))
        g_mat = g_mat + mm3(invs[h], jnp.where(head_of_lane == h, kkd, 0.0))
    yield True

    bdmask = (_iota((w, w), 0) >> 6) == (_iota((w, w), 1) >> 6)
    m = state_s[seq]
    us = []
    o_state = []
    for j in range(nsub):
        sl = slice(j * sub, (j + 1) * sub)
        from_state = _dot(jnp.concatenate([g_mat[sl], rd[sl]], axis=0), m)
        yield True
        u = u_intra[sl] + from_state[:sub]
        o_state.append(from_state[sub:])
        us.append(u)
        kv = _dot_tn(jnp.concatenate([khat[sl], -bhat[sl]], axis=0), jnp.concatenate([v[sl], u], axis=0))
        yield True
        m = m * dcol[:, j:j + 1] + jnp.where(bdmask, kv, 0.0)
    state_s[seq] = m
    u_all = jnp.concatenate(us, axis=0)
    o_u = jnp.concatenate([jnp.dot(abr[p].astype(BF16), split_heads(lanes_p(u_all, p)).astype(BF16),
                                   preferred_element_type=F32) for p in range(2)], axis=1)
    yield True
    o = o_acc + jnp.concatenate(o_state, axis=0) - o_u
    o_ref[seq] = _rwkv_group_norm(o, lnw_ref[...], lnb_ref[...], bonus, g, ones)

    @pl.when(i == pl.num_programs(0) - 1)
    def _():
        s_out_ref[seq] = m


def _rwkv_prompt(rw, prm, nb, nblk):
    w = GROUP_W
    cw = rw.shape[-1]
    const = lambda i: (0, 0)
    blk = lambda i: (0, i, 0)
    return pl.pallas_call(
        functools.partial(_rwkv_chunk_kernel, sub=RWKV_SUB, nseq=nb),
        grid=(nblk,),
        in_specs=[pl.BlockSpec((nb, ROW_BLK, cw), blk)] + [pl.BlockSpec(a.shape, const) for a in prm],
        out_specs=[pl.BlockSpec((nb, ROW_BLK, w), blk), pl.BlockSpec((nb, w, w), lambda i: (0, 0, 0))],
        out_shape=[jax.ShapeDtypeStruct((nb, nblk * ROW_BLK, w), F32), jax.ShapeDtypeStruct((nb, w, w), F32)],
        scratch_shapes=[pltpu.VMEM((nb, 8, cw), F32), pltpu.VMEM((nb, w, w), F32)],
        compiler_params=_cparams(("arbitrary",)),
        name="rwkv_prompt",
    )(rw, *prm)


def _rwkv_scan(tiles, tseq, s0, op_refs, vt_ref, acc_ref):
    g = len(tiles)
    hd = HEAD_DIM
    kk_ref, w_ref, bk_ref, kr_ref, rr_ref = op_refs
    ones_h = _head_ones(LANE)
    ones_f = jnp.ones((LANE, LANE), BF16)
    j2 = _mask01((_iota((LANE, 2 * LANE), 0) >> 6) == (_iota((LANE, 2 * LANE), 1) >> 7))
    lane_t = _iota((hd, LANE), 1)
    left = lane_t < hd
    x0 = [vt_ref[rb, p, 0:hd, :] for (rb, _, p) in tiles]
    x1 = [vt_ref[rb, p, hd:2 * hd, :] for (rb, _, p) in tiles]
    tile_rows = lambda a, i: a[i * hd:(i + 1) * hd]
    step_rows = 8

    def group(t8, s):
        base = t8 * step_rows
        blks = []
        for ref in op_refs:
            per_tile = []
            for (rb, j, p) in tiles:
                start = rb * ROW_BLK + j * tseq + base
                if not isinstance(start, int):
                    start = pl.multiple_of(start, step_rows)
                per_tile.append(ref[pl.ds(start, step_rows), pl.ds(p * LANE, LANE)])
            blks.append(per_tile)
        for u in range(step_rows):
            row = lambda q, i: blks[q][i][u:u + 1, :]
            s_t = [tile_rows(s, i) for i in range(g)]
            sa = _dotx(jnp.concatenate([s_t[i] * row(0, i) for i in range(g)], axis=0), ones_h, 2)
            msk = [lane_t == (j * tseq + base + u) for (_, j, _) in tiles]
            vsel = jnp.concatenate([jnp.where(msk[i], x0[i], 0.0) for i in range(g)]
                                   + [jnp.where(msk[i], x1[i], 0.0) for i in range(g)], axis=0)
            vc = _dotx(vsel, ones_f, 2)
            new = []
            for i in range(g):
                vcol = jnp.where(left, tile_rows(vc, i), tile_rows(vc, g + i))
                new.append(s_t[i] * row(1, i) - tile_rows(sa, i) * row(2, i) + vcol * row(3, i))
            ro = _dot(jnp.concatenate([new[i] * row(4, i) for i in range(g)], axis=0), j2)
            for i, (rb, _, p) in enumerate(tiles):
                r_i = tile_rows(ro, i)
                acc_ref[rb, p, 0:hd, :] = jnp.where(msk[i], r_i[:, :LANE], acc_ref[rb, p, 0:hd, :])
                acc_ref[rb, p, hd:2 * hd, :] = jnp.where(msk[i], r_i[:, LANE:], acc_ref[rb, p, hd:2 * hd, :])
            s = jnp.concatenate(new, axis=0)
        return s

    assert tseq % step_rows == 0
    if tseq == step_rows:
        return group(0, s0)
    return lax.fori_loop(0, tseq // step_rows, group, s0)


def _rwkv_sample_kernel(rw_ref, ovr_ref, s_in_ref, mu_ref, w0_ref, w2p_ref, a0_ref, a2p_ref, g2_ref, kkp_ref, ka_ref,
                        rk_ref, lnw_ref, lnb_ref, o_ref, s_out_ref,
                        kk_s, w_s, bk_s, kr_s, rr_s, vt_s, acc_s, *, tseq):
    w = GROUP_W
    hd = HEAD_DIM
    ones = _head_ones(w)
    nseq = ROW_BLK // tseq
    rw = rw_ref[0]
    first = (_iota((ROW_BLK, 1), 0) & (tseq - 1)) == 0
    prev = jnp.where(first, ovr_ref[0], pltpu.roll(rw, 1, 0))
    r, log_decay, k_rw, v, kk, bk, g, bonus = _rwkv_prologue(
        rw, prev, mu_ref[...], w0_ref[...], w2p_ref[...], a0_ref[...], a2p_ref[...], g2_ref[...],
        kkp_ref[...], ka_ref[...], rk_ref[...], ones)
    kk_s[...] = kk
    w_s[...] = jnp.exp(log_decay)
    bk_s[...] = bk
    kr_s[...] = k_rw
    rr_s[...] = r
    for p in range(2):
        vt_s[0, p] = v[:, p * LANE:(p + 1) * LANE].T
    acc_s[...] = jnp.zeros_like(acc_s)

    ops = (kk_s, w_s, bk_s, kr_s, rr_s)
    per = 8
    for grp in range(nseq // per):
        tiles = [(0, grp * per + jj, p) for jj in range(per) for p in range(2)]
        rows = pl.ds(grp * per * 2 * hd, per * 2 * hd)
        s_out_ref[rows, :] = _rwkv_scan(tiles, tseq, s_in_ref[rows, :], ops, vt_s, acc_s)

    o = jnp.concatenate([acc_s[0, p].T for p in range(2)], axis=1)
    o_ref[0] = _rwkv_group_norm(o, lnw_ref[...], lnb_ref[...], bonus, g, ones)


def _rwkv_sample(rw, ovr, s_in, prm, tseq):
    w = GROUP_W
    nblocks, _, cw = rw.shape
    nseq = ROW_BLK // tseq
    const = lambda i: (0, 0)
    pspecs = [pl.BlockSpec(a.shape, const) for a in prm]
    blk3 = lambda width: pl.BlockSpec((1, ROW_BLK, width), lambda i: (i, 0, 0))
    st_spec = pl.BlockSpec((nseq * 2 * HEAD_DIM, LANE), lambda i: (i, 0))
    scratch = ([pltpu.VMEM((ROW_BLK, w), F32) for _ in range(5)]
               + [pltpu.VMEM((1, 2, ROW_BLK, LANE), F32), pltpu.VMEM((1, 2, ROW_BLK, LANE), F32)])
    return pl.pallas_call(
        functools.partial(_rwkv_sample_kernel, tseq=tseq),
        grid=(nblocks,),
        in_specs=[blk3(cw), blk3(cw), st_spec] + pspecs,
        out_specs=[blk3(w), st_spec],
        out_shape=[jax.ShapeDtypeStruct((nblocks, ROW_BLK, w), F32), jax.ShapeDtypeStruct(s_in.shape, F32)],
        scratch_shapes=scratch,
        compiler_params=_cparams(("parallel",)),
        name="rwkv_sample",
    )(rw, ovr, s_in, *prm)


ROUTE_W = LANE


def _out_router_kernel(x_ref, oab_ref, oc_ref, od_ref, wo_ref, g2_ref, wr_hi_ref, wr_lo_ref, br_ref,
                       x1_ref, h_ref, gate_ref, *, n_experts, n_groups, null_rows, tm):
    w = GROUP_W
    x1 = (x_ref[...]
          + _dot(oab_ref[...], wo_ref[0:2 * w, :])
          + _dot(oc_ref[...], wo_ref[2 * w:3 * w, :])
          + _dot(od_ref[...], wo_ref[3 * w:4 * w, :]))
    if null_rows:
        rowg = pl.program_id(0) * tm + _iota((tm, 1), 0)
        null = rowg < 0
        for start in null_rows:
            null = null | ((rowg >= start) & (rowg < start + PAD_FRONT))
        x1 = jnp.where(null, 0.0, x1)
    x1_ref[...] = x1
    ms = jnp.mean(x1 * x1, axis=-1, keepdims=True)
    h = x1 * lax.rsqrt(ms + NORM_EPS) * g2_ref[...]
    h_hi = h.astype(BF16)
    h_lo = (h - h_hi.astype(F32)).astype(BF16)
    h_ref[...] = h_hi
    lg = (jnp.dot(h_hi, wr_hi_ref[...], preferred_element_type=F32)
          + jnp.dot(h_lo, wr_hi_ref[...], preferred_element_type=F32)
          + jnp.dot(h_hi, wr_lo_ref[...], preferred_element_type=F32)) + br_ref[...]
    lane = _iota(lg.shape, 1)
    big = jnp.int32(1 << 20)
    neg = jnp.float32(-jnp.inf)
    is_g = (lane >= n_experts) & (lane < n_experts + n_groups)
    gl = jnp.where(is_g, lg, neg)
    gmax = jnp.max(gl, axis=-1, keepdims=True)
    gidx = jnp.min(jnp.where(gl == gmax, lane, big), axis=-1, keepdims=True) - n_experts
    g_w = 1.0 / jnp.sum(jnp.where(is_g, jnp.exp(lg - gmax), 0.0), axis=-1, keepdims=True)
    per = n_experts // n_groups
    in_group = (lane >= gidx * per) & (lane < gidx * per + per)
    el = jnp.where(in_group, lg, neg)
    v1 = jnp.max(el, axis=-1, keepdims=True)
    i1 = jnp.min(jnp.where(el == v1, lane, big), axis=-1, keepdims=True)
    el2 = jnp.where(lane == i1, neg, el)
    v2 = jnp.max(el2, axis=-1, keepdims=True)
    i2 = jnp.min(jnp.where(el2 == v2, lane, big), axis=-1, keepdims=True)
    e21 = jnp.exp(v2 - v1)
    p1 = 1.0 / (1.0 + e21)
    p2 = e21 / (1.0 + e21)
    gate_ref[...] = jnp.where(lane == i1, p1 * g_w, 0.0) + jnp.where(lane == i2, p2 * g_w, 0.0)


def _out_router(x, oab, oc, od, wo, g2, wr_hi, wr_lo, br, n_experts, n_groups, null_rows):
    n, d = x.shape
    w = GROUP_W
    tm = 256 if n % 256 == 0 else ROW_BLK
    row = lambda i: (i, 0)
    const = lambda i: (0, 0)
    return pl.pallas_call(
        functools.partial(_out_router_kernel, n_experts=n_experts, n_groups=n_groups, null_rows=null_rows, tm=tm),
        grid=(n // tm,),
        in_specs=[pl.BlockSpec((tm, d), row), pl.BlockSpec((tm, 2 * w), row), pl.BlockSpec((tm, w), row),
                  pl.BlockSpec((tm, w), row), pl.BlockSpec(wo.shape, const), pl.BlockSpec((1, d), const),
                  pl.BlockSpec(wr_hi.shape, const), pl.BlockSpec(wr_lo.shape, const), pl.BlockSpec((1, ROUTE_W), const)],
        out_specs=[pl.BlockSpec((tm, d), row), pl.BlockSpec((tm, d), row), pl.BlockSpec((tm, ROUTE_W), row)],
        out_shape=[jax.ShapeDtypeStruct((n, d), F32), jax.ShapeDtypeStruct((n, d), BF16),
                   jax.ShapeDtypeStruct((n, ROUTE_W), F32)],
        compiler_params=_cparams(("parallel",)),
        name="out_router",
    )(x, oab, oc, od, wo, g2, wr_hi, wr_lo, br)


MOE_EXPERTS_PER_STEP = 2


def _moe_kernel(x1_ref, h_ref, gate_ref, w1_ref, w3_ref, w2_ref, y_ref):
    step = pl.program_id(1)

    @pl.when(step == 0)
    def _():
        y_ref[...] = x1_ref[...]

    h = h_ref[...]
    gates = gate_ref[...]
    lane = _iota(gates.shape, 1)
    hids = []
    for j in range(MOE_EXPERTS_PER_STEP):
        e = step * MOE_EXPERTS_PER_STEP + j
        ge = jnp.sum(jnp.where(lane == e, gates, 0.0), axis=-1, keepdims=True)
        up = jnp.dot(h, w1_ref[j].astype(BF16), preferred_element_type=F32)
        lin = jnp.dot(h, w3_ref[j].astype(BF16), preferred_element_type=F32)
        hids.append(((up * _sigmoid(up)) * lin * ge).astype(BF16))
    w2 = jnp.concatenate([w2_ref[j].astype(BF16) for j in range(MOE_EXPERTS_PER_STEP)], axis=0)
    y_ref[...] += jnp.dot(jnp.concatenate(hids, axis=1), w2, preferred_element_type=F32)


def _moe_tile(n):
    for t in (1536, 1408, 1280, 1024, 640, 512, 256, 128):
        if n % t == 0:
            return t
    raise ValueError(f"row count {n} is not a multiple of {ROW_BLK}")


def _moe(x1, h, gates, w1, w3, w2, layer):
    n, d = x1.shape
    n_experts, _, f = w1.shape[1:]
    tm = _moe_tile(n)
    eb = MOE_EXPERTS_PER_STEP
    assert n_experts % eb == 0
    row = lambda i, e: (i, 0)
    return pl.pallas_call(
        _moe_kernel,
        grid=(n // tm, n_experts // eb),
        in_specs=[pl.BlockSpec((tm, d), row), pl.BlockSpec((tm, d), row), pl.BlockSpec((tm, ROUTE_W), row),
                  pl.BlockSpec((None, eb, d, f), lambda i, e: (layer, e, 0, 0)),
                  pl.BlockSpec((None, eb, d, f), lambda i, e: (layer, e, 0, 0)),
                  pl.BlockSpec((None, eb, f, d), lambda i, e: (layer, e, 0, 0))],
        out_specs=pl.BlockSpec((tm, d), row),
        out_shape=jax.ShapeDtypeStruct((n, d), F32),
        compiler_params=_cparams(("parallel", "arbitrary")),
        name="moe",
    )(x1, h, gates, w1, w3, w2)


def _rope_tables(pos):
    half = HEAD_DIM // 2
    inv = ROPE_BASE ** (-jnp.arange(half, dtype=F32) / half)
    ang = pos.astype(F32)[:, None] * inv[None, :]
    reps = LANE // half
    return jnp.tile(jnp.cos(ang), (1, reps)), jnp.tile(jnp.sin(ang), (1, reps))


def _retention_log_gamma():
    lg = jnp.log1p(-jnp.exp2(-5.0 - jnp.arange(N_HG, dtype=F32)))
    return jnp.broadcast_to(jnp.repeat(lg, HEAD_DIM)[None, :], (ROW_BLK, GROUP_W))


def _row(v):
    return v.reshape(1, -1).astype(F32)


def _rwkv_params(p):
    w2 = p['rwkv_w2']
    a2 = p['rwkv_a2']
    w2p = jnp.concatenate([w2, jnp.zeros_like(a2)], axis=0)
    a2p = jnp.concatenate([jnp.zeros_like(w2), a2], axis=0)
    return (_row(p['rwkv_mu']), _row(p['rwkv_w0']), w2p, _row(p['rwkv_a0']), a2p, p['rwkv_g2'],
            _row(p['rwkv_kk']), _row(p['rwkv_ka']), _row(p['rwkv_rk']), _row(p['rwkv_ln_w']), _row(p['rwkv_ln_b']))


def _rwkv_state_to_tiles(s):
    n = s.shape[0]
    return s.reshape(n, 2, 2, HEAD_DIM, HEAD_DIM).transpose(0, 1, 3, 2, 4).reshape(n * 2 * HEAD_DIM, LANE)


def _rwkv_tiles_to_state(t, n):
    return t.reshape(n, 2, HEAD_DIM, 2, HEAD_DIM).transpose(0, 1, 3, 2, 4).reshape(n, N_HG, HEAD_DIM, HEAD_DIM)


def _diag_heads(s):
    return jnp.stack([s[:, h * HEAD_DIM:(h + 1) * HEAD_DIM, h * HEAD_DIM:(h + 1) * HEAD_DIM] for h in range(N_HG)], axis=1)


def kernel(x_prompt, x_sample, cache_sb_k, cache_sb_v, state_ret, state_hgrn, state_rwkv, state_rwkv_shift,
           page_table, meta_tokens, norm1, norm2, w_in, w_out, ret_norm, hgrn_lb_logits, hgrn_norm,
           sb_q_norm, sb_k_norm, sb_bias, rwkv_mu, rwkv_w0, rwkv_w2, rwkv_a0, rwkv_a2, rwkv_g2, rwkv_kk, rwkv_ka,
           rwkv_rk, rwkv_ln_w, rwkv_ln_b, moe_w_group, moe_b_group, moe_w_expert, moe_b_expert,
           moe_w1, moe_w3, moe_w2):
    bp, seq, dm = x_prompt.shape
    db, ds, _ = x_sample.shape
    depth = w_in.shape[0]
    w = GROUP_W
    assert seq % ROW_BLK == 0 and (db * ds) % ROW_BLK == 0 and ROW_BLK % ds == 0 and ds & (ds - 1) == 0
    assert w_in.shape[2] == LIN_COLS + 4 * w and cache_sb_k.shape[3] * cache_sb_k.shape[4] == w
    tp = seq + SB_QBLK
    nblk = tp // ROW_BLK
    tlen = seq + N_META
    past = page_table.shape[1] * cache_sb_k.shape[2]
    n_groups, e_per = moe_w_expert.shape[2:]
    n_experts = n_groups * e_per
    assert n_experts + n_groups <= ROUTE_W

    xp = jnp.concatenate([jnp.zeros((bp, PAD_FRONT, dm), F32),
                          jnp.broadcast_to(meta_tokens[None], (bp, N_META, dm)).astype(F32), x_prompt], axis=1)
    xp = xp.reshape(bp * tp, dm)
    xs = x_sample.reshape(db * ds, dm)
    null_rows = tuple(b * tp for b in range(bp))

    cos_p, sin_p = _rope_tables(jnp.maximum(jnp.arange(tp) - PAD_FRONT, 0))
    cos_s, sin_s = _rope_tables(jnp.tile(past + jnp.arange(ds), ROW_BLK // ds))
    lgam = _retention_log_gamma()
    cache_k = cache_sb_k.astype(BF16).reshape(cache_sb_k.shape[:3] + (w,))
    cache_v = cache_sb_v.astype(BF16).reshape(cache_sb_v.shape[:3] + (w,))
    tile_heads = lambda v: jnp.tile(v, N_HG)[None, :].astype(F32)

    prompt_rows, sample_rows = [], []
    for l in range(depth):
        wa = w_in[l][:, :LIN_COLS].astype(BF16)
        wb = w_in[l][:, LIN_COLS:].astype(BF16)
        g1 = _row(norm1[l])
        lin_args = (lgam, _row(ret_norm[l]), hgrn_lb_logits.astype(F32), _row(hgrn_norm[l]),
                    tile_heads(sb_q_norm[l]), tile_heads(sb_k_norm[l]))
        rwkv_prm = _rwkv_params(dict(rwkv_mu=rwkv_mu[l], rwkv_w0=rwkv_w0[l], rwkv_w2=rwkv_w2[l], rwkv_a0=rwkv_a0[l],
                                     rwkv_a2=rwkv_a2[l], rwkv_g2=rwkv_g2[l], rwkv_kk=rwkv_kk[l], rwkv_ka=rwkv_ka[l],
                                     rwkv_rk=rwkv_rk[l], rwkv_ln_w=rwkv_ln_w[l], rwkv_ln_b=rwkv_ln_b[l]))
        wo = w_out[l].astype(BF16)
        wr = jnp.zeros((dm, ROUTE_W), F32)
        wr = wr.at[:, :n_experts].set(moe_w_expert[l].reshape(dm, n_experts))
        wr = wr.at[:, n_experts:n_experts + n_groups].set(moe_w_group[l])
        wr_hi = wr.astype(BF16)
        wr_lo = (wr - wr_hi.astype(F32)).astype(BF16)
        br = jnp.zeros((1, ROUTE_W), F32)
        br = br.at[0, :n_experts].set(moe_b_expert[l].reshape(n_experts))
        br = br.at[0, n_experts:n_experts + n_groups].set(moe_b_group[l])
        router = (wo, _row(norm2[l]), wr_hi, wr_lo, br)

        pa, pb = _in_proj(xp, g1, wa, wb)
        oab, qs, knf, knb, vb, s_ret, s_hg = _lin_mix_prompt(pa, cos_p, sin_p, *lin_args, l, bp, nblk)
        oc = _sb_prompt(sb_bias[l].astype(F32), qs, knb, vb, bp, tp)
        od, s_rw = _rwkv_prompt(pb.reshape(bp, tp, -1), rwkv_prm, bp, nblk)
        x1, h2, gates = _out_router(xp, oab, oc, od.reshape(bp * tp, w), *router, n_experts, n_groups, null_rows)
        xp = _moe(x1, h2, gates, moe_w1, moe_w3, moe_w2, l)
        real = lambda a: a.reshape(bp, tp, -1)[:, PAD_FRONT:]
        prompt_rows.append((real(knf).reshape(bp, tlen, N_HG, HEAD_DIM),
                            real(pa[:, LIN_COLS - w:]).reshape(bp, tlen, N_HG, HEAD_DIM),
                            _diag_heads(s_ret), _diag_heads(s_hg), _diag_heads(s_rw).swapaxes(-1, -2),
                            pb.reshape(bp, tp, -1)[:, -1]))

        sa, sb = _in_proj(xs, g1, wa, wb)
        stack = lambda s: s.reshape(db, w, HEAD_DIM)
        oab, qs, knf, vf, s_ret, s_hg = _lin_mix_sample(sa, cos_s, sin_s, *lin_args, stack(state_ret[l]),
                                                        stack(state_hgrn[l]), l, ds)
        oc = _sb_sample(page_table, sb_bias[l].astype(F32), qs, knf, vf, cache_k, cache_v, l, ds)
        cw = sb.shape[-1]
        ovr = jnp.concatenate([state_rwkv_shift[l][:, None, :], jnp.zeros((db, ds - 1, cw), F32)], axis=1)
        od, s_rw = _rwkv_sample(sb.reshape(-1, ROW_BLK, cw), ovr.reshape(-1, ROW_BLK, cw),
                                _rwkv_state_to_tiles(state_rwkv[l]), rwkv_prm, ds)
        x1, h2, gates = _out_router(xs, oab, oc, od.reshape(db * ds, w), *router, n_experts, n_groups, ())
        xs = _moe(x1, h2, gates, moe_w1, moe_w3, moe_w2, l)
        sample_rows.append((knf.reshape(db, ds, N_HG, HEAD_DIM), vf.reshape(db, ds, N_HG, HEAD_DIM),
                            s_ret.reshape(db, N_HG, HEAD_DIM, HEAD_DIM), s_hg.reshape(db, N_HG, HEAD_DIM, HEAD_DIM),
                            _rwkv_tiles_to_state(s_rw, db), sb.reshape(db, ds, cw)[:, -1]))

    y_prompt = xp.reshape(bp, tp, dm)[:, SB_QBLK:]
    y_sample = xs.reshape(db, ds, dm)
    stacked_p = [jnp.stack(r) for r in zip(*prompt_rows)]
    stacked_s = [jnp.stack(r) for r in zip(*sample_rows)]
    return (y_prompt, y_sample, *stacked_p, *stacked_s)
```

```python
import functools
import math

import jax
import jax.numpy as jnp
from jax import lax
from jax.experimental import pallas as pl
from jax.experimental.pallas import tpu as pltpu

F32 = jnp.float32
BF16 = jnp.bfloat16

HEAD_DIM = 64
N_HG = 4
GROUP_W = N_HG * HEAD_DIM
N_META = 16
LANE = 128
ROW_BLK = 128
SB_QBLK = 256
PAD_FRONT = SB_QBLK - N_META
LIN_COLS = 11 * GROUP_W
NORM_EPS = 1e-6
RWKV_LN_EPS = 64e-5
ROPE_BASE = 10000.0
LOG2E = math.log2(math.e)
SB_SCALE = HEAD_DIM ** -0.5 * LOG2E
VMEM_LIMIT = 56 * 1024 * 1024


def _iota(shape, dim):
    return lax.broadcasted_iota(jnp.int32, shape, dim)


def _mask01(cond):
    return jnp.where(cond, 1.0, 0.0).astype(BF16)


def _dot(a, b):
    return jnp.dot(a.astype(BF16), b.astype(BF16), preferred_element_type=F32)


def _dot_nt(a, b):
    return lax.dot_general(a.astype(BF16), b.astype(BF16), (((1,), (1,)), ((), ())),
                           preferred_element_type=F32)


def _dot_tn(a, b):
    return lax.dot_general(a.astype(BF16), b.astype(BF16), (((0,), (0,)), ((), ())),
                           preferred_element_type=F32)


def _split(x, n):
    parts = []
    r = x
    for _ in range(n):
        h = r.astype(BF16)
        parts.append(h)
        r = r - h.astype(F32)
    return parts


def _dotx(x, m, n=2):
    out = None
    for p in _split(x, n):
        t = jnp.dot(p, m, preferred_element_type=F32)
        out = t if out is None else out + t
    return out


def _head_ones(w):
    return _mask01((_iota((w, w), 0) >> 6) == (_iota((w, w), 1) >> 6))


def _sigmoid(x):
    return 1.0 / (1.0 + jnp.exp(-x))


def _softplus(x):
    return jnp.maximum(x, 0.0) + jnp.log(1.0 + jnp.exp(-jnp.abs(x)))


def _softplus2(x):
    return jnp.maximum(x, 0.0) + jnp.log2(1.0 + jnp.exp2(-jnp.abs(x)))


def _head_rms(x, gain, ones):
    ms = _dotx(x * x, ones, 2) * (1.0 / HEAD_DIM)
    return x * lax.rsqrt(ms + NORM_EPS) * gain


def _cparams(sem, flags=None):
    return pltpu.CompilerParams(dimension_semantics=sem, vmem_limit_bytes=VMEM_LIMIT, flags=flags)


def _in_proj_kernel(x_ref, g_ref, wa_ref, wb_ref, oa_ref, ob_ref):
    x = x_ref[...]
    ms = jnp.mean(x * x, axis=-1, keepdims=True)
    h = (x * lax.rsqrt(ms + NORM_EPS) * g_ref[...]).astype(BF16)
    oa_ref[...] = jnp.dot(h, wa_ref[...], preferred_element_type=F32)
    ob_ref[...] = jnp.dot(h, wb_ref[...], preferred_element_type=F32)


def _in_proj(x, g, wa, wb, seq_rows):
    n, d = x.shape
    tm = 256 if seq_rows % 256 == 0 else ROW_BLK
    per_seq = seq_rows // tm
    return pl.pallas_call(
        _in_proj_kernel,
        grid=(n // tm,),
        in_specs=[pl.BlockSpec((tm, d), lambda i: (i, 0)),
                  pl.BlockSpec((1, d), lambda i: (0, 0)),
                  pl.BlockSpec(wa.shape, lambda i: (0, 0)),
                  pl.BlockSpec(wb.shape, lambda i: (0, 0))],
        out_specs=[pl.BlockSpec((tm, wa.shape[1]), lambda i: (i, 0)),
                   pl.BlockSpec((None, tm, wb.shape[1]), lambda i: (i // per_seq, i % per_seq, 0))],
        out_shape=[jax.ShapeDtypeStruct((n, wa.shape[1]), F32),
                   jax.ShapeDtypeStruct((n // seq_rows, seq_rows, wb.shape[1]), F32)],
        compiler_params=_cparams(("parallel",)),
        name="in_proj",
    )(x, g, wa, wb)


def _drain(steps):
    while True:
        try:
            next(steps)
        except StopIteration as stop:
            return stop.value


def _lockstep(sequences):
    while all([next(seq, False) for seq in sequences]):
        pass


def _gla_block(q, k, v, lw, states, sub, chain):
    rows, w = q.shape
    nsub = rows // sub
    shift = int(math.log2(sub))
    r_i = _iota((rows, rows), 0)
    c_i = _iota((rows, rows), 1)
    same = (r_i >> shift) == (c_i >> shift)
    tri = _mask01(same & (c_i <= r_i))
    blk = _mask01(same)
    sub_of_row = _mask01((_iota((rows, LANE), 0) >> shift) == _iota((rows, LANE), 1))
    bc = bl = dsum = None
    for p in _split(lw, 3):
        t1 = jnp.dot(tri, p, preferred_element_type=F32)
        t2 = jnp.dot(blk, p, preferred_element_type=F32)
        t3 = lax.dot_general(p, sub_of_row, (((0,), (0,)), ((), ())), preferred_element_type=F32)
        bc = t1 if bc is None else bc + t1
        bl = t2 if bl is None else bl + t2
        dsum = t3 if dsum is None else dsum + t3
    dcol = jnp.exp(dsum)
    qt = q * jnp.exp(bc)
    kh = k * jnp.exp(bl - bc)
    ones = _head_ones(w)

    local = _iota((rows, w), 0) & (sub - 1)
    o_intra = jnp.zeros((rows, w), F32)
    for dist in range(sub):
        back = (lambda x: x) if dist == 0 else (lambda x: pltpu.roll(x, dist, 0))
        e = jnp.exp(jnp.where(local >= dist, bc - back(bc), -1e30))
        p = q * back(k) * e
        o_intra = o_intra + _dot(p, ones) * back(v)

    bdmask = (_iota((w, w), 0) >> 6) == (_iota((w, w), 1) >> 6)
    rows_of = lambda i: slice(i * sub, (i + 1) * sub)
    kvs = [jnp.where(bdmask, _dot_tn(kh[rows_of(i)], v[rows_of(i)]), 0.0) for i in range(nsub)]
    if chain:
        outs = []
        s = states
        for i in range(nsub):
            outs.append(_dot(qt[rows_of(i)], s))
            yield True
            s = s * dcol[:, i:i + 1] + kvs[i]
        new_states = s
    else:
        outs = [_dot(qt[rows_of(i)], states[i]) for i in range(nsub)]
        new_states = [states[i] * dcol[:, i:i + 1] + kvs[i] for i in range(nsub)]
    o = o_intra + jnp.concatenate(outs, axis=0)
    return o, new_states


def _retention_block(q, k, v, lgam, s):
    rows, w = q.shape
    width = N_HG * rows
    t_row = _iota((rows, w), 0).astype(F32)
    qd = q * jnp.exp((t_row + 1.0) * lgam)
    kd = k * jnp.exp((rows - 1.0 - t_row) * lgam)
    left = _iota((rows, LANE), 1) < HEAD_DIM

    def split_heads(x):
        x = x.astype(BF16)
        zero = jnp.zeros_like(x)
        return jnp.concatenate([jnp.where(left, x, zero), jnp.where(left, zero, x)], axis=0)

    dist = (_iota((rows, width), 0) - (_iota((rows, width), 1) & (rows - 1))).astype(F32)
    lg_heads = jnp.concatenate([jnp.broadcast_to(lgam[:, h * HEAD_DIM:h * HEAD_DIM + 1], (rows, rows))
                                for h in range(N_HG)], axis=1)
    dmat = jnp.where(dist >= 0.0, jnp.exp(jnp.maximum(dist, 0.0) * lg_heads), 0.0)
    scores = jnp.concatenate([_dot_nt(q[:, p * LANE:(p + 1) * LANE], split_heads(k[:, p * LANE:(p + 1) * LANE]))
                              for p in range(2)], axis=1) * dmat
    o_intra = jnp.concatenate([_dot(scores[:, p * 2 * LANE:(p + 1) * 2 * LANE], split_heads(v[:, p * LANE:(p + 1) * LANE]))
                               for p in range(2)], axis=1)
    o = o_intra + _dot(qd, s)
    ones_rl = jnp.ones((rows, LANE), BF16)
    dsum = None
    for part in _split(lgam, 3):
        t3 = lax.dot_general(part, ones_rl, (((0,), (0,)), ((), ())), preferred_element_type=F32)
        dsum = t3 if dsum is None else dsum + t3
    bdmask = (_iota((w, w), 0) >> 6) == (_iota((w, w), 1) >> 6)
    s = s * jnp.exp(dsum)[:, 0:1] + jnp.where(bdmask, _dot_tn(kd, v), 0.0)
    return o, s


def _lin_mix_math(p, cos, sin, lgam, retw, lb_logits, hgw, qnw, knw, layer, st_ret, st_hg, sub, chain):
    w = GROUP_W
    qa, ka, va, ga, qb, fb, ib, gb, qc, kc, vc = [p[:, i * w:(i + 1) * w] for i in range(11)]
    rows = p.shape[0]
    ones = _head_ones(w)
    lane = _iota((rows, LANE), 1)
    first_half = (lane & (HEAD_DIM - 1)) < (HEAD_DIM // 2)

    def rope(x):
        halves = []
        for hp in range(w // LANE):
            xh = x[:, hp * LANE:(hp + 1) * LANE]
            rot = jnp.where(first_half, -pltpu.roll(xh, LANE - HEAD_DIM // 2, 1), pltpu.roll(xh, HEAD_DIM // 2, 1))
            halves.append(xh * cos + rot * sin)
        return jnp.concatenate(halves, axis=1)

    q_ret = rope(qa)
    k_ret = rope(ka) * (HEAD_DIM ** -0.5)
    if chain:
        o_ret, st_ret = _retention_block(q_ret, k_ret, va, lgam, st_ret)
    else:
        o_ret, st_ret = yield from _gla_block(q_ret, k_ret, va, lgam, st_ret, sub, chain)
    yield True
    o_a = _head_rms(o_ret, retw, ones) * (ga * _sigmoid(ga))

    lg = [lb_logits[d:d + 1, :] for d in range(lb_logits.shape[0])]
    mx = functools.reduce(jnp.maximum, lg)
    ex = [jnp.exp(row - mx) for row in lg]
    lb = sum(ex[1:layer + 1], jnp.zeros_like(mx)) / sum(ex[1:], ex[0])
    log_sig = jnp.minimum(fb, 0.0) - jnp.log1p(jnp.exp(-jnp.abs(fb)))
    t_a = jnp.broadcast_to(jnp.log(lb), fb.shape)
    t_b = jnp.log1p(-lb) + log_sig
    logf = jnp.maximum(t_a, t_b) + jnp.log1p(jnp.exp(-jnp.abs(t_a - t_b)))
    k_hg = (1.0 - lb) * _sigmoid(-fb)
    yield True
    o_hg, st_hg = yield from _gla_block(qb, k_hg, ib, logf, st_hg, sub, chain)
    o_b = _head_rms(o_hg, hgw, ones) * (gb * _sigmoid(gb))

    qn = _head_rms(qc, qnw, ones)
    kn = _head_rms(kc, knw, ones)
    return o_a, o_b, qn, kn, vc, st_ret, st_hg


def _lin_mix_prompt_kernel(p_ref, cos_ref, sin_ref, lgam_ref, retw_ref, lbl_ref, hgw_ref, qnw_ref, knw_ref,
                           oab_ref, qs_ref, knf_ref, knb_ref, vb_ref, sret_ref, shg_ref,
                           st_ret, st_hg, *, layer, sub, nseq):
    i = pl.program_id(0)

    def sequence(b):
        @pl.when(i == 0)
        def _():
            st_ret[b] = jnp.zeros(st_ret.shape[1:], F32)
            st_hg[b] = jnp.zeros(st_hg.shape[1:], F32)

        o_a, o_b, qn, kn, vc, s1, s2 = yield from _lin_mix_math(
            p_ref[b], cos_ref[...], sin_ref[...], lgam_ref[...], retw_ref[...], lbl_ref, hgw_ref[...],
            qnw_ref[...], knw_ref[...], layer, st_ret[b], st_hg[b], sub, True)
        st_ret[b] = s1
        st_hg[b] = s2
        oab_ref[b] = jnp.concatenate([o_a, o_b], axis=1)
        qs_ref[b] = (qn * SB_SCALE).astype(BF16)
        knf_ref[b] = kn
        knb_ref[b] = kn.astype(BF16)
        vb_ref[b] = vc.astype(BF16)

        @pl.when(i == pl.num_programs(0) - 1)
        def _():
            sret_ref[b] = s1
            shg_ref[b] = s2

    _lockstep([sequence(b) for b in range(nseq)])


def _lin_mix_prompt(proj, cos, sin, lgam, retw, lbl, hgw, qnw, knw, layer, nb, nblk):
    n = proj.shape[0]
    tp = n // nb
    w = GROUP_W
    blk = lambda i: (0, i, 0)
    const = lambda i: (0, 0)
    whole = lambda i: (0, 0, 0)
    outs = pl.pallas_call(
        functools.partial(_lin_mix_prompt_kernel, layer=layer, sub=16, nseq=nb),
        grid=(nblk,),
        in_specs=[pl.BlockSpec((nb, ROW_BLK, LIN_COLS), blk),
                  pl.BlockSpec((ROW_BLK, LANE), lambda i: (i, 0)),
                  pl.BlockSpec((ROW_BLK, LANE), lambda i: (i, 0)),
                  pl.BlockSpec((ROW_BLK, w), const), pl.BlockSpec((1, w), const),
                  pl.BlockSpec(lbl.shape, const), pl.BlockSpec((1, w), const),
                  pl.BlockSpec((1, w), const), pl.BlockSpec((1, w), const)],
        out_specs=[pl.BlockSpec((nb, ROW_BLK, 2 * w), blk),
                   pl.BlockSpec((nb, ROW_BLK, w), blk), pl.BlockSpec((nb, ROW_BLK, w), blk),
                   pl.BlockSpec((nb, ROW_BLK, w), blk), pl.BlockSpec((nb, ROW_BLK, w), blk),
                   pl.BlockSpec((nb, w, w), whole), pl.BlockSpec((nb, w, w), whole)],
        out_shape=[jax.ShapeDtypeStruct((nb, tp, 2 * w), F32),
                   jax.ShapeDtypeStruct((nb, tp, w), BF16), jax.ShapeDtypeStruct((nb, tp, w), F32),
                   jax.ShapeDtypeStruct((nb, tp, w), BF16), jax.ShapeDtypeStruct((nb, tp, w), BF16),
                   jax.ShapeDtypeStruct((nb, w, w), F32), jax.ShapeDtypeStruct((nb, w, w), F32)],
        scratch_shapes=[pltpu.VMEM((nb, w, w), F32), pltpu.VMEM((nb, w, w), F32)],
        compiler_params=_cparams(("arbitrary",)),
        name="lin_mix_prompt",
    )(proj.reshape(nb, tp, LIN_COLS), cos, sin, lgam, retw, lbl, hgw, qnw, knw)
    return [o.reshape(n, o.shape[-1]) for o in outs[:5]] + list(outs[5:])


def _lin_mix_sample_kernel(p_ref, cos_ref, sin_ref, lgam_ref, retw_ref, lbl_ref, hgw_ref, qnw_ref, knw_ref,
                           sret_in, shg_in,
                           oab_ref, qs_ref, knf_ref, vf_ref, sret_out, shg_out, *, layer, sub):
    w = GROUP_W
    nseq = ROW_BLK // sub
    bdmask = (_iota((w, w), 0) >> 6) == (_iota((w, w), 1) >> 6)
    rep = _mask01(_iota((HEAD_DIM, w), 0) == (_iota((HEAD_DIM, w), 1) & (HEAD_DIM - 1)))
    rep_t = _mask01((_iota((w, HEAD_DIM), 0) & (HEAD_DIM - 1)) == _iota((w, HEAD_DIM), 1))

    def expand(ref):
        return [jnp.where(bdmask, _dotx(ref[j], rep, 3), 0.0) for j in range(nseq)]

    def extract(ref, states):
        for j in range(nseq):
            ref[j] = _dotx(states[j], rep_t, 3)

    o_a, o_b, qn, kn, vc, s1, s2 = _drain(_lin_mix_math(
        p_ref[...], cos_ref[...], sin_ref[...], lgam_ref[...], retw_ref[...], lbl_ref, hgw_ref[...],
        qnw_ref[...], knw_ref[...], layer, expand(sret_in), expand(shg_in), sub, False))
    extract(sret_out, s1)
    extract(shg_out, s2)
    oab_ref[...] = jnp.concatenate([o_a, o_b], axis=1)
    qs_ref[...] = qn * SB_SCALE
    knf_ref[...] = kn
    vf_ref[...] = vc


def _lin_mix_sample(proj, cos, sin, lgam, retw, lbl, hgw, qnw, knw, s_ret, s_hg, layer, dec_seq):
    n = proj.shape[0]
    w = GROUP_W
    nseq = ROW_BLK // dec_seq
    row = lambda i: (i, 0)
    const = lambda i: (0, 0)
    st_spec = pl.BlockSpec((nseq, w, HEAD_DIM), lambda i: (i, 0, 0))
    return pl.pallas_call(
        functools.partial(_lin_mix_sample_kernel, layer=layer, sub=dec_seq),
        grid=(n // ROW_BLK,),
        in_specs=[pl.BlockSpec((ROW_BLK, LIN_COLS), row),
                  pl.BlockSpec((ROW_BLK, LANE), const), pl.BlockSpec((ROW_BLK, LANE), const),
                  pl.BlockSpec((ROW_BLK, w), const), pl.BlockSpec((1, w), const),
                  pl.BlockSpec(lbl.shape, const), pl.BlockSpec((1, w), const),
                  pl.BlockSpec((1, w), const), pl.BlockSpec((1, w), const),
                  st_spec, st_spec],
        out_specs=[pl.BlockSpec((ROW_BLK, 2 * w), row),
                   pl.BlockSpec((ROW_BLK, w), row), pl.BlockSpec((ROW_BLK, w), row),
                   pl.BlockSpec((ROW_BLK, w), row), st_spec, st_spec],
        out_shape=[jax.ShapeDtypeStruct((n, 2 * w), F32),
                   jax.ShapeDtypeStruct((n, w), F32), jax.ShapeDtypeStruct((n, w), F32),
                   jax.ShapeDtypeStruct((n, w), F32),
                   jax.ShapeDtypeStruct(s_ret.shape, F32), jax.ShapeDtypeStruct(s_hg.shape, F32)],
        compiler_params=_cparams(("parallel",)),
        name="lin_mix_sample",
    )(proj, cos, sin, lgam, retw, lbl, hgw, qnw, knw, s_ret, s_hg)


def _sb_consts(rows, tmask):
    j_i = _iota((LANE, 2 * LANE), 0)
    s_i = _iota((LANE, 2 * LANE), 1)
    ucat = _mask01((s_i >= LANE) | (j_i >= s_i))
    causal = _iota((rows, LANE), 1) < (_iota((rows, LANE), 0) & tmask)
    return ucat, causal


def _sb_prompt_kernel(bias_ref, q_ref, k_ref, v_ref, o_ref):
    i = pl.program_id(1)
    qb = SB_QBLK
    kpq = qb // ROW_BLK
    q = q_ref[...]
    left = _iota((ROW_BLK, LANE), 1) < HEAD_DIM
    zero = jnp.zeros((ROW_BLK, LANE), BF16)
    width = N_HG * ROW_BLK
    lane = _iota((qb, width), 1)
    bias_row = jnp.zeros((qb, width), F32)
    for h in range(N_HG):
        bias_row = jnp.where((lane >> 7) == h, bias_ref[h] * LOG2E, bias_row)
    sk = lane & (ROW_BLK - 1)
    tq = _iota((qb, width), 0)
    r2 = _iota((2 * LANE, 2 * LANE), 0)
    c2 = _iota((2 * LANE, 2 * LANE), 1)
    ubd = _mask01(((r2 >> 7) == (c2 >> 7)) & ((r2 & (LANE - 1)) > (c2 & (LANE - 1))))

    def split_heads(x):
        return jnp.concatenate([jnp.where(left, x, zero), jnp.where(left, zero, x)], axis=0)

    pair = lambda x, p: x[:, p * 2 * LANE:(p + 1) * 2 * LANE]

    def steps(kbs, masks, carry, acc0, acc1):
        blocks = []
        for kb in kbs:
            start = pl.multiple_of(kb * ROW_BLK, ROW_BLK)
            blocks.append((k_ref[pl.ds(start, ROW_BLK), :], v_ref[pl.ds(start, ROW_BLK), :]))
        zs = [jnp.concatenate([_dot_nt(q[:, p * LANE:(p + 1) * LANE], split_heads(kblk[:, p * LANE:(p + 1) * LANE]))
                               for p in range(2)], axis=1) + bias_row for kblk, _ in blocks]
        sp_own = [_softplus2(z) for z in zs]
        sps = [s if m is None else jnp.where(m, s, 0.0) for s, m in zip(sp_own, masks)]
        ts = []
        for sp in sps:
            spb = sp.astype(BF16)
            ts.append(jnp.concatenate([jnp.dot(pair(spb, p), ubd, preferred_element_type=F32) for p in range(2)], axis=1))
        abs_ = []
        for z, own, sp, t, m in zip(zs, sp_own, sps, ts, masks):
            a = jnp.exp2((z - own) - (carry + t))
            if m is not None:
                a = jnp.where(m, a, 0.0)
            abs_.append(a.astype(BF16))
            carry = carry + jnp.concatenate(
                [jnp.broadcast_to(jnp.sum(sp[:, h * ROW_BLK:(h + 1) * ROW_BLK], axis=-1, keepdims=True), (qb, ROW_BLK))
                 for h in range(N_HG)], axis=1)
        accs = [acc0, acc1]
        for ab, (_, vblk) in zip(abs_, blocks):
            for p in range(2):
                accs[p] = accs[p] + jnp.dot(pair(ab, p), split_heads(vblk[:, p * LANE:(p + 1) * LANE]),
                                            preferred_element_type=F32)
        return carry, accs[0], accs[1]

    zeros = jnp.zeros((qb, LANE), F32)
    state = (jnp.zeros((qb, width), F32), zeros, zeros)
    state = steps([i * kpq + d for d in reversed(range(kpq))], [sk + d * ROW_BLK < tq for d in reversed(range(kpq))],
                  *state)

    def earlier_blocks(j, c):
        return steps([(i - j) * kpq - 1 - d for d in range(kpq)], [None] * kpq, *c)

    state = lax.fori_loop(0, i, earlier_blocks, state)
    o_ref[...] = jnp.concatenate([state[1], state[2]], axis=1)


def _sb_prompt(bias, qs, kb, vb, nb, tp):
    n, w = qs.shape
    nq = tp // SB_QBLK
    return pl.pallas_call(
        _sb_prompt_kernel,
        grid=(nb, nq),
        in_specs=[pl.BlockSpec(memory_space=pltpu.SMEM),
                  pl.BlockSpec((SB_QBLK, w), lambda b, i: (b * nq + i, 0)),
                  pl.BlockSpec((tp, w), lambda b, i: (b, 0)),
                  pl.BlockSpec((tp, w), lambda b, i: (b, 0))],
        out_specs=pl.BlockSpec((SB_QBLK, w), lambda b, i: (b * nq + i, 0)),
        out_shape=jax.ShapeDtypeStruct((n, w), F32),
        compiler_params=_cparams(("parallel", "arbitrary")),
        name="sb_prompt",
    )(bias, qs, kb, vb)


def _sb_sample_kernel(pt_ref, bias_ref, q_ref, kn_ref, vn_ref, *rest, n_pages, dec_seq):
    del pt_ref
    k_pages = rest[:n_pages]
    v_pages = rest[n_pages:2 * n_pages]
    o_ref = rest[2 * n_pages]
    w = GROUP_W
    rows = N_HG * dec_seq
    q = q_ref[...]
    head_of_lane = _iota((dec_seq, w), 1) >> 6
    qbd = jnp.concatenate([jnp.where(head_of_lane == h, q, 0.0) for h in range(N_HG)], axis=0).astype(BF16)
    bias_col = jnp.concatenate([jnp.full((dec_seq, 1), bias_ref[h] * LOG2E, F32) for h in range(N_HG)], axis=0)
    ucat, causal = _sb_consts(rows, dec_seq - 1)
    pad = jnp.zeros((LANE - dec_seq, w), F32)
    k_new = jnp.concatenate([kn_ref[...], pad], axis=0)
    v_new = jnp.concatenate([vn_ref[...], pad], axis=0)
    key_blocks = [k_new] + [k_pages[p][...] for p in reversed(range(n_pages))]
    val_blocks = [v_new] + [v_pages[p][...] for p in reversed(range(n_pages))]
    zs = [_dot_nt(qbd, kb) + bias_col for kb in key_blocks]
    sps = [_softplus2(z) for z in zs]
    sps[0] = jnp.where(causal, sps[0], 0.0)
    ts = [_dotx(sp, ucat, 2) for sp in sps]
    carry = jnp.zeros((rows, LANE), F32)
    weights = []
    for z, t in zip(zs, ts):
        weights.append(jnp.exp2(z - (carry + t[:, :LANE])))
        carry = carry + t[:, LANE:]
    weights[0] = jnp.where(causal, weights[0], 0.0)
    acc = None
    for a, vb in zip(weights, val_blocks):
        term = _dot(a, vb)
        acc = term if acc is None else acc + term
    out = jnp.zeros((dec_seq, w), F32)
    for h in range(N_HG):
        out = out + jnp.where(head_of_lane == h, acc[h * dec_seq:(h + 1) * dec_seq], 0.0)
    o_ref[...] = out


def _sb_sample(page_table, bias, qs, kn, vn, cache_k, cache_v, layer, dec_seq):
    n, w = qs.shape
    db, n_pages = page_table.shape
    page = cache_k.shape[2]
    assert page == LANE and dec_seq % 8 == 0
    row = lambda b, pt: (b, 0)
    page_specs = [pl.BlockSpec((None, None, page, w), lambda b, pt, p=p: (layer, pt[b, p], 0, 0))
                  for p in range(n_pages)]
    grid_spec = pltpu.PrefetchScalarGridSpec(
        num_scalar_prefetch=1,
        grid=(db,),
        in_specs=[pl.BlockSpec(memory_space=pltpu.SMEM),
                  pl.BlockSpec((dec_seq, w), row), pl.BlockSpec((dec_seq, w), row), pl.BlockSpec((dec_seq, w), row)]
                 + page_specs + page_specs,
        out_specs=pl.BlockSpec((dec_seq, w), row),
    )
    return pl.pallas_call(
        functools.partial(_sb_sample_kernel, n_pages=n_pages, dec_seq=dec_seq),
        grid_spec=grid_spec,
        out_shape=jax.ShapeDtypeStruct((n, w), F32),
        compiler_params=_cparams(("arbitrary",)),
        name="sb_sample",
    )(page_table, bias, qs, kn, vn, *([cache_k] * n_pages), *([cache_v] * n_pages))


def _rwkv_prologue(rw, prev, mu, w0, w2p, a0, a2p, g2, kkp, ka, rk, ones):
    w = GROUP_W
    xm = rw + (prev - rw) * mu
    r = xm[:, 0:w]
    k = xm[:, w:2 * w]
    v = xm[:, 2 * w:3 * w]
    wa = xm[:, 3 * w:3 * w + LANE]
    gl = xm[:, 3 * w + LANE:]
    wd = w0 + _dot(jnp.tanh(wa), w2p)
    log_decay = -jnp.exp(-_softplus(-wd) - 0.5)
    a = _sigmoid(a0 + _dot(wa, a2p))
    g = _dot(_sigmoid(gl), g2)
    kk = k * kkp
    kk = kk * lax.rsqrt(jnp.maximum(_dotx(kk * kk, ones, 2), 1e-12))
    k_rw = k * (1.0 + (a - 1.0) * ka)
    bonus = _dotx(r * k_rw * rk, ones, 2) * v
    return r, log_decay, k_rw, v, kk, kk * a, g, bonus


RWKV_SUB = 64


def _rwkv_group_norm(o, lnw, lnb, bonus, g, ones):
    mu_h = _dotx(o, ones, 2) * (1.0 / HEAD_DIM)
    dlt = o - mu_h
    var = _dotx(dlt * dlt, ones, 2) * (1.0 / HEAD_DIM)
    return (dlt * lax.rsqrt(var + RWKV_LN_EPS) * lnw + lnb + bonus) * g


def _rwkv_chunk_kernel(rw_ref, mu_ref, w0_ref, w2p_ref, a0_ref, a2p_ref, g2_ref, kkp_ref, ka_ref, rk_ref, lnw_ref, lnb_ref,
                       o_ref, s_out_ref, carry_s, state_s, *, sub, nseq):
    prm = (mu_ref, w0_ref, w2p_ref, a0_ref, a2p_ref, g2_ref, kkp_ref, ka_ref, rk_ref, lnw_ref, lnb_ref)
    _lockstep([_rwkv_chunk_sequence(q, rw_ref, prm, o_ref, s_out_ref, carry_s, state_s, sub) for q in range(nseq)])


def _rwkv_chunk_sequence(seq, rw_ref, prm, o_ref, s_out_ref, carry_s, state_s, sub):
    mu_ref, w0_ref, w2p_ref, a0_ref, a2p_ref, g2_ref, kkp_ref, ka_ref, rk_ref, lnw_ref, lnb_ref = prm
    i = pl.program_id(0)
    w = GROUP_W
    rows = ROW_BLK
    nsub = rows // sub
    shift = int(math.log2(sub))
    ones = _head_ones(w)

    @pl.when(i == 0)
    def _():
        carry_s[seq] = jnp.zeros(carry_s.shape[1:], F32)
        state_s[seq] = jnp.zeros(state_s.shape[1:], F32)

    rw = rw_ref[seq]
    prev = jnp.where(_iota((rows, 1), 0) == 0, carry_s[seq, 0:1, :], pltpu.roll(rw, 1, 0))
    carry_s[seq, 0:1, :] = rw[rows - 1:rows, :]
    r, lw, k, v, kk, b, g, bonus = _rwkv_prologue(
        rw, prev, mu_ref[...], w0_ref[...], w2p_ref[...], a0_ref[...], a2p_ref[...], g2_ref[...],
        kkp_ref[...], ka_ref[...], rk_ref[...], ones)
    yield True

    r_i = _iota((rows, rows), 0)
    c_i = _iota((rows, rows), 1)
    same = (r_i >> shift) == (c_i >> shift)
    tri = _mask01(same & (c_i <= r_i))
    blk = _mask01(same)
    sub_of_row = _mask01((_iota((rows, LANE), 0) >> shift) == _iota((rows, LANE), 1))
    c = cl = dsum = None
    for part in _split(lw, 3):
        t1 = jnp.dot(tri, part, preferred_element_type=F32)
        t2 = jnp.dot(blk, part, preferred_element_type=F32)
        t3 = lax.dot_general(part, sub_of_row, (((0,), (0,)), ((), ())), preferred_element_type=F32)
        c = t1 if c is None else c + t1
        cl = t2 if cl is None else cl + t2
        dsum = t3 if dsum is None else dsum + t3
    yield True
    dcol = jnp.exp(dsum)
    kkd = kk * jnp.exp(c - lw)
    rd = r * jnp.exp(c)
    grow = jnp.exp(-c)
    kt = k * grow
    bt = b * grow
    tail = jnp.exp(cl - c)
    khat = k * tail
    bhat = b * tail

    left = _iota((rows, LANE), 1) < HEAD_DIM

    def split_heads(x):
        zero = jnp.zeros_like(x)
        return jnp.concatenate([jnp.where(left, x, zero), jnp.where(left, zero, x)], axis=0)

    lanes_p = lambda x, p: x[:, p * LANE:(p + 1) * LANE]
    wide = 2 * LANE
    t_i = _iota((rows, wide), 0)
    s_i = _iota((rows, wide), 1) & (rows - 1)
    same_sub = (t_i >> shift) == (s_i >> shift)
    incl = same_sub & (s_i <= t_i)
    strict = same_sub & (s_i < t_i)

    def dot3(x, y_hi, y_lo):
        x_hi, x_lo = _split(x, 2)
        nt = lambda a_, b_: lax.dot_general(a_, b_, (((1,), (1,)), ((), ())), preferred_element_type=F32)
        return nt(x_hi, y_hi) + nt(x_hi, y_lo) + nt(x_lo, y_hi)

    o_parts, y_parts, abr, abk = [], [], [], []
    for p in range(2):
        keys = jnp.concatenate([split_heads(lanes_p(kt, p)), split_heads(lanes_p(bt, p))], axis=0)
        keys_hi, keys_lo = _split(keys, 2)
        c_r = lax.dot_general(lanes_p(rd, p).astype(BF16), keys_hi, (((1,), (1,)), ((), ())),
                              preferred_element_type=F32)
        c_kk = dot3(lanes_p(kkd, p), keys_hi, keys_lo)
        akr = jnp.where(incl, c_r[:, :wide], 0.0)
        abr.append(jnp.where(incl, c_r[:, wide:], 0.0))
        akk = jnp.where(strict, c_kk[:, :wide], 0.0)
        abk.append(jnp.where(strict, c_kk[:, wide:], 0.0))
        v_heads = split_heads(lanes_p(v, p)).astype(BF16)
        o_parts.append(jnp.dot(akr.astype(BF16), v_heads, preferred_element_type=F32))
        y_parts.append(_dotx(akk, v_heads, 2))
    yield True
    o_acc = jnp.concatenate(o_parts, axis=1)
    y_intra = jnp.concatenate(y_parts, axis=1)

    def mm3(x, y):
        x_hi, x_lo = _split(x, 2)
        y_hi, y_lo = _split(y, 2)
        mm = lambda a_, b_: jnp.dot(a_, b_, preferred_element_type=F32)
        return mm(x_hi, y_hi) + mm(x_hi, y_lo) + mm(x_lo, y_hi)

    head_of_lane = _iota((rows, w), 1) >> 6
    eye = (_iota((rows, rows), 0) == _iota((rows, rows), 1)).astype(F32)
    powers = [-abk[h // 2][:, (h % 2) * rows:(h % 2 + 1) * rows] for h in range(N_HG)]
    invs = [eye + n for n in powers]
    for _ in range(shift - 1):
        powers = [mm3(n, n) for n in powers]
        yield True
        invs = [t + mm3(t, n) for t, n in zip(invs, powers)]
    yield True
    u_intra = jnp.zeros((rows, w), F32)
    g_mat = jnp.zeros((rows, w), F32)
    for h in range(N_HG):
        u_intra = u_intra + mm3(invs[h], jnp.where(head_of_lane == h, y_intra, 0.0))
        g_mat = g_mat + mm3(invs[h], jnp.where(head_of_lane == h, kkd, 0.0))
    yield True

    bdmask = (_iota((w, w), 0) >> 6) == (_iota((w, w), 1) >> 6)
    m = state_s[seq]
    us = []
    o_state = []
    for j in range(nsub):
        sl = slice(j * sub, (j + 1) * sub)
        from_state = _dot(jnp.concatenate([g_mat[sl], rd[sl]], axis=0), m)
        yield True
        u = u_intra[sl] + from_state[:sub]
        o_state.append(from_state[sub:])
        us.append(u)
        kv = _dot_tn(jnp.concatenate([khat[sl], -bhat[sl]], axis=0), jnp.concatenate([v[sl], u], axis=0))
        yield True
        m = m * dcol[:, j:j + 1] + jnp.where(bdmask, kv, 0.0)
    state_s[seq] = m
    u_all = jnp.concatenate(us, axis=0)
    o_u = jnp.concatenate([jnp.dot(abr[p].astype(BF16), split_heads(lanes_p(u_all, p)).astype(BF16),
                                   preferred_element_type=F32) for p in range(2)], axis=1)
    yield True
    o = o_acc + jnp.concatenate(o_state, axis=0) - o_u
    o_ref[seq] = _rwkv_group_norm(o, lnw_ref[...], lnb_ref[...], bonus, g, ones)

    @pl.when(i == pl.num_programs(0) - 1)
    def _():
        s_out_ref[seq] = m


def _rwkv_prompt(rw, prm, nb, nblk):
    w = GROUP_W
    cw = rw.shape[-1]
    const = lambda i: (0, 0)
    blk = lambda i: (0, i, 0)
    return pl.pallas_call(
        functools.partial(_rwkv_chunk_kernel, sub=RWKV_SUB, nseq=nb),
        grid=(nblk,),
        in_specs=[pl.BlockSpec((nb, ROW_BLK, cw), blk)] + [pl.BlockSpec(a.shape, const) for a in prm],
        out_specs=[pl.BlockSpec((nb, ROW_BLK, w), blk), pl.BlockSpec((nb, w, w), lambda i: (0, 0, 0))],
        out_shape=[jax.ShapeDtypeStruct((nb, nblk * ROW_BLK, w), F32), jax.ShapeDtypeStruct((nb, w, w), F32)],
        scratch_shapes=[pltpu.VMEM((nb, 8, cw), F32), pltpu.VMEM((nb, w, w), F32)],
        compiler_params=_cparams(("arbitrary",)),
        name="rwkv_prompt",
    )(rw, *prm)


def _rwkv_scan(tiles, tseq, s0, op_refs, vt_ref, acc_ref):
    g = len(tiles)
    hd = HEAD_DIM
    kk_ref, w_ref, bk_ref, kr_ref, rr_ref = op_refs
    ones_h = _head_ones(LANE)
    ones_f = jnp.ones((LANE, LANE), BF16)
    j2 = _mask01((_iota((LANE, 2 * LANE), 0) >> 6) == (_iota((LANE, 2 * LANE), 1) >> 7))
    lane_t = _iota((hd, LANE), 1)
    left = lane_t < hd
    x0 = [vt_ref[rb, p, 0:hd, :] for (rb, _, p) in tiles]
    x1 = [vt_ref[rb, p, hd:2 * hd, :] for (rb, _, p) in tiles]
    tile_rows = lambda a, i: a[i * hd:(i + 1) * hd]
    step_rows = 8

    def group(t8, s):
        base = t8 * step_rows
        blks = []
        for ref in op_refs:
            per_tile = []
            for (rb, j, p) in tiles:
                start = rb * ROW_BLK + j * tseq + base
                if not isinstance(start, int):
                    start = pl.multiple_of(start, step_rows)
                per_tile.append(ref[pl.ds(start, step_rows), pl.ds(p * LANE, LANE)])
            blks.append(per_tile)
        for u in range(step_rows):
            row = lambda q, i: blks[q][i][u:u + 1, :]
            s_t = [tile_rows(s, i) for i in range(g)]
            sa = _dotx(jnp.concatenate([s_t[i] * row(0, i) for i in range(g)], axis=0), ones_h, 2)
            msk = [lane_t == (j * tseq + base + u) for (_, j, _) in tiles]
            vsel = jnp.concatenate([jnp.where(msk[i], x0[i], 0.0) for i in range(g)]
                                   + [jnp.where(msk[i], x1[i], 0.0) for i in range(g)], axis=0)
            vc = _dotx(vsel, ones_f, 2)
            new = []
            for i in range(g):
                vcol = jnp.where(left, tile_rows(vc, i), tile_rows(vc, g + i))
                new.append(s_t[i] * row(1, i) - tile_rows(sa, i) * row(2, i) + vcol * row(3, i))
            ro = _dot(jnp.concatenate([new[i] * row(4, i) for i in range(g)], axis=0), j2)
            for i, (rb, _, p) in enumerate(tiles):
                r_i = tile_rows(ro, i)
                acc_ref[rb, p, 0:hd, :] = jnp.where(msk[i], r_i[:, :LANE], acc_ref[rb, p, 0:hd, :])
                acc_ref[rb, p, hd:2 * hd, :] = jnp.where(msk[i], r_i[:, LANE:], acc_ref[rb, p, hd:2 * hd, :])
            s = jnp.concatenate(new, axis=0)
        return s

    assert tseq % step_rows == 0
    if tseq == step_rows:
        return group(0, s0)
    return lax.fori_loop(0, tseq // step_rows, group, s0)


def _rwkv_sample_kernel(rw_ref, ovr_ref, s_in_ref, mu_ref, w0_ref, w2p_ref, a0_ref, a2p_ref, g2_ref, kkp_ref, ka_ref,
                        rk_ref, lnw_ref, lnb_ref, o_ref, s_out_ref,
                        kk_s, w_s, bk_s, kr_s, rr_s, vt_s, acc_s, *, tseq):
    w = GROUP_W
    hd = HEAD_DIM
    ones = _head_ones(w)
    nseq = ROW_BLK // tseq
    rw = rw_ref[0]
    first = (_iota((ROW_BLK, 1), 0) & (tseq - 1)) == 0
    prev = jnp.where(first, ovr_ref[0], pltpu.roll(rw, 1, 0))
    r, log_decay, k_rw, v, kk, bk, g, bonus = _rwkv_prologue(
        rw, prev, mu_ref[...], w0_ref[...], w2p_ref[...], a0_ref[...], a2p_ref[...], g2_ref[...],
        kkp_ref[...], ka_ref[...], rk_ref[...], ones)
    kk_s[...] = kk
    w_s[...] = jnp.exp(log_decay)
    bk_s[...] = bk
    kr_s[...] = k_rw
    rr_s[...] = r
    for p in range(2):
        vt_s[0, p] = v[:, p * LANE:(p + 1) * LANE].T
    acc_s[...] = jnp.zeros_like(acc_s)

    ops = (kk_s, w_s, bk_s, kr_s, rr_s)
    per = 8
    for grp in range(nseq // per):
        tiles = [(0, grp * per + jj, p) for jj in range(per) for p in range(2)]
        rows = pl.ds(grp * per * 2 * hd, per * 2 * hd)
        s_out_ref[rows, :] = _rwkv_scan(tiles, tseq, s_in_ref[rows, :], ops, vt_s, acc_s)

    o = jnp.concatenate([acc_s[0, p].T for p in range(2)], axis=1)
    o_ref[0] = _rwkv_group_norm(o, lnw_ref[...], lnb_ref[...], bonus, g, ones)


def _rwkv_sample(rw, ovr, s_in, prm, tseq):
    w = GROUP_W
    nblocks, _, cw = rw.shape
    nseq = ROW_BLK // tseq
    const = lambda i: (0, 0)
    pspecs = [pl.BlockSpec(a.shape, const) for a in prm]
    blk3 = lambda width: pl.BlockSpec((1, ROW_BLK, width), lambda i: (i, 0, 0))
    st_spec = pl.BlockSpec((nseq * 2 * HEAD_DIM, LANE), lambda i: (i, 0))
    scratch = ([pltpu.VMEM((ROW_BLK, w), F32) for _ in range(5)]
               + [pltpu.VMEM((1, 2, ROW_BLK, LANE), F32), pltpu.VMEM((1, 2, ROW_BLK, LANE), F32)])
    return pl.pallas_call(
        functools.partial(_rwkv_sample_kernel, tseq=tseq),
        grid=(nblocks,),
        in_specs=[blk3(cw), blk3(cw), st_spec] + pspecs,
        out_specs=[blk3(w), st_spec],
        out_shape=[jax.ShapeDtypeStruct((nblocks, ROW_BLK, w), F32), jax.ShapeDtypeStruct(s_in.shape, F32)],
        scratch_shapes=scratch,
        compiler_params=_cparams(("parallel",)),
        name="rwkv_sample",
    )(rw, ovr, s_in, *prm)


ROUTE_W = LANE


def _out_router_kernel(x_ref, oab_ref, oc_ref, od_ref, wo_ref, g2_ref, wr_hi_ref, wr_lo_ref, br_ref,
                       x1_ref, h_ref, gate_ref, *, n_experts, n_groups, null_rows, tm):
    w = GROUP_W
    x1 = (x_ref[...]
          + _dot(oab_ref[...], wo_ref[0:2 * w, :])
          + _dot(oc_ref[...], wo_ref[2 * w:3 * w, :])
          + _dot(od_ref[...], wo_ref[3 * w:4 * w, :]))
    if null_rows:
        rowg = pl.program_id(0) * tm + _iota((tm, 1), 0)
        null = rowg < 0
        for start in null_rows:
            null = null | ((rowg >= start) & (rowg < start + PAD_FRONT))
        x1 = jnp.where(null, 0.0, x1)
    x1_ref[...] = x1
    ms = jnp.mean(x1 * x1, axis=-1, keepdims=True)
    h = x1 * lax.rsqrt(ms + NORM_EPS) * g2_ref[...]
    h_hi = h.astype(BF16)
    h_lo = (h - h_hi.astype(F32)).astype(BF16)
    h_ref[...] = h_hi
    lg = (jnp.dot(h_hi, wr_hi_ref[...], preferred_element_type=F32)
          + jnp.dot(h_lo, wr_hi_ref[...], preferred_element_type=F32)
          + jnp.dot(h_hi, wr_lo_ref[...], preferred_element_type=F32)) + br_ref[...]
    lane = _iota(lg.shape, 1)
    big = jnp.int32(1 << 20)
    neg = jnp.float32(-jnp.inf)
    is_g = (lane >= n_experts) & (lane < n_experts + n_groups)
    gl = jnp.where(is_g, lg, neg)
    gmax = jnp.max(gl, axis=-1, keepdims=True)
    gidx = jnp.min(jnp.where(gl == gmax, lane, big), axis=-1, keepdims=True) - n_experts
    g_w = 1.0 / jnp.sum(jnp.where(is_g, jnp.exp(lg - gmax), 0.0), axis=-1, keepdims=True)
    per = n_experts // n_groups
    in_group = (lane >= gidx * per) & (lane < gidx * per + per)
    el = jnp.where(in_group, lg, neg)
    v1 = jnp.max(el, axis=-1, keepdims=True)
    i1 = jnp.min(jnp.where(el == v1, lane, big), axis=-1, keepdims=True)
    el2 = jnp.where(lane == i1, neg, el)
    v2 = jnp.max(el2, axis=-1, keepdims=True)
    i2 = jnp.min(jnp.where(el2 == v2, lane, big), axis=-1, keepdims=True)
    e21 = jnp.exp(v2 - v1)
    p1 = 1.0 / (1.0 + e21)
    p2 = e21 / (1.0 + e21)
    gate_ref[...] = jnp.where(lane == i1, p1 * g_w, 0.0) + jnp.where(lane == i2, p2 * g_w, 0.0)


def _out_router(x, oab, oc, od, wo, g2, wr_hi, wr_lo, br, n_experts, n_groups, null_rows):
    n, d = x.shape
    w = GROUP_W
    tm = 256 if n % 256 == 0 else ROW_BLK
    row = lambda i: (i, 0)
    const = lambda i: (0, 0)
    return pl.pallas_call(
        functools.partial(_out_router_kernel, n_experts=n_experts, n_groups=n_groups, null_rows=null_rows, tm=tm),
        grid=(n // tm,),
        in_specs=[pl.BlockSpec((tm, d), row), pl.BlockSpec((tm, 2 * w), row), pl.BlockSpec((tm, w), row),
                  pl.BlockSpec((tm, w), row), pl.BlockSpec(wo.shape, const), pl.BlockSpec((1, d), const),
                  pl.BlockSpec(wr_hi.shape, const), pl.BlockSpec(wr_lo.shape, const), pl.BlockSpec((1, ROUTE_W), const)],
        out_specs=[pl.BlockSpec((tm, d), row), pl.BlockSpec((tm, d), row), pl.BlockSpec((tm, ROUTE_W), row)],
        out_shape=[jax.ShapeDtypeStruct((n, d), F32), jax.ShapeDtypeStruct((n, d), BF16),
                   jax.ShapeDtypeStruct((n, ROUTE_W), F32)],
        compiler_params=_cparams(("parallel",)),
        name="out_router",
    )(x, oab, oc, od, wo, g2, wr_hi, wr_lo, br)


MOE_EXPERTS_PER_STEP = 2


def _moe_kernel(x1_ref, h_ref, gate_ref, w1_ref, w3_ref, w2_ref, y_ref):
    step = pl.program_id(1)

    @pl.when(step == 0)
    def _():
        y_ref[...] = x1_ref[...]

    h = h_ref[...]
    gates = gate_ref[...]
    lane = _iota(gates.shape, 1)
    hids = []
    for j in range(MOE_EXPERTS_PER_STEP):
        e = step * MOE_EXPERTS_PER_STEP + j
        ge = jnp.sum(jnp.where(lane == e, gates, 0.0), axis=-1, keepdims=True)
        up = jnp.dot(h, w1_ref[j].astype(BF16), preferred_element_type=F32)
        lin = jnp.dot(h, w3_ref[j].astype(BF16), preferred_element_type=F32)
        hids.append(((up * _sigmoid(up)) * lin * ge).astype(BF16))
    w2 = jnp.concatenate([w2_ref[j].astype(BF16) for j in range(MOE_EXPERTS_PER_STEP)], axis=0)
    y_ref[...] += jnp.dot(jnp.concatenate(hids, axis=1), w2, preferred_element_type=F32)


def _moe_tile(n):
    for t in (1536, 1408, 1280, 1024, 640, 512, 256, 128):
        if n % t == 0:
            return t
    raise ValueError(f"row count {n} is not a multiple of {ROW_BLK}")


def _moe(x1, h, gates, w1, w3, w2, layer):
    n, d = x1.shape
    n_experts, _, f = w1.shape[1:]
    tm = _moe_tile(n)
    eb = MOE_EXPERTS_PER_STEP
    assert n_experts % eb == 0
    row = lambda i, e: (i, 0)
    return pl.pallas_call(
        _moe_kernel,
        grid=(n // tm, n_experts // eb),
        in_specs=[pl.BlockSpec((tm, d), row), pl.BlockSpec((tm, d), row), pl.BlockSpec((tm, ROUTE_W), row),
                  pl.BlockSpec((None, eb, d, f), lambda i, e: (layer, e, 0, 0)),
                  pl.BlockSpec((None, eb, d, f), lambda i, e: (layer, e, 0, 0)),
                  pl.BlockSpec((None, eb, f, d), lambda i, e: (layer, e, 0, 0))],
        out_specs=pl.BlockSpec((tm, d), row),
        out_shape=jax.ShapeDtypeStruct((n, d), F32),
        compiler_params=_cparams(("parallel", "arbitrary")),
        name="moe",
    )(x1, h, gates, w1, w3, w2)


def _rope_tables(pos):
    half = HEAD_DIM // 2
    inv = ROPE_BASE ** (-jnp.arange(half, dtype=F32) / half)
    ang = pos.astype(F32)[:, None] * inv[None, :]
    reps = LANE // half
    return jnp.tile(jnp.cos(ang), (1, reps)), jnp.tile(jnp.sin(ang), (1, reps))


def _retention_log_gamma():
    lg = jnp.log1p(-jnp.exp2(-5.0 - jnp.arange(N_HG, dtype=F32)))
    return jnp.broadcast_to(jnp.repeat(lg, HEAD_DIM)[None, :], (ROW_BLK, GROUP_W))


def _row(v):
    return v.reshape(1, -1).astype(F32)


def _rwkv_params(p):
    w2 = p['rwkv_w2']
    a2 = p['rwkv_a2']
    w2p = jnp.concatenate([w2, jnp.zeros_like(a2)], axis=0)
    a2p = jnp.concatenate([jnp.zeros_like(w2), a2], axis=0)
    return (_row(p['rwkv_mu']), _row(p['rwkv_w0']), w2p, _row(p['rwkv_a0']), a2p, p['rwkv_g2'],
            _row(p['rwkv_kk']), _row(p['rwkv_ka']), _row(p['rwkv_rk']), _row(p['rwkv_ln_w']), _row(p['rwkv_ln_b']))


def _rwkv_state_to_tiles(s):
    n = s.shape[0]
    return s.reshape(n, 2, 2, HEAD_DIM, HEAD_DIM).transpose(0, 1, 3, 2, 4).reshape(n * 2 * HEAD_DIM, LANE)


def _rwkv_tiles_to_state(t, n):
    return t.reshape(n, 2, HEAD_DIM, 2, HEAD_DIM).transpose(0, 1, 3, 2, 4).reshape(n, N_HG, HEAD_DIM, HEAD_DIM)


def _diag_heads(s):
    return jnp.stack([s[:, h * HEAD_DIM:(h + 1) * HEAD_DIM, h * HEAD_DIM:(h + 1) * HEAD_DIM] for h in range(N_HG)], axis=1)


def kernel(x_prompt, x_sample, cache_sb_k, cache_sb_v, state_ret, state_hgrn, state_rwkv, state_rwkv_shift,
           page_table, meta_tokens, norm1, norm2, w_in, w_out, ret_norm, hgrn_lb_logits, hgrn_norm,
           sb_q_norm, sb_k_norm, sb_bias, rwkv_mu, rwkv_w0, rwkv_w2, rwkv_a0, rwkv_a2, rwkv_g2, rwkv_kk, rwkv_ka,
           rwkv_rk, rwkv_ln_w, rwkv_ln_b, moe_w_group, moe_b_group, moe_w_expert, moe_b_expert,
           moe_w1, moe_w3, moe_w2):
    bp, seq, dm = x_prompt.shape
    db, ds, _ = x_sample.shape
    depth = w_in.shape[0]
    w = GROUP_W
    assert seq % ROW_BLK == 0 and (db * ds) % ROW_BLK == 0 and ROW_BLK % ds == 0 and ds & (ds - 1) == 0
    assert w_in.shape[2] == LIN_COLS + 4 * w and cache_sb_k.shape[3] * cache_sb_k.shape[4] == w
    tp = seq + SB_QBLK
    nblk = tp // ROW_BLK
    tlen = seq + N_META
    past = page_table.shape[1] * cache_sb_k.shape[2]
    n_groups, e_per = moe_w_expert.shape[2:]
    n_experts = n_groups * e_per
    assert n_experts + n_groups <= ROUTE_W

    xp = jnp.concatenate([jnp.zeros((bp, PAD_FRONT, dm), F32),
                          jnp.broadcast_to(meta_tokens[None], (bp, N_META, dm)).astype(F32), x_prompt], axis=1)
    xp = xp.reshape(bp * tp, dm)
    xs = x_sample.reshape(db * ds, dm)
    null_rows = tuple(b * tp for b in range(bp))

    cos_p, sin_p = _rope_tables(jnp.maximum(jnp.arange(tp) - PAD_FRONT, 0))
    cos_s, sin_s = _rope_tables(jnp.tile(past + jnp.arange(ds), ROW_BLK // ds))
    lgam = _retention_log_gamma()
    cache_k = cache_sb_k.reshape(cache_sb_k.shape[:3] + (w,))
    cache_v = cache_sb_v.reshape(cache_sb_v.shape[:3] + (w,))
    tile_heads = lambda v: jnp.tile(v, N_HG)[None, :].astype(F32)

    prompt_rows, sample_rows = [], []
    for l in range(depth):
        wa = w_in[l][:, :LIN_COLS].astype(BF16)
        wb = w_in[l][:, LIN_COLS:].astype(BF16)
        g1 = _row(norm1[l])
        lin_args = (lgam, _row(ret_norm[l]), hgrn_lb_logits.astype(F32), _row(hgrn_norm[l]),
                    tile_heads(sb_q_norm[l]), tile_heads(sb_k_norm[l]))
        rwkv_prm = _rwkv_params(dict(rwkv_mu=rwkv_mu[l], rwkv_w0=rwkv_w0[l], rwkv_w2=rwkv_w2[l], rwkv_a0=rwkv_a0[l],
                                     rwkv_a2=rwkv_a2[l], rwkv_g2=rwkv_g2[l], rwkv_kk=rwkv_kk[l], rwkv_ka=rwkv_ka[l],
                                     rwkv_rk=rwkv_rk[l], rwkv_ln_w=rwkv_ln_w[l], rwkv_ln_b=rwkv_ln_b[l]))
        wo = w_out[l].astype(BF16)
        wr = jnp.zeros((dm, ROUTE_W), F32)
        wr = wr.at[:, :n_experts].set(moe_w_expert[l].reshape(dm, n_experts))
        wr = wr.at[:, n_experts:n_experts + n_groups].set(moe_w_group[l])
        wr_hi = wr.astype(BF16)
        wr_lo = (wr - wr_hi.astype(F32)).astype(BF16)
        br = jnp.zeros((1, ROUTE_W), F32)
        br = br.at[0, :n_experts].set(moe_b_expert[l].reshape(n_experts))
        br = br.at[0, n_experts:n_experts + n_groups].set(moe_b_group[l])
        router = (wo, _row(norm2[l]), wr_hi, wr_lo, br)

        pa, pb = _in_proj(xp, g1, wa, wb, tp)
        oab, qs, knf, knb, vb, s_ret, s_hg = _lin_mix_prompt(pa, cos_p, sin_p, *lin_args, l, bp, nblk)
        oc = _sb_prompt(sb_bias[l].astype(F32), qs, knb, vb, bp, tp)
        od, s_rw = _rwkv_prompt(pb, rwkv_prm, bp, nblk)
        x1, h2, gates = _out_router(xp, oab, oc, od.reshape(bp * tp, w), *router, n_experts, n_groups, null_rows)
        xp = _moe(x1, h2, gates, moe_w1, moe_w3, moe_w2, l)
        real = lambda a: a.reshape(bp, tp, -1)[:, PAD_FRONT:]
        prompt_rows.append((real(knf).reshape(bp, tlen, N_HG, HEAD_DIM),
                            real(pa[:, LIN_COLS - w:]).reshape(bp, tlen, N_HG, HEAD_DIM),
                            _diag_heads(s_ret), _diag_heads(s_hg), _diag_heads(s_rw).swapaxes(-1, -2),
                            pb[:, -1]))

        sa, sb = _in_proj(xs, g1, wa, wb, ROW_BLK)
        stack = lambda s: s.reshape(db, w, HEAD_DIM)
        oab, qs, knf, vf, s_ret, s_hg = _lin_mix_sample(sa, cos_s, sin_s, *lin_args, stack(state_ret[l]),
                                                        stack(state_hgrn[l]), l, ds)
        oc = _sb_sample(page_table, sb_bias[l].astype(F32), qs, knf, vf, cache_k, cache_v, l, ds)
        cw = sb.shape[-1]
        ovr = jnp.concatenate([state_rwkv_shift[l][:, None, :], jnp.zeros((db, ds - 1, cw), F32)], axis=1)
        od, s_rw = _rwkv_sample(sb, ovr.reshape(-1, ROW_BLK, cw),
                                _rwkv_state_to_tiles(state_rwkv[l]), rwkv_prm, ds)
        x1, h2, gates = _out_router(xs, oab, oc, od.reshape(db * ds, w), *router, n_experts, n_groups, ())
        xs = _moe(x1, h2, gates, moe_w1, moe_w3, moe_w2, l)
        sample_rows.append((knf.reshape(db, ds, N_HG, HEAD_DIM), vf.reshape(db, ds, N_HG, HEAD_DIM),
                            s_ret.reshape(db, N_HG, HEAD_DIM, HEAD_DIM), s_hg.reshape(db, N_HG, HEAD_DIM, HEAD_DIM),
                            _rwkv_tiles_to_state(s_rw, db), sb.reshape(db, ds, cw)[:, -1]))

    y_prompt = xp.reshape(bp, tp, dm)[:, SB_QBLK:]
    y_sample = xs.reshape(db, ds, dm)
    stacked_p = [jnp.stack(r) for r in zip(*prompt_rows)]
    stacked_s = [jnp.stack(r) for r in zip(*sample_rows)]
    return (y_prompt, y_sample, *stacked_p, *stacked_s)
```

```python
import functools
import math

import jax
import jax.numpy as jnp
from jax import lax
from jax.experimental import pallas as pl
from jax.experimental.pallas import tpu as pltpu

F32 = jnp.float32
BF16 = jnp.bfloat16

HEAD_DIM = 64
N_HG = 4
GROUP_W = N_HG * HEAD_DIM
N_META = 16
LANE = 128
ROW_BLK = 128
SB_QBLK = 256
PAD_FRONT = SB_QBLK - N_META
LIN_COLS = 11 * GROUP_W
NORM_EPS = 1e-6
RWKV_LN_EPS = 64e-5
ROPE_BASE = 10000.0
LOG2E = math.log2(math.e)
SB_SCALE = HEAD_DIM ** -0.5 * LOG2E
VMEM_LIMIT = 56 * 1024 * 1024


def _iota(shape, dim):
    return lax.broadcasted_iota(jnp.int32, shape, dim)


def _mask01(cond):
    return jnp.where(cond, 1.0, 0.0).astype(BF16)


def _dot(a, b):
    return jnp.dot(a.astype(BF16), b.astype(BF16), preferred_element_type=F32)


def _dot_nt(a, b):
    return lax.dot_general(a.astype(BF16), b.astype(BF16), (((1,), (1,)), ((), ())),
                           preferred_element_type=F32)


def _dot_tn(a, b):
    return lax.dot_general(a.astype(BF16), b.astype(BF16), (((0,), (0,)), ((), ())),
                           preferred_element_type=F32)


def _split(x, n):
    parts = []
    r = x
    for _ in range(n):
        h = r.astype(BF16)
        parts.append(h)
        r = r - h.astype(F32)
    return parts


def _dotx(x, m, n=2):
    out = None
    for p in _split(x, n):
        t = jnp.dot(p, m, preferred_element_type=F32)
        out = t if out is None else out + t
    return out


def _head_ones(w):
    return _mask01((_iota((w, w), 0) >> 6) == (_iota((w, w), 1) >> 6))


def _sigmoid(x):
    return 1.0 / (1.0 + jnp.exp(-x))


def _softplus(x):
    return jnp.maximum(x, 0.0) + jnp.log(1.0 + jnp.exp(-jnp.abs(x)))


def _softplus2(x):
    return jnp.maximum(x, 0.0) + jnp.log2(1.0 + jnp.exp2(-jnp.abs(x)))


def _head_rms(x, gain, ones):
    ms = _dotx(x * x, ones, 2) * (1.0 / HEAD_DIM)
    return x * lax.rsqrt(ms + NORM_EPS) * gain


def _cparams(sem, flags=None):
    return pltpu.CompilerParams(dimension_semantics=sem, vmem_limit_bytes=VMEM_LIMIT, flags=flags)


def _in_proj_kernel(x_ref, g_ref, wa_ref, wb_ref, oa_ref, ob_ref):
    x = x_ref[...]
    ms = jnp.mean(x * x, axis=-1, keepdims=True)
    h = (x * lax.rsqrt(ms + NORM_EPS) * g_ref[...]).astype(BF16)
    oa_ref[...] = jnp.dot(h, wa_ref[...], preferred_element_type=F32)
    ob_ref[...] = jnp.dot(h, wb_ref[...], preferred_element_type=F32)


def _in_proj(x, g, wa, wb, seq_rows):
    n, d = x.shape
    tm = 256 if seq_rows % 256 == 0 else ROW_BLK
    per_seq = seq_rows // tm
    return pl.pallas_call(
        _in_proj_kernel,
        grid=(n // tm,),
        in_specs=[pl.BlockSpec((tm, d), lambda i: (i, 0)),
                  pl.BlockSpec((1, d), lambda i: (0, 0)),
                  pl.BlockSpec(wa.shape, lambda i: (0, 0)),
                  pl.BlockSpec(wb.shape, lambda i: (0, 0))],
        out_specs=[pl.BlockSpec((tm, wa.shape[1]), lambda i: (i, 0)),
                   pl.BlockSpec((None, tm, wb.shape[1]), lambda i: (i // per_seq, i % per_seq, 0))],
        out_shape=[jax.ShapeDtypeStruct((n, wa.shape[1]), F32),
                   jax.ShapeDtypeStruct((n // seq_rows, seq_rows, wb.shape[1]), F32)],
        compiler_params=_cparams(("parallel",)),
        name="in_proj",
    )(x, g, wa, wb)


def _drain(steps):
    while True:
        try:
            next(steps)
        except StopIteration as stop:
            return stop.value


def _lockstep(sequences):
    while all([next(seq, False) for seq in sequences]):
        pass


def _gla_block(q, k, v, lw, states, sub, chain):
    rows, w = q.shape
    nsub = rows // sub
    shift = int(math.log2(sub))
    r_i = _iota((rows, rows), 0)
    c_i = _iota((rows, rows), 1)
    same = (r_i >> shift) == (c_i >> shift)
    tri = _mask01(same & (c_i <= r_i))
    blk = _mask01(same)
    sub_of_row = _mask01((_iota((rows, LANE), 0) >> shift) == _iota((rows, LANE), 1))
    bc = bl = dsum = None
    for p in _split(lw, 3):
        t1 = jnp.dot(tri, p, preferred_element_type=F32)
        t2 = jnp.dot(blk, p, preferred_element_type=F32)
        t3 = lax.dot_general(p, sub_of_row, (((0,), (0,)), ((), ())), preferred_element_type=F32)
        bc = t1 if bc is None else bc + t1
        bl = t2 if bl is None else bl + t2
        dsum = t3 if dsum is None else dsum + t3
    dcol = jnp.exp(dsum)
    qt = q * jnp.exp(bc)
    kh = k * jnp.exp(bl - bc)
    ones = _head_ones(w)

    local = _iota((rows, w), 0) & (sub - 1)
    o_intra = jnp.zeros((rows, w), F32)
    for dist in range(sub):
        back = (lambda x: x) if dist == 0 else (lambda x: pltpu.roll(x, dist, 0))
        e = jnp.exp(jnp.where(local >= dist, bc - back(bc), -1e30))
        p = q * back(k) * e
        o_intra = o_intra + _dot(p, ones) * back(v)

    bdmask = (_iota((w, w), 0) >> 6) == (_iota((w, w), 1) >> 6)
    rows_of = lambda i: slice(i * sub, (i + 1) * sub)
    kvs = [jnp.where(bdmask, _dot_tn(kh[rows_of(i)], v[rows_of(i)]), 0.0) for i in range(nsub)]
    if chain:
        outs = []
        s = states
        for i in range(nsub):
            outs.append(_dot(qt[rows_of(i)], s))
            yield True
            s = s * dcol[:, i:i + 1] + kvs[i]
        new_states = s
    else:
        outs = [_dot(qt[rows_of(i)], states[i]) for i in range(nsub)]
        new_states = [states[i] * dcol[:, i:i + 1] + kvs[i] for i in range(nsub)]
    o = o_intra + jnp.concatenate(outs, axis=0)
    return o, new_states


def _retention_block(q, k, v, lgam, s):
    rows, w = q.shape
    width = N_HG * rows
    t_row = _iota((rows, w), 0).astype(F32)
    qd = q * jnp.exp((t_row + 1.0) * lgam)
    kd = k * jnp.exp((rows - 1.0 - t_row) * lgam)
    left = _iota((rows, LANE), 1) < HEAD_DIM

    def split_heads(x):
        x = x.astype(BF16)
        zero = jnp.zeros_like(x)
        return jnp.concatenate([jnp.where(left, x, zero), jnp.where(left, zero, x)], axis=0)

    dist = (_iota((rows, width), 0) - (_iota((rows, width), 1) & (rows - 1))).astype(F32)
    lg_heads = jnp.concatenate([jnp.broadcast_to(lgam[:, h * HEAD_DIM:h * HEAD_DIM + 1], (rows, rows))
                                for h in range(N_HG)], axis=1)
    dmat = jnp.where(dist >= 0.0, jnp.exp(jnp.maximum(dist, 0.0) * lg_heads), 0.0)
    scores = jnp.concatenate([_dot_nt(q[:, p * LANE:(p + 1) * LANE], split_heads(k[:, p * LANE:(p + 1) * LANE]))
                              for p in range(2)], axis=1) * dmat
    o_intra = jnp.concatenate([_dot(scores[:, p * 2 * LANE:(p + 1) * 2 * LANE], split_heads(v[:, p * LANE:(p + 1) * LANE]))
                               for p in range(2)], axis=1)
    o = o_intra + _dot(qd, s)
    ones_rl = jnp.ones((rows, LANE), BF16)
    dsum = None
    for part in _split(lgam, 3):
        t3 = lax.dot_general(part, ones_rl, (((0,), (0,)), ((), ())), preferred_element_type=F32)
        dsum = t3 if dsum is None else dsum + t3
    bdmask = (_iota((w, w), 0) >> 6) == (_iota((w, w), 1) >> 6)
    s = s * jnp.exp(dsum)[:, 0:1] + jnp.where(bdmask, _dot_tn(kd, v), 0.0)
    return o, s


def _lin_mix_math(p, cos, sin, lgam, retw, lb_logits, hgw, qnw, knw, layer, st_ret, st_hg, sub, chain):
    w = GROUP_W
    qa, ka, va, ga, qb, fb, ib, gb, qc, kc, vc = [p[:, i * w:(i + 1) * w] for i in range(11)]
    rows = p.shape[0]
    ones = _head_ones(w)
    lane = _iota((rows, LANE), 1)
    first_half = (lane & (HEAD_DIM - 1)) < (HEAD_DIM // 2)

    def rope(x):
        halves = []
        for hp in range(w // LANE):
            xh = x[:, hp * LANE:(hp + 1) * LANE]
            rot = jnp.where(first_half, -pltpu.roll(xh, LANE - HEAD_DIM // 2, 1), pltpu.roll(xh, HEAD_DIM // 2, 1))
            halves.append(xh * cos + rot * sin)
        return jnp.concatenate(halves, axis=1)

    q_ret = rope(qa)
    k_ret = rope(ka) * (HEAD_DIM ** -0.5)
    if chain:
        o_ret, st_ret = _retention_block(q_ret, k_ret, va, lgam, st_ret)
    else:
        o_ret, st_ret = yield from _gla_block(q_ret, k_ret, va, lgam, st_ret, sub, chain)
    yield True
    o_a = _head_rms(o_ret, retw, ones) * (ga * _sigmoid(ga))

    lg = [lb_logits[d:d + 1, :] for d in range(lb_logits.shape[0])]
    mx = functools.reduce(jnp.maximum, lg)
    ex = [jnp.exp(row - mx) for row in lg]
    lb = sum(ex[1:layer + 1], jnp.zeros_like(mx)) / sum(ex[1:], ex[0])
    log_sig = jnp.minimum(fb, 0.0) - jnp.log1p(jnp.exp(-jnp.abs(fb)))
    t_a = jnp.broadcast_to(jnp.log(lb), fb.shape)
    t_b = jnp.log1p(-lb) + log_sig
    logf = jnp.maximum(t_a, t_b) + jnp.log1p(jnp.exp(-jnp.abs(t_a - t_b)))
    k_hg = (1.0 - lb) * _sigmoid(-fb)
    yield True
    o_hg, st_hg = yield from _gla_block(qb, k_hg, ib, logf, st_hg, sub, chain)
    o_b = _head_rms(o_hg, hgw, ones) * (gb * _sigmoid(gb))

    qn = _head_rms(qc, qnw, ones)
    kn = _head_rms(kc, knw, ones)
    return o_a, o_b, qn, kn, vc, st_ret, st_hg


def _lin_mix_prompt_kernel(p_ref, cos_ref, sin_ref, lgam_ref, retw_ref, lbl_ref, hgw_ref, qnw_ref, knw_ref,
                           oab_ref, qs_ref, knf_ref, knb_ref, vb_ref, sret_ref, shg_ref,
                           st_ret, st_hg, *, layer, sub, nseq):
    i = pl.program_id(0)

    def sequence(b):
        @pl.when(i == 0)
        def _():
            st_ret[b] = jnp.zeros(st_ret.shape[1:], F32)
            st_hg[b] = jnp.zeros(st_hg.shape[1:], F32)

        o_a, o_b, qn, kn, vc, s1, s2 = yield from _lin_mix_math(
            p_ref[b], cos_ref[...], sin_ref[...], lgam_ref[...], retw_ref[...], lbl_ref, hgw_ref[...],
            qnw_ref[...], knw_ref[...], layer, st_ret[b], st_hg[b], sub, True)
        st_ret[b] = s1
        st_hg[b] = s2
        oab_ref[b] = jnp.concatenate([o_a, o_b], axis=1)
        qs_ref[b] = (qn * SB_SCALE).astype(BF16)
        knf_ref[b] = kn
        knb_ref[b] = kn.astype(BF16)
        vb_ref[b] = vc.astype(BF16)

        @pl.when(i == pl.num_programs(0) - 1)
        def _():
            sret_ref[b] = s1
            shg_ref[b] = s2

    _lockstep([sequence(b) for b in range(nseq)])


def _lin_mix_prompt(proj, cos, sin, lgam, retw, lbl, hgw, qnw, knw, layer, nb, nblk):
    n = proj.shape[0]
    tp = n // nb
    w = GROUP_W
    blk = lambda i: (0, i, 0)
    const = lambda i: (0, 0)
    whole = lambda i: (0, 0, 0)
    outs = pl.pallas_call(
        functools.partial(_lin_mix_prompt_kernel, layer=layer, sub=16, nseq=nb),
        grid=(nblk,),
        in_specs=[pl.BlockSpec((nb, ROW_BLK, LIN_COLS), blk),
                  pl.BlockSpec((ROW_BLK, LANE), lambda i: (i, 0)),
                  pl.BlockSpec((ROW_BLK, LANE), lambda i: (i, 0)),
                  pl.BlockSpec((ROW_BLK, w), const), pl.BlockSpec((1, w), const),
                  pl.BlockSpec(lbl.shape, const), pl.BlockSpec((1, w), const),
                  pl.BlockSpec((1, w), const), pl.BlockSpec((1, w), const)],
        out_specs=[pl.BlockSpec((nb, ROW_BLK, 2 * w), blk),
                   pl.BlockSpec((nb, ROW_BLK, w), blk), pl.BlockSpec((nb, ROW_BLK, w), blk),
                   pl.BlockSpec((nb, ROW_BLK, w), blk), pl.BlockSpec((nb, ROW_BLK, w), blk),
                   pl.BlockSpec((nb, w, w), whole), pl.BlockSpec((nb, w, w), whole)],
        out_shape=[jax.ShapeDtypeStruct((nb, tp, 2 * w), F32),
                   jax.ShapeDtypeStruct((nb, tp, w), BF16), jax.ShapeDtypeStruct((nb, tp, w), F32),
                   jax.ShapeDtypeStruct((nb, tp, w), BF16), jax.ShapeDtypeStruct((nb, tp, w), BF16),
                   jax.ShapeDtypeStruct((nb, w, w), F32), jax.ShapeDtypeStruct((nb, w, w), F32)],
        scratch_shapes=[pltpu.VMEM((nb, w, w), F32), pltpu.VMEM((nb, w, w), F32)],
        compiler_params=_cparams(("arbitrary",)),
        name="lin_mix_prompt",
    )(proj.reshape(nb, tp, LIN_COLS), cos, sin, lgam, retw, lbl, hgw, qnw, knw)
    return [o.reshape(n, o.shape[-1]) for o in outs[:5]] + list(outs[5:])


def _lin_mix_sample_kernel(p_ref, cos_ref, sin_ref, lgam_ref, retw_ref, lbl_ref, hgw_ref, qnw_ref, knw_ref,
                           sret_in, shg_in,
                           oab_ref, qs_ref, knf_ref, vf_ref, sret_out, shg_out, *, layer, sub):
    w = GROUP_W
    nseq = ROW_BLK // sub
    bdmask = (_iota((w, w), 0) >> 6) == (_iota((w, w), 1) >> 6)
    rep = _mask01(_iota((HEAD_DIM, w), 0) == (_iota((HEAD_DIM, w), 1) & (HEAD_DIM - 1)))
    rep_t = _mask01((_iota((w, HEAD_DIM), 0) & (HEAD_DIM - 1)) == _iota((w, HEAD_DIM), 1))

    def expand(ref):
        return [jnp.where(bdmask, _dotx(ref[j], rep, 3), 0.0) for j in range(nseq)]

    def extract(ref, states):
        for j in range(nseq):
            ref[j] = _dotx(states[j], rep_t, 3)

    o_a, o_b, qn, kn, vc, s1, s2 = _drain(_lin_mix_math(
        p_ref[...], cos_ref[...], sin_ref[...], lgam_ref[...], retw_ref[...], lbl_ref, hgw_ref[...],
        qnw_ref[...], knw_ref[...], layer, expand(sret_in), expand(shg_in), sub, False))
    extract(sret_out, s1)
    extract(shg_out, s2)
    oab_ref[...] = jnp.concatenate([o_a, o_b], axis=1)
    qs_ref[...] = qn * SB_SCALE
    knf_ref[...] = kn
    vf_ref[...] = vc


def _lin_mix_sample(proj, cos, sin, lgam, retw, lbl, hgw, qnw, knw, s_ret, s_hg, layer, dec_seq):
    n = proj.shape[0]
    w = GROUP_W
    nseq = ROW_BLK // dec_seq
    row = lambda i: (i, 0)
    const = lambda i: (0, 0)
    st_spec = pl.BlockSpec((nseq, w, HEAD_DIM), lambda i: (i, 0, 0))
    return pl.pallas_call(
        functools.partial(_lin_mix_sample_kernel, layer=layer, sub=dec_seq),
        grid=(n // ROW_BLK,),
        in_specs=[pl.BlockSpec((ROW_BLK, LIN_COLS), row),
                  pl.BlockSpec((ROW_BLK, LANE), const), pl.BlockSpec((ROW_BLK, LANE), const),
                  pl.BlockSpec((ROW_BLK, w), const), pl.BlockSpec((1, w), const),
                  pl.BlockSpec(lbl.shape, const), pl.BlockSpec((1, w), const),
                  pl.BlockSpec((1, w), const), pl.BlockSpec((1, w), const),
                  st_spec, st_spec],
        out_specs=[pl.BlockSpec((ROW_BLK, 2 * w), row),
                   pl.BlockSpec((ROW_BLK, w), row), pl.BlockSpec((ROW_BLK, w), row),
                   pl.BlockSpec((ROW_BLK, w), row), st_spec, st_spec],
        out_shape=[jax.ShapeDtypeStruct((n, 2 * w), F32),
                   jax.ShapeDtypeStruct((n, w), F32), jax.ShapeDtypeStruct((n, w), F32),
                   jax.ShapeDtypeStruct((n, w), F32),
                   jax.ShapeDtypeStruct(s_ret.shape, F32), jax.ShapeDtypeStruct(s_hg.shape, F32)],
        compiler_params=_cparams(("parallel",)),
        name="lin_mix_sample",
    )(proj, cos, sin, lgam, retw, lbl, hgw, qnw, knw, s_ret, s_hg)


def _sb_consts(rows, tmask):
    j_i = _iota((LANE, 2 * LANE), 0)
    s_i = _iota((LANE, 2 * LANE), 1)
    ucat = _mask01((s_i >= LANE) | (j_i >= s_i))
    causal = _iota((rows, LANE), 1) < (_iota((rows, LANE), 0) & tmask)
    return ucat, causal


def _sb_prompt_kernel(bias_ref, q_ref, k_ref, v_ref, o_ref):
    i = pl.program_id(1)
    qb = SB_QBLK
    kpq = qb // ROW_BLK
    q = q_ref[...]
    left = _iota((ROW_BLK, LANE), 1) < HEAD_DIM
    zero = jnp.zeros((ROW_BLK, LANE), BF16)
    r2 = _iota((2 * LANE, 2 * LANE), 0)
    c2 = _iota((2 * LANE, 2 * LANE), 1)
    ubd = _mask01(((r2 >> 7) == (c2 >> 7)) & ((r2 & (LANE - 1)) > (c2 & (LANE - 1))))

    def split_heads(x):
        return jnp.concatenate([jnp.where(left, x, zero), jnp.where(left, zero, x)], axis=0)

    half = 2 * LANE
    sk_h = _iota((qb, half), 1) & (ROW_BLK - 1)
    tq_h = _iota((qb, half), 0)
    second = _iota((qb, half), 1) >= ROW_BLK
    bias_p = [jnp.where(second, bias_ref[2 * p + 1] * LOG2E, bias_ref[2 * p] * LOG2E) for p in range(2)]

    def steps(kbs, masks, carry0, carry1, acc0, acc1):
        work = [(n, p) for n in range(len(kbs)) for p in range(2)]
        blocks = []
        for kb in kbs:
            start = pl.multiple_of(kb * ROW_BLK, ROW_BLK)
            blocks.append((k_ref[pl.ds(start, ROW_BLK), :], v_ref[pl.ds(start, ROW_BLK), :]))
        zs = [_dot_nt(q[:, p * LANE:(p + 1) * LANE], split_heads(blocks[n][0][:, p * LANE:(p + 1) * LANE])) + bias_p[p]
              for n, p in work]
        sp_own = [_softplus2(z) for z in zs]
        sps = [s if masks[n] is None else jnp.where(masks[n], s, 0.0) for s, (n, _) in zip(sp_own, work)]
        ts = [jnp.dot(sp.astype(BF16), ubd, preferred_element_type=F32) for sp in sps]
        carries = [carry0, carry1]
        weights = []
        for (n, p), z, own, sp, t in zip(work, zs, sp_own, sps, ts):
            a = jnp.exp2((z - own) - (carries[p] + t))
            if masks[n] is not None:
                a = jnp.where(masks[n], a, 0.0)
            weights.append(a.astype(BF16))
            carries[p] = carries[p] + jnp.concatenate(
                [jnp.broadcast_to(jnp.sum(sp[:, h * ROW_BLK:(h + 1) * ROW_BLK], axis=-1, keepdims=True), (qb, ROW_BLK))
                 for h in range(2)], axis=1)
        accs = [acc0, acc1]
        for (n, p), ab in zip(work, weights):
            accs[p] = accs[p] + jnp.dot(ab, split_heads(blocks[n][1][:, p * LANE:(p + 1) * LANE]),
                                        preferred_element_type=F32)
        return carries[0], carries[1], accs[0], accs[1]

    zeros = jnp.zeros((qb, LANE), F32)
    state = (jnp.zeros((qb, half), F32), jnp.zeros((qb, half), F32), zeros, zeros)
    state = steps([i * kpq + d for d in reversed(range(kpq))],
                  [sk_h + d * ROW_BLK < tq_h for d in reversed(range(kpq))], *state)

    def earlier_blocks(j, c):
        return steps([(i - j) * kpq - 1 - d for d in range(kpq)], [None] * kpq, *c)

    state = lax.fori_loop(0, i, earlier_blocks, state)
    o_ref[...] = jnp.concatenate([state[2], state[3]], axis=1)


def _sb_prompt(bias, qs, kb, vb, nb, tp):
    n, w = qs.shape
    nq = tp // SB_QBLK
    return pl.pallas_call(
        _sb_prompt_kernel,
        grid=(nb, nq),
        in_specs=[pl.BlockSpec(memory_space=pltpu.SMEM),
                  pl.BlockSpec((SB_QBLK, w), lambda b, i: (b * nq + i, 0)),
                  pl.BlockSpec((tp, w), lambda b, i: (b, 0)),
                  pl.BlockSpec((tp, w), lambda b, i: (b, 0))],
        out_specs=pl.BlockSpec((SB_QBLK, w), lambda b, i: (b * nq + i, 0)),
        out_shape=jax.ShapeDtypeStruct((n, w), F32),
        compiler_params=_cparams(("parallel", "arbitrary")),
        name="sb_prompt",
    )(bias, qs, kb, vb)


def _sb_sample_kernel(pt_ref, bias_ref, q_ref, kn_ref, vn_ref, *rest, n_pages, dec_seq):
    del pt_ref
    k_pages = rest[:n_pages]
    v_pages = rest[n_pages:2 * n_pages]
    o_ref = rest[2 * n_pages]
    w = GROUP_W
    rows = N_HG * dec_seq
    q = q_ref[...]
    head_of_lane = _iota((dec_seq, w), 1) >> 6
    qbd = jnp.concatenate([jnp.where(head_of_lane == h, q, 0.0) for h in range(N_HG)], axis=0).astype(BF16)
    bias_col = jnp.concatenate([jnp.full((dec_seq, 1), bias_ref[h] * LOG2E, F32) for h in range(N_HG)], axis=0)
    ucat, causal = _sb_consts(rows, dec_seq - 1)
    pad = jnp.zeros((LANE - dec_seq, w), F32)
    k_new = jnp.concatenate([kn_ref[...], pad], axis=0)
    v_new = jnp.concatenate([vn_ref[...], pad], axis=0)
    key_blocks = [k_new] + [k_pages[p][...] for p in reversed(range(n_pages))]
    val_blocks = [v_new] + [v_pages[p][...] for p in reversed(range(n_pages))]
    zs = [_dot_nt(qbd, kb) + bias_col for kb in key_blocks]
    sps = [_softplus2(z) for z in zs]
    sps[0] = jnp.where(causal, sps[0], 0.0)
    ts = [_dotx(sp, ucat, 2) for sp in sps]
    carry = jnp.zeros((rows, LANE), F32)
    weights = []
    for z, t in zip(zs, ts):
        weights.append(jnp.exp2(z - (carry + t[:, :LANE])))
        carry = carry + t[:, LANE:]
    weights[0] = jnp.where(causal, weights[0], 0.0)
    acc = None
    for a, vb in zip(weights, val_blocks):
        term = _dot(a, vb)
        acc = term if acc is None else acc + term
    out = jnp.zeros((dec_seq, w), F32)
    for h in range(N_HG):
        out = out + jnp.where(head_of_lane == h, acc[h * dec_seq:(h + 1) * dec_seq], 0.0)
    o_ref[...] = out


def _sb_sample(page_table, bias, qs, kn, vn, cache_k, cache_v, layer, dec_seq):
    n, w = qs.shape
    db, n_pages = page_table.shape
    page = cache_k.shape[2]
    assert page == LANE and dec_seq % 8 == 0
    row = lambda b, pt: (b, 0)
    page_specs = [pl.BlockSpec((None, None, page, w), lambda b, pt, p=p: (layer, pt[b, p], 0, 0))
                  for p in range(n_pages)]
    grid_spec = pltpu.PrefetchScalarGridSpec(
        num_scalar_prefetch=1,
        grid=(db,),
        in_specs=[pl.BlockSpec(memory_space=pltpu.SMEM),
                  pl.BlockSpec((dec_seq, w), row), pl.BlockSpec((dec_seq, w), row), pl.BlockSpec((dec_seq, w), row)]
                 + page_specs + page_specs,
        out_specs=pl.BlockSpec((dec_seq, w), row),
    )
    return pl.pallas_call(
        functools.partial(_sb_sample_kernel, n_pages=n_pages, dec_seq=dec_seq),
        grid_spec=grid_spec,
        out_shape=jax.ShapeDtypeStruct((n, w), F32),
        compiler_params=_cparams(("arbitrary",)),
        name="sb_sample",
    )(page_table, bias, qs, kn, vn, *([cache_k] * n_pages), *([cache_v] * n_pages))


def _rwkv_prologue(rw, prev, mu, w0, w2p, a0, a2p, g2, kkp, ka, rk, ones):
    w = GROUP_W
    xm = rw + (prev - rw) * mu
    r = xm[:, 0:w]
    k = xm[:, w:2 * w]
    v = xm[:, 2 * w:3 * w]
    wa = xm[:, 3 * w:3 * w + LANE]
    gl = xm[:, 3 * w + LANE:]
    wd = w0 + _dot(jnp.tanh(wa), w2p)
    log_decay = -jnp.exp(-_softplus(-wd) - 0.5)
    a = _sigmoid(a0 + _dot(wa, a2p))
    g = _dot(_sigmoid(gl), g2)
    kk = k * kkp
    kk = kk * lax.rsqrt(jnp.maximum(_dotx(kk * kk, ones, 2), 1e-12))
    k_rw = k * (1.0 + (a - 1.0) * ka)
    bonus = _dotx(r * k_rw * rk, ones, 2) * v
    return r, log_decay, k_rw, v, kk, kk * a, g, bonus


RWKV_SUB = 64


def _rwkv_group_norm(o, lnw, lnb, bonus, g, ones):
    mu_h = _dotx(o, ones, 2) * (1.0 / HEAD_DIM)
    dlt = o - mu_h
    var = _dotx(dlt * dlt, ones, 2) * (1.0 / HEAD_DIM)
    return (dlt * lax.rsqrt(var + RWKV_LN_EPS) * lnw + lnb + bonus) * g


def _rwkv_chunk_kernel(rw_ref, mu_ref, w0_ref, w2p_ref, a0_ref, a2p_ref, g2_ref, kkp_ref, ka_ref, rk_ref, lnw_ref, lnb_ref,
                       o_ref, s_out_ref, carry_s, state_s, *, sub, nseq):
    prm = (mu_ref, w0_ref, w2p_ref, a0_ref, a2p_ref, g2_ref, kkp_ref, ka_ref, rk_ref, lnw_ref, lnb_ref)
    _lockstep([_rwkv_chunk_sequence(q, rw_ref, prm, o_ref, s_out_ref, carry_s, state_s, sub) for q in range(nseq)])


def _rwkv_chunk_sequence(seq, rw_ref, prm, o_ref, s_out_ref, carry_s, state_s, sub):
    mu_ref, w0_ref, w2p_ref, a0_ref, a2p_ref, g2_ref, kkp_ref, ka_ref, rk_ref, lnw_ref, lnb_ref = prm
    i = pl.program_id(0)
    w = GROUP_W
    rows = ROW_BLK
    nsub = rows // sub
    shift = int(math.log2(sub))
    ones = _head_ones(w)

    @pl.when(i == 0)
    def _():
        carry_s[seq] = jnp.zeros(carry_s.shape[1:], F32)
        state_s[seq] = jnp.zeros(state_s.shape[1:], F32)

    rw = rw_ref[seq]
    prev = jnp.where(_iota((rows, 1), 0) == 0, carry_s[seq, 0:1, :], pltpu.roll(rw, 1, 0))
    carry_s[seq, 0:1, :] = rw[rows - 1:rows, :]
    r, lw, k, v, kk, b, g, bonus = _rwkv_prologue(
        rw, prev, mu_ref[...], w0_ref[...], w2p_ref[...], a0_ref[...], a2p_ref[...], g2_ref[...],
        kkp_ref[...], ka_ref[...], rk_ref[...], ones)
    yield True

    r_i = _iota((rows, rows), 0)
    c_i = _iota((rows, rows), 1)
    same = (r_i >> shift) == (c_i >> shift)
    tri = _mask01(same & (c_i <= r_i))
    blk = _mask01(same)
    sub_of_row = _mask01((_iota((rows, LANE), 0) >> shift) == _iota((rows, LANE), 1))
    c = cl = dsum = None
    for part in _split(lw, 3):
        t1 = jnp.dot(tri, part, preferred_element_type=F32)
        t2 = jnp.dot(blk, part, preferred_element_type=F32)
        t3 = lax.dot_general(part, sub_of_row, (((0,), (0,)), ((), ())), preferred_element_type=F32)
        c = t1 if c is None else c + t1
        cl = t2 if cl is None else cl + t2
        dsum = t3 if dsum is None else dsum + t3
    yield True
    dcol = jnp.exp(dsum)
    kkd = kk * jnp.exp(c - lw)
    rd = r * jnp.exp(c)
    grow = jnp.exp(-c)
    kt = k * grow
    bt = b * grow
    tail = jnp.exp(cl - c)
    khat = k * tail
    bhat = b * tail

    left = _iota((rows, LANE), 1) < HEAD_DIM

    def split_heads(x):
        zero = jnp.zeros_like(x)
        return jnp.concatenate([jnp.where(left, x, zero), jnp.where(left, zero, x)], axis=0)

    lanes_p = lambda x, p: x[:, p * LANE:(p + 1) * LANE]
    wide = 2 * LANE
    t_i = _iota((rows, wide), 0)
    s_i = _iota((rows, wide), 1) & (rows - 1)
    same_sub = (t_i >> shift) == (s_i >> shift)
    incl = same_sub & (s_i <= t_i)
    strict = same_sub & (s_i < t_i)

    def dot3(x, y_hi, y_lo):
        x_hi, x_lo = _split(x, 2)
        nt = lambda a_, b_: lax.dot_general(a_, b_, (((1,), (1,)), ((), ())), preferred_element_type=F32)
        return nt(x_hi, y_hi) + nt(x_hi, y_lo) + nt(x_lo, y_hi)

    o_parts, y_parts, abr, abk = [], [], [], []
    for p in range(2):
        keys = jnp.concatenate([split_heads(lanes_p(kt, p)), split_heads(lanes_p(bt, p))], axis=0)
        keys_hi, keys_lo = _split(keys, 2)
        c_r = lax.dot_general(lanes_p(rd, p).astype(BF16), keys_hi, (((1,), (1,)), ((), ())),
                              preferred_element_type=F32)
        c_kk = dot3(lanes_p(kkd, p), keys_hi, keys_lo)
        akr = jnp.where(incl, c_r[:, :wide], 0.0)
        abr.append(jnp.where(incl, c_r[:, wide:], 0.0))
        akk = jnp.where(strict, c_kk[:, :wide], 0.0)
        abk.append(jnp.where(strict, c_kk[:, wide:], 0.0))
        v_heads = split_heads(lanes_p(v, p)).astype(BF16)
        o_parts.append(jnp.dot(akr.astype(BF16), v_heads, preferred_element_type=F32))
        y_parts.append(_dotx(akk, v_heads, 2))
    yield True
    o_acc = jnp.concatenate(o_parts, axis=1)
    y_intra = jnp.concatenate(y_parts, axis=1)

    def mm3(x, y):
        x_hi, x_lo = _split(x, 2)
        y_hi, y_lo = _split(y, 2)
        mm = lambda a_, b_: jnp.dot(a_, b_, preferred_element_type=F32)
        return mm(x_hi, y_hi) + mm(x_hi, y_lo) + mm(x_lo, y_hi)

    head_of_lane = _iota((rows, w), 1) >> 6
    eye = (_iota((rows, rows), 0) == _iota((rows, rows), 1)).astype(F32)
    powers = [-abk[h // 2][:, (h % 2) * rows:(h % 2 + 1) * rows] for h in range(N_HG)]
    invs = [eye + n for n in powers]
    for _ in range(shift - 1):
        powers = [mm3(n, n) for n in powers]
        yield True
        invs = [t + mm3(t, n) for t, n in zip(invs, powers)]
    yield True
    u_intra = jnp.zeros((rows, w), F32)
    g_mat = jnp.zeros((rows, w), F32)
    for h in range(N_HG):
        u_intra = u_intra + mm3(invs[h], jnp.where(head_of_lane == h, y_intra, 0.0))
        g_mat = g_mat + mm3(invs[h], jnp.where(head_of_lane == h, kkd, 0.0))
    yield True

    bdmask = (_iota((w, w), 0) >> 6) == (_iota((w, w), 1) >> 6)
    m = state_s[seq]
    us = []
    o_state = []
    for j in range(nsub):
        sl = slice(j * sub, (j + 1) * sub)
        from_state = _dot(jnp.concatenate([g_mat[sl], rd[sl]], axis=0), m)
        yield True
        u = u_intra[sl] + from_state[:sub]
        o_state.append(from_state[sub:])
        us.append(u)
        kv = _dot_tn(jnp.concatenate([khat[sl], -bhat[sl]], axis=0), jnp.concatenate([v[sl], u], axis=0))
        yield True
        m = m * dcol[:, j:j + 1] + jnp.where(bdmask, kv, 0.0)
    state_s[seq] = m
    u_all = jnp.concatenate(us, axis=0)
    o_u = jnp.concatenate([jnp.dot(abr[p].astype(BF16), split_heads(lanes_p(u_all, p)).astype(BF16),
                                   preferred_element_type=F32) for p in range(2)], axis=1)
    yield True
    o = o_acc + jnp.concatenate(o_state, axis=0) - o_u
    o_ref[seq] = _rwkv_group_norm(o, lnw_ref[...], lnb_ref[...], bonus, g, ones)

    @pl.when(i == pl.num_programs(0) - 1)
    def _():
        s_out_ref[seq] = m


def _rwkv_prompt(rw, prm, nb, nblk):
    w = GROUP_W
    cw = rw.shape[-1]
    const = lambda i: (0, 0)
    blk = lambda i: (0, i, 0)
    return pl.pallas_call(
        functools.partial(_rwkv_chunk_kernel, sub=RWKV_SUB, nseq=nb),
        grid=(nblk,),
        in_specs=[pl.BlockSpec((nb, ROW_BLK, cw), blk)] + [pl.BlockSpec(a.shape, const) for a in prm],
        out_specs=[pl.BlockSpec((nb, ROW_BLK, w), blk), pl.BlockSpec((nb, w, w), lambda i: (0, 0, 0))],
        out_shape=[jax.ShapeDtypeStruct((nb, nblk * ROW_BLK, w), F32), jax.ShapeDtypeStruct((nb, w, w), F32)],
        scratch_shapes=[pltpu.VMEM((nb, 8, cw), F32), pltpu.VMEM((nb, w, w), F32)],
        compiler_params=_cparams(("arbitrary",)),
        name="rwkv_prompt",
    )(rw, *prm)


def _rwkv_scan(tiles, tseq, s0, op_refs, vt_ref, acc_ref):
    g = len(tiles)
    hd = HEAD_DIM
    kk_ref, w_ref, bk_ref, kr_ref, rr_ref = op_refs
    ones_h = _head_ones(LANE)
    ones_f = jnp.ones((LANE, LANE), BF16)
    j2 = _mask01((_iota((LANE, 2 * LANE), 0) >> 6) == (_iota((LANE, 2 * LANE), 1) >> 7))
    lane_t = _iota((hd, LANE), 1)
    left = lane_t < hd
    x0 = [vt_ref[rb, p, 0:hd, :] for (rb, _, p) in tiles]
    x1 = [vt_ref[rb, p, hd:2 * hd, :] for (rb, _, p) in tiles]
    tile_rows = lambda a, i: a[i * hd:(i + 1) * hd]
    step_rows = 8

    def group(t8, s):
        base = t8 * step_rows
        blks = []
        for ref in op_refs:
            per_tile = []
            for (rb, j, p) in tiles:
                start = rb * ROW_BLK + j * tseq + base
                if not isinstance(start, int):
                    start = pl.multiple_of(start, step_rows)
                per_tile.append(ref[pl.ds(start, step_rows), pl.ds(p * LANE, LANE)])
            blks.append(per_tile)
        for u in range(step_rows):
            row = lambda q, i: blks[q][i][u:u + 1, :]
            s_t = [tile_rows(s, i) for i in range(g)]
            sa = _dotx(jnp.concatenate([s_t[i] * row(0, i) for i in range(g)], axis=0), ones_h, 2)
            msk = [lane_t == (j * tseq + base + u) for (_, j, _) in tiles]
            vsel = jnp.concatenate([jnp.where(msk[i], x0[i], 0.0) for i in range(g)]
                                   + [jnp.where(msk[i], x1[i], 0.0) for i in range(g)], axis=0)
            vc = _dotx(vsel, ones_f, 2)
            new = []
            for i in range(g):
                vcol = jnp.where(left, tile_rows(vc, i), tile_rows(vc, g + i))
                new.append(s_t[i] * row(1, i) - tile_rows(sa, i) * row(2, i) + vcol * row(3, i))
            ro = _dot(jnp.concatenate([new[i] * row(4, i) for i in range(g)], axis=0), j2)
            for i, (rb, _, p) in enumerate(tiles):
                r_i = tile_rows(ro, i)
                acc_ref[rb, p, 0:hd, :] = jnp.where(msk[i], r_i[:, :LANE], acc_ref[rb, p, 0:hd, :])
                acc_ref[rb, p, hd:2 * hd, :] = jnp.where(msk[i], r_i[:, LANE:], acc_ref[rb, p, hd:2 * hd, :])
            s = jnp.concatenate(new, axis=0)
        return s

    assert tseq % step_rows == 0
    if tseq == step_rows:
        return group(0, s0)
    return lax.fori_loop(0, tseq // step_rows, group, s0)


def _rwkv_sample_kernel(rw_ref, ovr_ref, s_in_ref, mu_ref, w0_ref, w2p_ref, a0_ref, a2p_ref, g2_ref, kkp_ref, ka_ref,
                        rk_ref, lnw_ref, lnb_ref, o_ref, s_out_ref,
                        kk_s, w_s, bk_s, kr_s, rr_s, vt_s, acc_s, *, tseq):
    w = GROUP_W
    hd = HEAD_DIM
    ones = _head_ones(w)
    nseq = ROW_BLK // tseq
    rw = rw_ref[0]
    first = (_iota((ROW_BLK, 1), 0) & (tseq - 1)) == 0
    prev = jnp.where(first, ovr_ref[0], pltpu.roll(rw, 1, 0))
    r, log_decay, k_rw, v, kk, bk, g, bonus = _rwkv_prologue(
        rw, prev, mu_ref[...], w0_ref[...], w2p_ref[...], a0_ref[...], a2p_ref[...], g2_ref[...],
        kkp_ref[...], ka_ref[...], rk_ref[...], ones)
    kk_s[...] = kk
    w_s[...] = jnp.exp(log_decay)
    bk_s[...] = bk
    kr_s[...] = k_rw
    rr_s[...] = r
    for p in range(2):
        vt_s[0, p] = v[:, p * LANE:(p + 1) * LANE].T
    acc_s[...] = jnp.zeros_like(acc_s)

    ops = (kk_s, w_s, bk_s, kr_s, rr_s)
    per = 8
    for grp in range(nseq // per):
        tiles = [(0, grp * per + jj, p) for jj in range(per) for p in range(2)]
        rows = pl.ds(grp * per * 2 * hd, per * 2 * hd)
        s_out_ref[rows, :] = _rwkv_scan(tiles, tseq, s_in_ref[rows, :], ops, vt_s, acc_s)

    o = jnp.concatenate([acc_s[0, p].T for p in range(2)], axis=1)
    o_ref[0] = _rwkv_group_norm(o, lnw_ref[...], lnb_ref[...], bonus, g, ones)


def _rwkv_sample(rw, ovr, s_in, prm, tseq):
    w = GROUP_W
    nblocks, _, cw = rw.shape
    nseq = ROW_BLK // tseq
    const = lambda i: (0, 0)
    pspecs = [pl.BlockSpec(a.shape, const) for a in prm]
    blk3 = lambda width: pl.BlockSpec((1, ROW_BLK, width), lambda i: (i, 0, 0))
    st_spec = pl.BlockSpec((nseq * 2 * HEAD_DIM, LANE), lambda i: (i, 0))
    scratch = ([pltpu.VMEM((ROW_BLK, w), F32) for _ in range(5)]
               + [pltpu.VMEM((1, 2, ROW_BLK, LANE), F32), pltpu.VMEM((1, 2, ROW_BLK, LANE), F32)])
    return pl.pallas_call(
        functools.partial(_rwkv_sample_kernel, tseq=tseq),
        grid=(nblocks,),
        in_specs=[blk3(cw), blk3(cw), st_spec] + pspecs,
        out_specs=[blk3(w), st_spec],
        out_shape=[jax.ShapeDtypeStruct((nblocks, ROW_BLK, w), F32), jax.ShapeDtypeStruct(s_in.shape, F32)],
        scratch_shapes=scratch,
        compiler_params=_cparams(("parallel",)),
        name="rwkv_sample",
    )(rw, ovr, s_in, *prm)


ROUTE_W = LANE


def _out_router_kernel(x_ref, oab_ref, oc_ref, od_ref, wo_ref, g2_ref, wr_hi_ref, wr_lo_ref, br_ref,
                       x1_ref, h_ref, gate_ref, *, n_experts, n_groups, null_rows, tm):
    w = GROUP_W
    x1 = (x_ref[...]
          + _dot(oab_ref[...], wo_ref[0:2 * w, :])
          + _dot(oc_ref[...], wo_ref[2 * w:3 * w, :])
          + _dot(od_ref[...], wo_ref[3 * w:4 * w, :]))
    if null_rows:
        rowg = pl.program_id(0) * tm + _iota((tm, 1), 0)
        null = rowg < 0
        for start in null_rows:
            null = null | ((rowg >= start) & (rowg < start + PAD_FRONT))
        x1 = jnp.where(null, 0.0, x1)
    x1_ref[...] = x1
    ms = jnp.mean(x1 * x1, axis=-1, keepdims=True)
    h = x1 * lax.rsqrt(ms + NORM_EPS) * g2_ref[...]
    h_hi = h.astype(BF16)
    h_lo = (h - h_hi.astype(F32)).astype(BF16)
    h_ref[...] = h_hi
    lg = (jnp.dot(h_hi, wr_hi_ref[...], preferred_element_type=F32)
          + jnp.dot(h_lo, wr_hi_ref[...], preferred_element_type=F32)
          + jnp.dot(h_hi, wr_lo_ref[...], preferred_element_type=F32)) + br_ref[...]
    lane = _iota(lg.shape, 1)
    big = jnp.int32(1 << 20)
    neg = jnp.float32(-jnp.inf)
    is_g = (lane >= n_experts) & (lane < n_experts + n_groups)
    gl = jnp.where(is_g, lg, neg)
    gmax = jnp.max(gl, axis=-1, keepdims=True)
    gidx = jnp.min(jnp.where(gl == gmax, lane, big), axis=-1, keepdims=True) - n_experts
    g_w = 1.0 / jnp.sum(jnp.where(is_g, jnp.exp(lg - gmax), 0.0), axis=-1, keepdims=True)
    per = n_experts // n_groups
    in_group = (lane >= gidx * per) & (lane < gidx * per + per)
    el = jnp.where(in_group, lg, neg)
    v1 = jnp.max(el, axis=-1, keepdims=True)
    i1 = jnp.min(jnp.where(el == v1, lane, big), axis=-1, keepdims=True)
    el2 = jnp.where(lane == i1, neg, el)
    v2 = jnp.max(el2, axis=-1, keepdims=True)
    i2 = jnp.min(jnp.where(el2 == v2, lane, big), axis=-1, keepdims=True)
    e21 = jnp.exp(v2 - v1)
    p1 = 1.0 / (1.0 + e21)
    p2 = e21 / (1.0 + e21)
    gate_ref[...] = jnp.where(lane == i1, p1 * g_w, 0.0) + jnp.where(lane == i2, p2 * g_w, 0.0)


def _out_router(x, oab, oc, od, wo, g2, wr_hi, wr_lo, br, n_experts, n_groups, null_rows):
    n, d = x.shape
    w = GROUP_W
    tm = 256 if n % 256 == 0 else ROW_BLK
    row = lambda i: (i, 0)
    const = lambda i: (0, 0)
    return pl.pallas_call(
        functools.partial(_out_router_kernel, n_experts=n_experts, n_groups=n_groups, null_rows=null_rows, tm=tm),
        grid=(n // tm,),
        in_specs=[pl.BlockSpec((tm, d), row), pl.BlockSpec((tm, 2 * w), row), pl.BlockSpec((tm, w), row),
                  pl.BlockSpec((tm, w), row), pl.BlockSpec(wo.shape, const), pl.BlockSpec((1, d), const),
                  pl.BlockSpec(wr_hi.shape, const), pl.BlockSpec(wr_lo.shape, const), pl.BlockSpec((1, ROUTE_W), const)],
        out_specs=[pl.BlockSpec((tm, d), row), pl.BlockSpec((tm, d), row), pl.BlockSpec((tm, ROUTE_W), row)],
        out_shape=[jax.ShapeDtypeStruct((n, d), F32), jax.ShapeDtypeStruct((n, d), BF16),
                   jax.ShapeDtypeStruct((n, ROUTE_W), F32)],
        compiler_params=_cparams(("parallel",)),
        name="out_router",
    )(x, oab, oc, od, wo, g2, wr_hi, wr_lo, br)


MOE_EXPERTS_PER_STEP = 2


def _moe_kernel(x1_ref, h_ref, gate_ref, w1_ref, w3_ref, w2_ref, y_ref):
    step = pl.program_id(1)

    @pl.when(step == 0)
    def _():
        y_ref[...] = x1_ref[...]

    h = h_ref[...]
    gates = gate_ref[...]
    lane = _iota(gates.shape, 1)
    hids = []
    for j in range(MOE_EXPERTS_PER_STEP):
        e = step * MOE_EXPERTS_PER_STEP + j
        ge = jnp.sum(jnp.where(lane == e, gates, 0.0), axis=-1, keepdims=True)
        up = jnp.dot(h, w1_ref[j].astype(BF16), preferred_element_type=F32)
        lin = jnp.dot(h, w3_ref[j].astype(BF16), preferred_element_type=F32)
        hids.append(((up * _sigmoid(up)) * lin * ge).astype(BF16))
    w2 = jnp.concatenate([w2_ref[j].astype(BF16) for j in range(MOE_EXPERTS_PER_STEP)], axis=0)
    y_ref[...] += jnp.dot(jnp.concatenate(hids, axis=1), w2, preferred_element_type=F32)


def _moe_tile(n):
    for t in (1536, 1408, 1280, 1024, 640, 512, 256, 128):
        if n % t == 0:
            return t
    raise ValueError(f"row count {n} is not a multiple of {ROW_BLK}")


def _moe(x1, h, gates, w1, w3, w2, layer):
    n, d = x1.shape
    n_experts, _, f = w1.shape[1:]
    tm = _moe_tile(n)
    eb = MOE_EXPERTS_PER_STEP
    assert n_experts % eb == 0
    row = lambda i, e: (i, 0)
    return pl.pallas_call(
        _moe_kernel,
        grid=(n // tm, n_experts // eb),
        in_specs=[pl.BlockSpec((tm, d), row), pl.BlockSpec((tm, d), row), pl.BlockSpec((tm, ROUTE_W), row),
                  pl.BlockSpec((None, eb, d, f), lambda i, e: (layer, e, 0, 0)),
                  pl.BlockSpec((None, eb, d, f), lambda i, e: (layer, e, 0, 0)),
                  pl.BlockSpec((None, eb, f, d), lambda i, e: (layer, e, 0, 0))],
        out_specs=pl.BlockSpec((tm, d), row),
        out_shape=jax.ShapeDtypeStruct((n, d), F32),
        compiler_params=_cparams(("parallel", "arbitrary")),
        name="moe",
    )(x1, h, gates, w1, w3, w2)


def _rope_tables(pos):
    half = HEAD_DIM // 2
    inv = ROPE_BASE ** (-jnp.arange(half, dtype=F32) / half)
    ang = pos.astype(F32)[:, None] * inv[None, :]
    reps = LANE // half
    return jnp.tile(jnp.cos(ang), (1, reps)), jnp.tile(jnp.sin(ang), (1, reps))


def _retention_log_gamma():
    lg = jnp.log1p(-jnp.exp2(-5.0 - jnp.arange(N_HG, dtype=F32)))
    return jnp.broadcast_to(jnp.repeat(lg, HEAD_DIM)[None, :], (ROW_BLK, GROUP_W))


def _row(v):
    return v.reshape(1, -1).astype(F32)


def _rwkv_params(p):
    w2 = p['rwkv_w2']
    a2 = p['rwkv_a2']
    w2p = jnp.concatenate([w2, jnp.zeros_like(a2)], axis=0)
    a2p = jnp.concatenate([jnp.zeros_like(w2), a2], axis=0)
    return (_row(p['rwkv_mu']), _row(p['rwkv_w0']), w2p, _row(p['rwkv_a0']), a2p, p['rwkv_g2'],
            _row(p['rwkv_kk']), _row(p['rwkv_ka']), _row(p['rwkv_rk']), _row(p['rwkv_ln_w']), _row(p['rwkv_ln_b']))


def _rwkv_state_to_tiles(s):
    n = s.shape[0]
    return s.reshape(n, 2, 2, HEAD_DIM, HEAD_DIM).transpose(0, 1, 3, 2, 4).reshape(n * 2 * HEAD_DIM, LANE)


def _rwkv_tiles_to_state(t, n):
    return t.reshape(n, 2, HEAD_DIM, 2, HEAD_DIM).transpose(0, 1, 3, 2, 4).reshape(n, N_HG, HEAD_DIM, HEAD_DIM)


def _diag_heads(s):
    return jnp.stack([s[:, h * HEAD_DIM:(h + 1) * HEAD_DIM, h * HEAD_DIM:(h + 1) * HEAD_DIM] for h in range(N_HG)], axis=1)


def kernel(x_prompt, x_sample, cache_sb_k, cache_sb_v, state_ret, state_hgrn, state_rwkv, state_rwkv_shift,
           page_table, meta_tokens, norm1, norm2, w_in, w_out, ret_norm, hgrn_lb_logits, hgrn_norm,
           sb_q_norm, sb_k_norm, sb_bias, rwkv_mu, rwkv_w0, rwkv_w2, rwkv_a0, rwkv_a2, rwkv_g2, rwkv_kk, rwkv_ka,
           rwkv_rk, rwkv_ln_w, rwkv_ln_b, moe_w_group, moe_b_group, moe_w_expert, moe_b_expert,
           moe_w1, moe_w3, moe_w2):
    bp, seq, dm = x_prompt.shape
    db, ds, _ = x_sample.shape
    depth = w_in.shape[0]
    w = GROUP_W
    assert seq % ROW_BLK == 0 and (db * ds) % ROW_BLK == 0 and ROW_BLK % ds == 0 and ds & (ds - 1) == 0
    assert w_in.shape[2] == LIN_COLS + 4 * w and cache_sb_k.shape[3] * cache_sb_k.shape[4] == w
    tp = seq + SB_QBLK
    nblk = tp // ROW_BLK
    tlen = seq + N_META
    past = page_table.shape[1] * cache_sb_k.shape[2]
    n_groups, e_per = moe_w_expert.shape[2:]
    n_experts = n_groups * e_per
    assert n_experts + n_groups <= ROUTE_W

    xp = jnp.concatenate([jnp.zeros((bp, PAD_FRONT, dm), F32),
                          jnp.broadcast_to(meta_tokens[None], (bp, N_META, dm)).astype(F32), x_prompt], axis=1)
    xp = xp.reshape(bp * tp, dm)
    xs = x_sample.reshape(db * ds, dm)
    null_rows = tuple(b * tp for b in range(bp))

    cos_p, sin_p = _rope_tables(jnp.maximum(jnp.arange(tp) - PAD_FRONT, 0))
    cos_s, sin_s = _rope_tables(jnp.tile(past + jnp.arange(ds), ROW_BLK // ds))
    lgam = _retention_log_gamma()
    cache_k = cache_sb_k.reshape(cache_sb_k.shape[:3] + (w,))
    cache_v = cache_sb_v.reshape(cache_sb_v.shape[:3] + (w,))
    tile_heads = lambda v: jnp.tile(v, N_HG)[None, :].astype(F32)

    prompt_rows, sample_rows = [], []
    for l in range(depth):
        wa = w_in[l][:, :LIN_COLS].astype(BF16)
        wb = w_in[l][:, LIN_COLS:].astype(BF16)
        g1 = _row(norm1[l])
        lin_args = (lgam, _row(ret_norm[l]), hgrn_lb_logits.astype(F32), _row(hgrn_norm[l]),
                    tile_heads(sb_q_norm[l]), tile_heads(sb_k_norm[l]))
        rwkv_prm = _rwkv_params(dict(rwkv_mu=rwkv_mu[l], rwkv_w0=rwkv_w0[l], rwkv_w2=rwkv_w2[l], rwkv_a0=rwkv_a0[l],
                                     rwkv_a2=rwkv_a2[l], rwkv_g2=rwkv_g2[l], rwkv_kk=rwkv_kk[l], rwkv_ka=rwkv_ka[l],
                                     rwkv_rk=rwkv_rk[l], rwkv_ln_w=rwkv_ln_w[l], rwkv_ln_b=rwkv_ln_b[l]))
        wo = w_out[l].astype(BF16)
        wr = jnp.zeros((dm, ROUTE_W), F32)
        wr = wr.at[:, :n_experts].set(moe_w_expert[l].reshape(dm, n_experts))
        wr = wr.at[:, n_experts:n_experts + n_groups].set(moe_w_group[l])
        wr_hi = wr.astype(BF16)
        wr_lo = (wr - wr_hi.astype(F32)).astype(BF16)
        br = jnp.zeros((1, ROUTE_W), F32)
        br = br.at[0, :n_experts].set(moe_b_expert[l].reshape(n_experts))
        br = br.at[0, n_experts:n_experts + n_groups].set(moe_b_group[l])
        router = (wo, _row(norm2[l]), wr_hi, wr_lo, br)

        pa, pb = _in_proj(xp, g1, wa, wb, tp)
        oab, qs, knf, knb, vb, s_ret, s_hg = _lin_mix_prompt(pa, cos_p, sin_p, *lin_args, l, bp, nblk)
        oc = _sb_prompt(sb_bias[l].astype(F32), qs, knb, vb, bp, tp)
        od, s_rw = _rwkv_prompt(pb, rwkv_prm, bp, nblk)
        x1, h2, gates = _out_router(xp, oab, oc, od.reshape(bp * tp, w), *router, n_experts, n_groups, null_rows)
        xp = _moe(x1, h2, gates, moe_w1, moe_w3, moe_w2, l)
        real = lambda a: a.reshape(bp, tp, -1)[:, PAD_FRONT:]
        prompt_rows.append((real(knf).reshape(bp, tlen, N_HG, HEAD_DIM),
                            real(pa[:, LIN_COLS - w:]).reshape(bp, tlen, N_HG, HEAD_DIM),
                            _diag_heads(s_ret), _diag_heads(s_hg), _diag_heads(s_rw).swapaxes(-1, -2),
                            pb[:, -1]))

        sa, sb = _in_proj(xs, g1, wa, wb, ROW_BLK)
        stack = lambda s: s.reshape(db, w, HEAD_DIM)
        oab, qs, knf, vf, s_ret, s_hg = _lin_mix_sample(sa, cos_s, sin_s, *lin_args, stack(state_ret[l]),
                                                        stack(state_hgrn[l]), l, ds)
        oc = _sb_sample(page_table, sb_bias[l].astype(F32), qs, knf, vf, cache_k, cache_v, l, ds)
        cw = sb.shape[-1]
        ovr = jnp.concatenate([state_rwkv_shift[l][:, None, :], jnp.zeros((db, ds - 1, cw), F32)], axis=1)
        od, s_rw = _rwkv_sample(sb, ovr.reshape(-1, ROW_BLK, cw),
                                _rwkv_state_to_tiles(state_rwkv[l]), rwkv_prm, ds)
        x1, h2, gates = _out_router(xs, oab, oc, od.reshape(db * ds, w), *router, n_experts, n_groups, ())
        xs = _moe(x1, h2, gates, moe_w1, moe_w3, moe_w2, l)
        sample_rows.append((knf.reshape(db, ds, N_HG, HEAD_DIM), vf.reshape(db, ds, N_HG, HEAD_DIM),
                            s_ret.reshape(db, N_HG, HEAD_DIM, HEAD_DIM), s_hg.reshape(db, N_HG, HEAD_DIM, HEAD_DIM),
                            _rwkv_tiles_to_state(s_rw, db), sb.reshape(db, ds, cw)[:, -1]))

    y_prompt = xp.reshape(bp, tp, dm)[:, SB_QBLK:]
    y_sample = xs.reshape(db, ds, dm)
    stacked_p = [jnp.stack(r) for r in zip(*prompt_rows)]
    stacked_s = [jnp.stack(r) for r in zip(*sample_rows)]
    return (y_prompt, y_sample, *stacked_p, *stacked_s)
```
